```python
import math
import jax, jax.numpy as jnp
from jax import lax
import numpy as np

D_MODEL = 2048
BATCH = 8
SEQ = 2048
DEPTH = 1
DEC_BATCH = 128
DEC_SEQ = 8
PAST_LEN = 2048
PAGE_SIZE = 128

MIX_W = D_MODEL
NSA_W = MIX_W // 2
POOL_W = MIX_W - NSA_W
HEAD_DIM = 64
N_HEADS = NSA_W // HEAD_DIM
N_KV = 4
GROUP = N_HEADS // N_KV
KV_W = N_KV * HEAD_DIM
CMP_BLOCK = 32
CMP_STRIDE = 16
SEL_BLOCK = 64
SEL_TOP = 16
WINDOW = 512
Q_BLOCK = 64
POOL_SIZES = (2, 4, 8, 16)
N_POOL = len(POOL_SIZES)
POOL_GW = POOL_W // N_POOL
POOL_BUF = max(POOL_SIZES) - 1
PEER_HEADS = 8
PEER_QDIM = 256
N_KEYS = 128
N_EXPERTS = N_KEYS * N_KEYS
PEER_TOPK = 16
PEER_TBLOCK = 128
IN_COLS = NSA_W + 3 * 2 * KV_W + 3 * N_HEADS + POOL_W
EPS = 1e-6
NEG = -1e30
FORCE_BONUS = 1e4

kernel_name = 'hymba_nsa_pool_peer_step'


def rms_norm(x, g):
    xf = x.astype(jnp.float32)
    y = xf * lax.rsqrt(jnp.mean(xf * xf, axis=-1, keepdims=True) + EPS)
    return (y * g.astype(jnp.float32)).astype(x.dtype)


def masked_softmax(s, mask, axis):
    s = jnp.where(mask, s, NEG)
    e = jnp.where(mask, jnp.exp(s - jnp.max(s, axis=axis, keepdims=True)), 0.0)
    return e / jnp.maximum(jnp.sum(e, axis=axis, keepdims=True), 1e-30)


def alibi_slopes():
    h = np.arange(1, N_HEADS + 1, dtype=np.float32)
    return jnp.asarray((2.0 ** (-8.0 * h / N_HEADS)).reshape(N_KV, GROUP), jnp.float32)


def adaln(c, w_ada, b_ada):
    m = jax.nn.silu(c) @ w_ada + b_ada
    return m.reshape(c.shape[0], 6, D_MODEL)


def modulate(x, shift, scale, g):
    return rms_norm(x, g) * (1.0 + scale[:, None, :]) + shift[:, None, :]


def mix_inputs(h, w_in, g_qnorm, g_knorm):
    B, T, _ = h.shape
    z = h @ w_in
    offs = [NSA_W, NSA_W + 2 * KV_W, NSA_W + 4 * KV_W, NSA_W + 6 * KV_W, NSA_W + 6 * KV_W + 3 * N_HEADS]
    q, kvc, kvs, kvw, gl, u = jnp.split(z, offs, axis=-1)
    q = rms_norm(q.reshape(B, T, N_KV, GROUP, HEAD_DIM), g_qnorm)
    kvc = kvc.reshape(B, T, 2, N_KV, HEAD_DIM)
    kvs = kvs.reshape(B, T, 2, N_KV, HEAD_DIM)
    kvw = kvw.reshape(B, T, 2, N_KV, HEAD_DIM)
    kvs = jnp.stack([rms_norm(kvs[:, :, 0], g_knorm[1]), kvs[:, :, 1]], axis=2)
    kvw = jnp.stack([rms_norm(kvw[:, :, 0], g_knorm[2]), kvw[:, :, 1]], axis=2)
    gates = jax.nn.sigmoid(gl.reshape(B, T, N_KV, GROUP, 3))
    return q, kvc, kvs, kvw, gates, u


def compressed_kv(kv_rows, cmp_pe, w_cmp, g_k):
    T = kv_rows.shape[1]
    nc = (T - CMP_BLOCK) // CMP_STRIDE + 1
    idx = np.arange(nc)[:, None] * CMP_STRIDE + np.arange(CMP_BLOCK)[None, :]
    blk = kv_rows[:, idx] + jnp.swapaxes(cmp_pe, 0, 1)
    comp = jnp.einsum('bnlcgd,cglde->bncge', blk, w_cmp)
    kc = rms_norm(comp[:, :, 0], g_k)
    vc = comp[:, :, 1]
    pos_c = jnp.asarray(np.arange(nc) * CMP_STRIDE + CMP_BLOCK - 1, jnp.int32)
    return kc, vc, pos_c


def select_blocks(kv_rows):
    B, T = kv_rows.shape[:2]
    ns = -(-T // SEL_BLOCK)
    kv = jnp.pad(kv_rows, ((0, 0), (0, ns * SEL_BLOCK - T), (0, 0), (0, 0), (0, 0)))
    kv = kv.reshape(B, ns, SEL_BLOCK, 2, N_KV, HEAD_DIM)
    return kv[:, :, :, 0], kv[:, :, :, 1]


def cmp_to_sel(nc, ns):
    pos = np.arange(nc)[:, None] * CMP_STRIDE + np.arange(CMP_BLOCK)[None, :]
    sel = pos // SEL_BLOCK
    return jnp.asarray((sel[:, :, None] == np.arange(ns)[None, None, :]).mean(axis=1), jnp.float32)


def nsa_block(q, pos_q, gates, kc, vc, pos_c, ks_blk, vs_blk, kw, vw, pos_w, slopes):
    B, Tq = q.shape[:2]
    scale = HEAD_DIM ** -0.5
    f32 = jnp.float32
    dist_c = (pos_q[:, None] - pos_c[None, :]).astype(f32)
    s_c = jnp.einsum('bqgrd,bngd->bgrqn', q, kc, preferred_element_type=f32) * scale \
        - slopes[None, :, :, None, None] * dist_c
    p_c = masked_softmax(s_c, dist_c >= 0, -1)
    o_c = jnp.einsum('bgrqn,bngd->bqgrd', p_c, vc.astype(f32))
    ns = ks_blk.shape[1]
    imp = jnp.einsum('bgrqn,ns->bqgs', p_c, cmp_to_sel(kc.shape[1], ns))
    sel_ids = jnp.arange(ns)
    blk_q = pos_q // SEL_BLOCK
    forced = (sel_ids[None, :] == 0) | (sel_ids[None, :] == blk_q[:, None]) | (sel_ids[None, :] == blk_q[:, None] - 1)
    valid = sel_ids[None, :] * SEL_BLOCK <= pos_q[:, None]
    imp = jnp.where(valid[None, :, None, :], imp + FORCE_BONUS * forced[None, :, None, :].astype(f32), NEG)
    _, idx = lax.top_k(imp, min(SEL_TOP, ns))
    b_i = jnp.arange(B)[:, None, None, None]
    g_i = jnp.arange(N_KV)[None, None, :, None]
    kg = ks_blk[b_i, idx, :, g_i]
    vg = vs_blk[b_i, idx, :, g_i]
    pos_s = idx[..., None] * SEL_BLOCK + jnp.arange(SEL_BLOCK)
    dist_s = (pos_q[None, :, None, None, None] - pos_s).astype(f32)[:, :, :, None]
    s_s = jnp.einsum('bqgrd,bqgnkd->bqgrnk', q, kg, preferred_element_type=f32) * scale \
        - slopes[None, None, :, :, None, None] * dist_s
    p_s = masked_softmax(s_s, dist_s >= 0, (-2, -1))
    o_s = jnp.einsum('bqgrnk,bqgnkd->bqgrd', p_s, vg.astype(f32))
    dist_w = (pos_q[:, None] - pos_w[None, :])
    mask_w = (dist_w >= 0) & (dist_w <= WINDOW) & (pos_w[None, :] >= 0)
    s_w = jnp.einsum('bqgrd,bkgd->bgrqk', q, kw, preferred_element_type=f32) * scale \
        - slopes[None, :, :, None, None] * dist_w.astype(f32)
    p_w = masked_softmax(s_w, mask_w, -1)
    o_w = jnp.einsum('bgrqk,bkgd->bqgrd', p_w, vw.astype(f32))
    g = gates.astype(f32)
    out = g[..., 0:1] * o_c + g[..., 1:2] * o_s + g[..., 2:3] * o_w
    return out.astype(q.dtype)


def nsa_prompt(q, gates, kvc, kvs, kvw, cmp_pe, w_cmp, g_knorm, slopes):
    B, T = q.shape[:2]
    kc, vc, pos_c = compressed_kv(kvc, cmp_pe, w_cmp, g_knorm[0])
    ks_blk, vs_blk = select_blocks(kvs)
    kvw_pad = jnp.pad(kvw, ((0, 0), (WINDOW, 0), (0, 0), (0, 0), (0, 0)))
    qb = min(Q_BLOCK, T)

    def body(i):
        s = i * qb
        qi = lax.dynamic_slice_in_dim(q, s, qb, axis=1)
        gi = lax.dynamic_slice_in_dim(gates, s, qb, axis=1)
        wi = lax.dynamic_slice_in_dim(kvw_pad, s, qb + WINDOW, axis=1)
        pos_q = s + jnp.arange(qb)
        pos_w = s - WINDOW + jnp.arange(qb + WINDOW)
        return nsa_block(qi, pos_q, gi, kc, vc, pos_c, ks_blk, vs_blk, wi[:, :, 0], wi[:, :, 1], pos_w, slopes)

    out = lax.map(body, jnp.arange(T // qb))
    return jnp.moveaxis(out, 0, 1).reshape(B, T, N_KV, GROUP, HEAD_DIM)


def nsa_sample(q, gates, kvc, kvs, kvw, cache_kv_cmp, cache_kv_slc, cache_kv_win, page_table,
               cmp_pe, w_cmp, g_knorm, slopes):
    B, T = q.shape[:2]
    past_len = page_table.shape[1] * PAGE_SIZE
    full_c = jnp.concatenate([cache_kv_cmp[page_table].reshape((B, past_len) + cache_kv_cmp.shape[2:]), kvc], axis=1)
    full_s = jnp.concatenate([cache_kv_slc[page_table].reshape((B, past_len) + cache_kv_slc.shape[2:]), kvs], axis=1)
    kc, vc, pos_c = compressed_kv(full_c, cmp_pe, w_cmp, g_knorm[0])
    ks_blk, vs_blk = select_blocks(full_s)
    w_buf = cache_kv_win.shape[1]
    w_ctx = jnp.concatenate([cache_kv_win, kvw], axis=1)
    pos_q = past_len + jnp.arange(T)
    pos_w = past_len - w_buf + jnp.arange(w_buf + T)
    out = nsa_block(q, pos_q, gates, kc, vc, pos_c, ks_blk, vs_blk, w_ctx[:, :, 0], w_ctx[:, :, 1], pos_w, slopes)
    return out, w_ctx[:, -w_buf:]


def pool_mix(u, prefix, pos, w_pool, pool_scale):
    B, T, _ = u.shape
    ext = jnp.concatenate([prefix, u], axis=1).astype(jnp.float32)
    cs = jnp.concatenate([jnp.zeros((B, 1, POOL_W), jnp.float32), jnp.cumsum(ext, axis=1)], axis=1)
    outs = []
    for gi, w in enumerate(POOL_SIZES):
        c0, c1 = gi * POOL_GW, (gi + 1) * POOL_GW
        hi = cs[:, POOL_BUF + 1:POOL_BUF + 1 + T, c0:c1]
        lo = cs[:, POOL_BUF + 1 - w:POOL_BUF + 1 - w + T, c0:c1]
        cnt = jnp.minimum(w, pos + 1).astype(jnp.float32)[None, :, None]
        outs.append((hi - lo) / cnt - ext[:, POOL_BUF:, c0:c1])
    d = jnp.stack(outs, axis=2)
    y = jnp.einsum('btgc,gce->btge', d, w_pool.astype(jnp.float32)).reshape(B, T, POOL_W)
    return (y * pool_scale.astype(jnp.float32)).astype(u.dtype)


def peer(h, peer_wq, peer_subkeys, peer_u, peer_v):
    B, T, D = h.shape
    n = B * T
    nb = -(-n // PEER_TBLOCK)
    xt = jnp.pad(h.reshape(n, D), ((0, nb * PEER_TBLOCK - n), (0, 0))).reshape(nb, PEER_TBLOCK, D)

    def body(xb):
        q = (xb @ peer_wq).reshape(PEER_TBLOCK, PEER_HEADS, 2, PEER_QDIM // 2)
        s = jnp.einsum('thpd,phkd->thpk', q, peer_subkeys, preferred_element_type=jnp.float32)
        s1, i1 = lax.top_k(s[:, :, 0], PEER_TOPK)
        s2, i2 = lax.top_k(s[:, :, 1], PEER_TOPK)
        cand = (s1[..., :, None] + s2[..., None, :]).reshape(PEER_TBLOCK, PEER_HEADS, PEER_TOPK * PEER_TOPK)
        cid = (i1[..., :, None] * N_KEYS + i2[..., None, :]).reshape(PEER_TBLOCK, PEER_HEADS, PEER_TOPK * PEER_TOPK)
        sc, j = lax.top_k(cand, PEER_TOPK)
        eid = jnp.take_along_axis(cid, j, axis=-1)
        g = jax.nn.softmax(sc, axis=-1)
        a = jax.nn.gelu(jnp.einsum('td,thkd->thk', xb, peer_u[eid], preferred_element_type=jnp.float32),
                        approximate=False)
        return jnp.einsum('thk,thkd->td', (g * a).astype(peer_v.dtype), peer_v[eid])

    y = lax.map(body, xt).reshape(nb * PEER_TBLOCK, D)[:n]
    return y.reshape(B, T, D).astype(h.dtype)


def mix_out(o_nsa, o_pool, w_out):
    B, T = o_pool.shape[:2]
    return jnp.concatenate([o_nsa.reshape(B, T, NSA_W), o_pool], axis=-1) @ w_out


def setup_inputs(seed: int = 0) -> dict:
    key = jax.random.key(seed)
    ks = jax.random.split(key, 26)
    n_pages = PAST_LEN // PAGE_SIZE
    n_phys = (DEC_BATCH * n_pages * 5) // 4
    w_buf = min(WINDOW, PAST_LEN)
    nrm = jax.random.normal
    page_table = jax.random.permutation(ks[6], n_phys)[:DEC_BATCH * n_pages].reshape(DEC_BATCH, n_pages).astype(jnp.int32)
    return {
        'x_prompt': nrm(ks[0], (BATCH, SEQ, D_MODEL), jnp.float32),
        'x_sample': nrm(ks[1], (DEC_BATCH, DEC_SEQ, D_MODEL), jnp.float32),
        'cache_kv_cmp': nrm(ks[2], (n_phys, PAGE_SIZE, 2, N_KV, HEAD_DIM), jnp.float32),
        'cache_kv_slc': nrm(ks[3], (n_phys, PAGE_SIZE, 2, N_KV, HEAD_DIM), jnp.float32),
        'cache_kv_win': nrm(ks[4], (DEC_BATCH, w_buf, 2, N_KV, HEAD_DIM), jnp.float32),
        'state_pool': nrm(ks[5], (DEC_BATCH, POOL_BUF, POOL_W), jnp.float32),
        'page_table': page_table,
        'c_prompt': nrm(ks[7], (BATCH, D_MODEL), jnp.float32),
        'c_sample': nrm(ks[8], (DEC_BATCH, D_MODEL), jnp.float32),
        'w_ada': nrm(ks[9], (D_MODEL, 6 * D_MODEL), jnp.float32) * (0.5 * D_MODEL ** -0.5),
        'b_ada': nrm(ks[10], (6 * D_MODEL,), jnp.float32) * 0.01,
        'g_norm1': 1.0 + 0.1 * nrm(ks[11], (D_MODEL,), jnp.float32),
        'g_norm2': 1.0 + 0.1 * nrm(ks[12], (D_MODEL,), jnp.float32),
        'w_in': nrm(ks[13], (D_MODEL, IN_COLS), jnp.float32) * D_MODEL ** -0.5,
        'w_out': nrm(ks[14], (MIX_W, D_MODEL), jnp.float32) * MIX_W ** -0.5,
        'g_qnorm': 1.0 + 0.1 * nrm(ks[15], (HEAD_DIM,), jnp.float32),
        'g_knorm': 1.0 + 0.1 * nrm(ks[16], (3, HEAD_DIM), jnp.float32),
        'cmp_pe': 0.5 * nrm(ks[17], (2, CMP_BLOCK, N_KV, HEAD_DIM), jnp.float32),
        'w_cmp': nrm(ks[18], (2, N_KV, CMP_BLOCK, HEAD_DIM, HEAD_DIM), jnp.float32) * (CMP_BLOCK * HEAD_DIM) ** -0.5,
        'w_pool': nrm(ks[19], (N_POOL, POOL_GW, POOL_GW), jnp.float32) * POOL_GW ** -0.5,
        'pool_scale': 1.0 + 0.1 * nrm(ks[20], (POOL_W,), jnp.float32),
        'peer_wq': nrm(ks[21], (D_MODEL, PEER_HEADS * PEER_QDIM), jnp.float32) * D_MODEL ** -0.5,
        'peer_subkeys': nrm(ks[22], (2, PEER_HEADS, N_KEYS, PEER_QDIM // 2), jnp.float32) * (PEER_QDIM // 2) ** -0.5,
        'peer_u': nrm(ks[23], (N_EXPERTS, D_MODEL), jnp.float32) * D_MODEL ** -0.5,
        'peer_v': nrm(ks[24], (N_EXPERTS, D_MODEL), jnp.float32) * (PEER_HEADS * PEER_TOPK) ** -0.5,
    }


def reference(x_prompt, x_sample, cache_kv_cmp, cache_kv_slc, cache_kv_win, state_pool, page_table,
              c_prompt, c_sample, w_ada, b_ada, g_norm1, g_norm2, w_in, w_out, g_qnorm, g_knorm,
              cmp_pe, w_cmp, w_pool, pool_scale, peer_wq, peer_subkeys, peer_u, peer_v):
    slopes = alibi_slopes()

    Bp, Tp = x_prompt.shape[:2]
    mod_p = adaln(c_prompt, w_ada, b_ada)
    h = modulate(x_prompt, mod_p[:, 0], mod_p[:, 1], g_norm1)
    q, kvc_p, kvs_p, kvw_p, gates, u_p = mix_inputs(h, w_in, g_qnorm, g_knorm)
    o_nsa = nsa_prompt(q, gates, kvc_p, kvs_p, kvw_p, cmp_pe, w_cmp, g_knorm, slopes)
    o_pool = pool_mix(u_p, jnp.zeros((Bp, POOL_BUF, POOL_W), u_p.dtype), jnp.arange(Tp), w_pool, pool_scale)
    x1 = x_prompt + mod_p[:, 2][:, None, :] * mix_out(o_nsa, o_pool, w_out)
    h2 = modulate(x1, mod_p[:, 3], mod_p[:, 4], g_norm2)
    y_prompt = x1 + mod_p[:, 5][:, None, :] * peer(h2, peer_wq, peer_subkeys, peer_u, peer_v)
    win_p = kvw_p[:, -min(WINDOW, Tp):]
    pool_p = u_p[:, -POOL_BUF:]

    Ts = x_sample.shape[1]
    past_len = page_table.shape[1] * PAGE_SIZE
    mod_s = adaln(c_sample, w_ada, b_ada)
    h = modulate(x_sample, mod_s[:, 0], mod_s[:, 1], g_norm1)
    q, kvc_s, kvs_s, kvw_s, gates, u_s = mix_inputs(h, w_in, g_qnorm, g_knorm)
    o_nsa, win_s = nsa_sample(q, gates, kvc_s, kvs_s, kvw_s, cache_kv_cmp, cache_kv_slc, cache_kv_win,
                              page_table, cmp_pe, w_cmp, g_knorm, slopes)
    o_pool = pool_mix(u_s, state_pool, past_len + jnp.arange(Ts), w_pool, pool_scale)
    x1 = x_sample + mod_s[:, 2][:, None, :] * mix_out(o_nsa, o_pool, w_out)
    h2 = modulate(x1, mod_s[:, 3], mod_s[:, 4], g_norm2)
    y_sample = x1 + mod_s[:, 5][:, None, :] * peer(h2, peer_wq, peer_subkeys, peer_u, peer_v)
    pool_s = jnp.concatenate([state_pool, u_s], axis=1)[:, -POOL_BUF:]

    return (y_prompt, y_sample, kvc_p, kvs_p, win_p, pool_p, kvc_s, kvs_s, win_s, pool_s)
```

```python
import functools

import numpy as np
import jax
import jax.numpy as jnp
from jax import lax
from jax.experimental import pallas as pl
from jax.experimental.pallas import tpu as pltpu

F32, BF16, I32 = jnp.float32, jnp.bfloat16, jnp.int32

HEAD_DIM = 64
N_KV = 4
GROUP = 4
N_HEADS = N_KV * GROUP
CMP_BLOCK = 32
CMP_STRIDE = 16
SEL_BLOCK = 64
SEL_TOP = 16
WINDOW = 512
POOL_SIZES = (2, 4, 8, 16)
POOL_BUF = max(POOL_SIZES) - 1
PEER_HEADS = 8
N_KEYS = 128
PEER_TOPK = 16
EPS = 1e-6
NEG = -1e30
FORCE_BONUS = 1e4
KV_W = N_KV * HEAD_DIM
VMEM_LIMIT = 56 * 1024 * 1024


def _cp(sem, vmem=VMEM_LIMIT):
    return pltpu.CompilerParams(dimension_semantics=sem, vmem_limit_bytes=vmem)


def _dot(a, b):
    return jnp.dot(a, b, preferred_element_type=F32)


def _nt(a, b):
    return lax.dot_general(a, b, (((1,), (1,)), ((), ())), preferred_element_type=F32)


def _tn(a, b):
    return lax.dot_general(a, b, (((0,), (0,)), ((), ())), preferred_element_type=F32)


def _split_dot(x, w):
    hi = x.astype(BF16)
    r1 = x - hi.astype(F32)
    mid = r1.astype(BF16)
    lo = (r1 - mid.astype(F32)).astype(BF16)
    return _dot(hi, w) + _dot(mid, w) + _dot(lo, w)


def _ones64():
    r = lax.broadcasted_iota(I32, (KV_W, KV_W), 0) // HEAD_DIM
    c = lax.broadcasted_iota(I32, (KV_W, KV_W), 1) // HEAD_DIM
    return (r == c).astype(BF16)


def _rms64(x, ones):
    ms = _split_dot(x * x, ones) * (1.0 / HEAD_DIM)
    return x * lax.rsqrt(ms + EPS)


def _rms_rows(x):
    return x * lax.rsqrt(jnp.mean(x * x, axis=-1, keepdims=True) + EPS)


def _adaln_body(c_ref, w_ref, b_ref, o_ref):
    c = c_ref[...]
    s = (c * jax.nn.sigmoid(c)).astype(BF16)
    o_ref[...] = _dot(s, w_ref[...].astype(BF16)) + b_ref[...]


def adaln(c_all, w_ada, b_ada, tn=1024):
    m, d = c_all.shape
    n = w_ada.shape[1]
    return pl.pallas_call(
        _adaln_body,
        out_shape=jax.ShapeDtypeStruct((m, n), F32),
        grid=(n // tn,),
        in_specs=[pl.BlockSpec((m, d), lambda j: (0, 0)),
                  pl.BlockSpec((d, tn), lambda j: (0, j)),
                  pl.BlockSpec((1, tn), lambda j: (0, j))],
        out_specs=pl.BlockSpec((m, tn), lambda j: (0, j)),
        compiler_params=_cp(("arbitrary",)),
        name="adaln",
    )(c_all, w_ada, b_ada.reshape(1, n))


def _inproj_body(x_ref, sh_ref, sc_ref, g1_ref, wq_ref, wkv_ref, wg_ref, wu_ref, gq_ref, gk_ref,
                 q_ref, kvc_ref, kvs_ref, kvw_ref, gt_ref, u_ref):
    h = _rms_rows(x_ref[...]) * g1_ref[...]
    h = h * (1.0 + sc_ref[0]) + sh_ref[0]
    hb = h.astype(BF16)
    ones = _ones64()
    zq = _dot(hb, wq_ref[...])
    for r in range(GROUP):
        sl = slice(r * KV_W, (r + 1) * KV_W)
        q_ref[:, sl] = _rms64(zq[:, sl], ones) * gq_ref[:, sl] * (HEAD_DIM ** -0.5)
    zkv = _dot(hb, wkv_ref[...])
    kvc_ref[...] = zkv[:, 0:2 * KV_W]
    kvs_ref[:, 0:KV_W] = _rms64(zkv[:, 2 * KV_W:3 * KV_W], ones) * gk_ref[1:2, :]
    kvs_ref[:, KV_W:] = zkv[:, 3 * KV_W:4 * KV_W]
    kvw_ref[:, 0:KV_W] = _rms64(zkv[:, 4 * KV_W:5 * KV_W], ones) * gk_ref[2:3, :]
    kvw_ref[:, KV_W:] = zkv[:, 5 * KV_W:6 * KV_W]
    gt_ref[...] = jax.nn.sigmoid(_dot(hb, wg_ref[...]))
    u_ref[...] = _dot(hb, wu_ref[...])


def _mod_spec(arr, tm, rows_per_mod):
    d = arr.shape[-1]
    if arr.shape[1] == 1:
        return pl.BlockSpec((1, 1, d), lambda i: (i // rows_per_mod, 0, 0))
    return pl.BlockSpec((1, tm, d), lambda i: (i, 0, 0))


def inproj(x, shift, scale, g1, wq, wkv, wg, wu, gq, gk, tm, tiles_per_mod):
    t, d = x.shape
    nq, nkv, ng, nu = wq.shape[1], wkv.shape[1], wg.shape[1], wu.shape[1]
    full = lambda a: pl.BlockSpec(a.shape, lambda i: (0,) * a.ndim)
    row = lambda n: pl.BlockSpec((tm, n), lambda i: (i, 0))
    return pl.pallas_call(
        _inproj_body,
        out_shape=(jax.ShapeDtypeStruct((t, nq), F32), jax.ShapeDtypeStruct((t, 2 * KV_W), F32),
                   jax.ShapeDtypeStruct((t, 2 * KV_W), F32), jax.ShapeDtypeStruct((t, 2 * KV_W), F32),
                   jax.ShapeDtypeStruct((t, ng), F32), jax.ShapeDtypeStruct((t, nu), F32)),
        grid=(t // tm,),
        in_specs=[row(d), _mod_spec(shift, tm, tiles_per_mod), _mod_spec(scale, tm, tiles_per_mod), full(g1),
                  full(wq), full(wkv), full(wg), full(wu), full(gq), full(gk)],
        out_specs=(row(nq), row(2 * KV_W), row(2 * KV_W), row(2 * KV_W), row(ng), row(nu)),
        compiler_params=_cp(("arbitrary",)),
        name="inproj",
    )(x, shift, scale, g1, wq, wkv, wg, wu, gq, gk)


def _compress(x_ref, pe_ref, w_ref, gk_ref, kc_ref, vc_ref):
    nchunk = x_ref.shape[0]
    half = CMP_BLOCK // 2
    row_w = 2 * KV_W
    acc = [jnp.zeros((nchunk, KV_W), F32) for _ in range(4)]
    for l in range(half):
        xl = x_ref[:, l * row_w:(l + 1) * row_w]
        a = (xl + pe_ref[l:l + 1, :]).astype(BF16)
        b = (xl + pe_ref[half + l:half + l + 1, :]).astype(BF16)
        acc[0] += _dot(a[:, :KV_W], w_ref[0, l])
        acc[1] += _dot(a[:, KV_W:], w_ref[1, l])
        acc[2] += _dot(b[:, :KV_W], w_ref[0, half + l])
        acc[3] += _dot(b[:, KV_W:], w_ref[1, half + l])
    rio = lax.broadcasted_iota(I32, (nchunk, KV_W), 0)
    nxt = lambda v: jnp.where(rio < nchunk - 1, pltpu.roll(v, nchunk - 1, 0), 0.0)
    ck = acc[0] + nxt(acc[2])
    cv = acc[1] + nxt(acc[3])
    kc_ref[...] = _rms64(ck, _ones64()) * gk_ref[0:1, :]
    vc_ref[...] = cv


def _compress_prompt_body(x_ref, pe_ref, w_ref, gk_ref, kc_ref, vc_ref):
    _compress(x_ref.at[0], pe_ref, w_ref, gk_ref, kc_ref.at[0], vc_ref.at[0])


def compress_prompt(kv_chunks, pe, wc, gk):
    b, nchunk, width = kv_chunks.shape
    full = lambda a: pl.BlockSpec(a.shape, lambda i: (0,) * a.ndim)
    out = jax.ShapeDtypeStruct((b, nchunk, KV_W), F32)
    ospec = pl.BlockSpec((1, nchunk, KV_W), lambda i: (i, 0, 0))
    return pl.pallas_call(
        _compress_prompt_body,
        out_shape=(out, out),
        grid=(b,),
        in_specs=[pl.BlockSpec((1, nchunk, width), lambda i: (i, 0, 0)), full(pe), full(wc), full(gk)],
        out_specs=(ospec, ospec),
        compiler_params=_cp(("arbitrary",)),
        name="compress_prompt",
    )(kv_chunks, pe, wc, gk)


def _page_copy(cache_ref, buf_ref, sem_ref, pt_ref, seq, slot, j, rows):
    return pltpu.make_async_copy(cache_ref.at[pt_ref[seq, j]],
                                 buf_ref.at[slot, pl.ds(j * rows, rows), :], sem_ref.at[slot])


def _gather_pages(cache_ref, buf_ref, sem_ref, pt_ref, n_pages, rows):
    b = pl.program_id(0)
    nb = pl.num_programs(0)
    slot = b % 2

    def start(seq, s):
        for j in range(n_pages):
            _page_copy(cache_ref, buf_ref, sem_ref, pt_ref, seq, s, j, rows).start()

    @pl.when(b == 0)
    def _():
        start(b, slot)

    @pl.when(b + 1 < nb)
    def _():
        start(b + 1, 1 - slot)

    for j in range(n_pages):
        _page_copy(cache_ref, buf_ref, sem_ref, pt_ref, b, slot, j, rows).wait()
    return slot


def _compress_sample_body(pt_ref, cache_ref, pe_ref, w_ref, gk_ref, kc_ref, vc_ref, buf_ref, sem_ref):
    n_pages = pt_ref.shape[1]
    slot = _gather_pages(cache_ref, buf_ref, sem_ref, pt_ref, n_pages, cache_ref.shape[1])
    _compress(buf_ref.at[slot], pe_ref, w_ref, gk_ref, kc_ref.at[0], vc_ref.at[0])


def compress_sample(page_table, cache_chunks, pe, wc, gk):
    b, n_pages = page_table.shape
    _, cpp, width = cache_chunks.shape
    nchunk = n_pages * cpp
    full = lambda a: pl.BlockSpec(a.shape, lambda i, pt: (0,) * a.ndim)
    out = jax.ShapeDtypeStruct((b, nchunk, KV_W), F32)
    ospec = pl.BlockSpec((1, nchunk, KV_W), lambda i, pt: (i, 0, 0))
    return pl.pallas_call(
        _compress_sample_body,
        out_shape=(out, out),
        grid_spec=pltpu.PrefetchScalarGridSpec(
            num_scalar_prefetch=1, grid=(b,),
            in_specs=[pl.BlockSpec(memory_space=pl.ANY), full(pe), full(wc), full(gk)],
            out_specs=(ospec, ospec),
            scratch_shapes=[pltpu.VMEM((2, nchunk, width), F32), pltpu.SemaphoreType.DMA((2,))]),
        compiler_params=_cp(("arbitrary",)),
        name="compress_sample",
    )(page_table, cache_chunks, pe, wc, gk)


def _softmax_cols(s, valid):
    s = jnp.where(valid, s, NEG)
    m = jnp.max(s, axis=0, keepdims=True)
    e = jnp.where(valid, jnp.exp(s - m), 0.0)
    return e / jnp.maximum(jnp.sum(e, axis=0, keepdims=True), 1e-30)


def _nsa_tile(q, gates, kc, vc, ks_ref, kw, vw, pq0, pw0, slope, frac_t, e_gate, tq, ns):
    cols = N_HEADS * tq
    lane = lax.broadcasted_iota(I32, (1, cols), 1)
    pq = (pq0 + lane % tq).astype(F32)
    lg = lax.broadcasted_iota(I32, (tq, KV_W), 1) // HEAD_DIM
    qfull = jnp.concatenate([jnp.where(lg == g, q[:, r * KV_W:(r + 1) * KV_W], 0.0)
                             for r in range(GROUP) for g in range(N_KV)], axis=0).astype(BF16)

    nc = kc.shape[0]
    pos_c = (lax.broadcasted_iota(I32, (nc, cols), 0) * CMP_STRIDE + (CMP_BLOCK - 1)).astype(F32)
    dist_c = pq - pos_c
    p_c = _softmax_cols(_nt(kc.astype(BF16), qfull) - slope * dist_c, dist_c >= 0).astype(BF16)
    o_c = _tn(p_c, vc.astype(BF16))

    imp = _dot(frac_t.astype(BF16), p_c)
    imp = imp + pltpu.roll(imp, 4 * tq, 1) + pltpu.roll(imp, 8 * tq, 1) + pltpu.roll(imp, 12 * tq, 1)
    nsp = imp.shape[0]
    jio = lax.broadcasted_iota(I32, (nsp, cols), 0)
    blk_q = (pq0 + lane % tq) // SEL_BLOCK
    forced = (jio == 0) | (jio == blk_q) | (jio == blk_q - 1)
    valid_blk = (jio * SEL_BLOCK).astype(F32) <= pq
    imp = jnp.where(valid_blk, imp + FORCE_BONUS * forced.astype(F32), NEG)
    rank = jnp.zeros((nsp, cols), I32)
    for i in range(ns):
        row = imp[i:i + 1, :]
        rank = rank + ((row > imp) | ((row == imp) & (i < jio))).astype(I32)
    sel_bias = jnp.where(rank < min(SEL_TOP, ns), 0.0, NEG)[:ns]
    tk = ns * SEL_BLOCK
    sel_bias = jnp.broadcast_to(sel_bias[:, None, :], (ns, SEL_BLOCK, cols)).reshape(tk, cols)

    kpos = lax.broadcasted_iota(I32, (tk, cols), 0).astype(F32)
    dist_s = pq - kpos
    s_s = _nt(ks_ref[:, 0:KV_W].astype(BF16), qfull) - slope * dist_s
    p_s = _softmax_cols(s_s, (dist_s >= 0) & (sel_bias == 0.0)).astype(BF16)
    o_s = _tn(p_s, ks_ref[:, KV_W:].astype(BF16))

    tw = kw.shape[0]
    pos_w = (pw0 + lax.broadcasted_iota(I32, (tw, cols), 0)).astype(F32)
    dist_w = pq - pos_w
    valid_w = (dist_w >= 0) & (dist_w <= WINDOW) & (pos_w >= 0)
    p_w = _softmax_cols(_nt(kw.astype(BF16), qfull) - slope * dist_w, valid_w).astype(BF16)
    o_w = _tn(p_w, vw.astype(BF16))

    row_g = (lax.broadcasted_iota(I32, (cols, KV_W), 0) // tq) % N_KV
    own = row_g == lax.broadcasted_iota(I32, (cols, KV_W), 1) // HEAD_DIM

    def fold(o):
        o = jnp.where(own, o, 0.0)
        return jnp.concatenate(
            [sum(o[(r * N_KV + g) * tq:(r * N_KV + g + 1) * tq, :] for g in range(N_KV)) for r in range(GROUP)], axis=1)

    out = jnp.zeros((tq, N_HEADS * HEAD_DIM), F32)
    for j, o in enumerate((o_c, o_s, o_w)):
        out = out + _split_dot(gates, e_gate[j]) * fold(o)
    return out


def _nsa_prompt_body(q_ref, gt_ref, kc_ref, vc_ref, ks_ref, kw_ref, sl_ref, fr_ref, eg_ref, o_ref, *, tq, ns, tw):
    q0 = pl.program_id(1) * tq
    w = kw_ref[0, pl.ds(pl.multiple_of(q0, 8), tw), :]
    o_ref[...] = _nsa_tile(q_ref[...], gt_ref[...], kc_ref[0], vc_ref[0], ks_ref.at[0], w[:, :KV_W], w[:, KV_W:],
                           q0, q0 - WINDOW, sl_ref[...], fr_ref[...], eg_ref, tq, ns)


def nsa_prompt(q, gates, kc, vc, kvs, kvw_pad, slope, frac_t, e_gate, tq):
    b, t, _ = kvs.shape
    ns = t // SEL_BLOCK
    tw = WINDOW + max(tq, 8)
    nq = t // tq
    full = lambda a: pl.BlockSpec(a.shape, lambda i, j: (0,) * a.ndim)
    perb = lambda a: pl.BlockSpec((1,) + a.shape[1:], lambda i, j: (i, 0, 0))
    row = lambda n: pl.BlockSpec((tq, n), lambda i, j: (i * nq + j, 0))
    return pl.pallas_call(
        functools.partial(_nsa_prompt_body, tq=tq, ns=ns, tw=tw),
        out_shape=jax.ShapeDtypeStruct((b * t, N_HEADS * HEAD_DIM), F32),
        grid=(b, nq),
        in_specs=[row(q.shape[1]), row(gates.shape[1]), perb(kc), perb(vc), perb(kvs), perb(kvw_pad),
                  full(slope), full(frac_t), full(e_gate)],
        out_specs=row(N_HEADS * HEAD_DIM),
        compiler_params=_cp(("arbitrary", "arbitrary")),
        name="nsa_prompt",
    )(q, gates, kc, vc, kvs, kvw_pad, slope, frac_t, e_gate)


def _nsa_sample_body(pt_ref, q_ref, gt_ref, kc_ref, vc_ref, cache_ref, knew_ref, win_ref, wnew_ref, sl_ref, fr_ref,
                     eg_ref, o_ref, buf_ref, sem_ref, *, tq, ns, past):
    n_pages = pt_ref.shape[1]
    tk = ns * SEL_BLOCK

    @pl.when(pl.program_id(0) == 0)
    def _():
        buf_ref[:, past:, :] = jnp.zeros((2, tk - past, buf_ref.shape[2]), F32)

    slot = _gather_pages(cache_ref, buf_ref, sem_ref, pt_ref, n_pages, cache_ref.shape[1])
    buf_ref[slot, past:past + tq, :] = knew_ref[0]
    wbuf = win_ref.shape[1]
    tw = wbuf + SEL_BLOCK
    w = jnp.concatenate([win_ref[0], wnew_ref[0], jnp.zeros((tw - wbuf - tq, 2 * KV_W), F32)], axis=0)
    o_ref[0] = _nsa_tile(q_ref[0], gt_ref[0], kc_ref[0], vc_ref[0], buf_ref.at[slot], w[:, :KV_W], w[:, KV_W:],
                         past, past - wbuf, sl_ref[...], fr_ref[...], eg_ref, tq, ns)


def nsa_sample(page_table, q, gates, kc, vc, cache_rows, kvs_new, win, kvw_new, slope, frac_t, e_gate):
    b, n_pages = page_table.shape
    rows = cache_rows.shape[1]
    past = n_pages * rows
    tq = q.shape[1]
    ns = -(-(past + tq) // SEL_BLOCK)
    tk = ns * SEL_BLOCK
    full = lambda a: pl.BlockSpec(a.shape, lambda i, pt: (0,) * a.ndim)
    perb = lambda a: pl.BlockSpec((1,) + a.shape[1:], lambda i, pt: (i, 0, 0))
    return pl.pallas_call(
        functools.partial(_nsa_sample_body, tq=tq, ns=ns, past=past),
        out_shape=jax.ShapeDtypeStruct((b, tq, N_HEADS * HEAD_DIM), F32),
        grid_spec=pltpu.PrefetchScalarGridSpec(
            num_scalar_prefetch=1, grid=(b,),
            in_specs=[perb(q), perb(gates), perb(kc), perb(vc), pl.BlockSpec(memory_space=pl.ANY), perb(kvs_new),
                      perb(win), perb(kvw_new), full(slope), full(frac_t), full(e_gate)],
            out_specs=pl.BlockSpec((1, tq, N_HEADS * HEAD_DIM), lambda i, pt: (i, 0, 0)),
            scratch_shapes=[pltpu.VMEM((2, tk, 2 * KV_W), F32), pltpu.SemaphoreType.DMA((2,))]),
        compiler_params=_cp(("arbitrary",)),
        name="nsa_sample",
    )(page_table, q, gates, kc, vc, cache_rows, kvs_new, win, kvw_new, slope, frac_t, e_gate)


def _pool_diff_body(ext_ref, d_ref, *, t, pos0):
    gw = ext_ref.shape[2] // len(POOL_SIZES)
    pos = pos0 + lax.broadcasted_iota(I32, (t, gw), 0)
    for gi, w in enumerate(POOL_SIZES):
        lanes = pl.ds(gi * gw, gw)
        cur = ext_ref[0, pl.ds(POOL_BUF, t), lanes]
        acc = cur
        for j in range(1, w):
            acc = acc + ext_ref[0, pl.ds(POOL_BUF - j, t), lanes]
        cnt = jnp.minimum(w, pos + 1).astype(F32)
        d_ref[0, :, lanes] = acc / cnt - cur


def pool_diff(ext, pos0):
    b, rows, width = ext.shape
    t = rows - POOL_BUF
    return pl.pallas_call(
        functools.partial(_pool_diff_body, t=t, pos0=pos0),
        out_shape=jax.ShapeDtypeStruct((b, t, width), F32),
        grid=(b,),
        in_specs=[pl.BlockSpec((1, rows, width), lambda i: (i, 0, 0))],
        out_specs=pl.BlockSpec((1, t, width), lambda i: (i, 0, 0)),
        compiler_params=_cp(("arbitrary",)),
        name="pool_diff",
    )(ext)


def _outproj_body(o_ref, d_ref, x_ref, gate_ref, sh_ref, sc_ref, g2_ref, wn_ref, wp_ref, wpool_ref, ps_ref,
                  x1_ref, h2_ref):
    gw = wpool_ref.shape[1]
    d = d_ref[...]
    yp = jnp.concatenate([_dot(d[:, g * gw:(g + 1) * gw].astype(BF16), wpool_ref[g]) for g in range(len(POOL_SIZES))],
                         axis=1) * ps_ref[...]
    mix = _dot(o_ref[...].astype(BF16), wn_ref[...]) + _dot(yp.astype(BF16), wp_ref[...])
    x1 = x_ref[...] + gate_ref[0] * mix
    x1_ref[...] = x1
    h2 = _rms_rows(x1) * g2_ref[...]
    h2_ref[...] = (h2 * (1.0 + sc_ref[0]) + sh_ref[0]).astype(BF16)


def outproj(o, d, x, gate, shift, scale, g2, wn, wp, wpool, ps, tm, tiles_per_mod):
    t, dm = x.shape
    full = lambda a: pl.BlockSpec(a.shape, lambda i: (0,) * a.ndim)
    row = lambda n: pl.BlockSpec((tm, n), lambda i: (i, 0))
    ms = lambda a: _mod_spec(a, tm, tiles_per_mod)
    return pl.pallas_call(
        _outproj_body,
        out_shape=(jax.ShapeDtypeStruct((t, dm), F32), jax.ShapeDtypeStruct((t, dm), BF16)),
        grid=(t // tm,),
        in_specs=[row(o.shape[1]), row(d.shape[1]), row(dm), ms(gate), ms(shift), ms(scale), full(g2),
                  full(wn), full(wp), full(wpool), full(ps)],
        out_specs=(row(dm), row(dm)),
        compiler_params=_cp(("arbitrary",)),
        name="outproj",
    )(o, d, x, gate, shift, scale, g2, wn, wp, wpool, ps)


_CAND_PIECES = ((0, 16), (1, 8), (2, 8), (3, 8), (4, 8), (5, 8), (6, 8), (7, 8))


def _top16_rows(s):
    kio = lax.broadcasted_iota(I32, s.shape, 0)
    nk = s.shape[0]
    rank = jnp.full(s.shape, float(PEER_TOPK), F32)
    vals, keys = [], []
    for it in range(PEER_TOPK):
        m = jnp.max(s, axis=0, keepdims=True)
        idx = jnp.min(jnp.where(s == m, kio, nk), axis=0, keepdims=True)
        hit = kio == idx
        rank = jnp.where(hit, float(it), rank)
        s = jnp.where(hit, -jnp.inf, s)
        vals.append(m)
        keys.append(idx)
    return vals, keys, rank


def _peer_route_body(h_ref, wq_ref, sk_ref, rk_ref, lim_ref, e1_ref, e2_ref, q_scr):
    hd = pl.program_id(1)

    qd = sk_ref.shape[3]

    @pl.when(hd == 0)
    def _():
        q = _dot(h_ref[...], wq_ref[...]).astype(BF16)
        for i in range(PEER_HEADS):
            q_scr[i] = q[:, i * 2 * qd:(i + 1) * 2 * qd]

    qh = q_scr[hd]
    s1 = _nt(sk_ref[0, 0], qh[:, :qd])
    s2 = _nt(sk_ref[1, 0], qh[:, qd:])
    v1, k1, _ = _top16_rows(s1)
    v2, _, rank2 = _top16_rows(s2)
    n = s1.shape[1]
    v2a = jnp.concatenate(v2, axis=0)
    top = v1[0] + v2[0]

    bio = {nb: lax.broadcasted_iota(I32, (nb, n), 0) for nb in (8, PEER_TOPK)}
    v2p = {8: jnp.concatenate(v2[:8], axis=0), PEER_TOPK: v2a}
    cands, flats = [], []
    for a, nb in _CAND_PIECES:
        c = v1[a] + v2p[nb]
        ok = (bio[nb] + 1) * (a + 1) <= PEER_TOPK
        cands.append(jnp.where(ok, c, -jnp.inf))
        flats.append(a * PEER_TOPK + bio[nb])
    v1b = jnp.concatenate(v1[8:], axis=0)
    cands.append(v1b + v2[0])
    flats.append((bio[8] + 8) * PEER_TOPK)
    cand = jnp.concatenate(cands, axis=0)
    flat = jnp.concatenate(flats, axis=0)
    big = PEER_TOPK * PEER_TOPK
    taken = jnp.zeros(cand.shape, jnp.bool_)
    z = jnp.zeros((1, n), F32)
    for _ in range(PEER_TOPK):
        m = jnp.max(cand, axis=0, keepdims=True)
        f = jnp.min(jnp.where(cand == m, flat, big), axis=0, keepdims=True)
        hit = flat == f
        taken = taken | hit
        cand = jnp.where(hit, -jnp.inf, cand)
        z = z + jnp.exp(m - top)
    takenf = taken.astype(F32)

    lim = jnp.zeros(s1.shape, F32)
    kio = lax.broadcasted_iota(I32, s1.shape, 0)
    off = 0
    for a, nb in _CAND_PIECES:
        cnt = jnp.sum(takenf[off:off + nb], axis=0, keepdims=True)
        lim = jnp.where(kio == k1[a], cnt, lim)
        off += nb
    for i in range(8):
        lim = jnp.where(kio == k1[8 + i], takenf[off + i:off + i + 1], lim)

    rk_ref[0] = rank2
    lim_ref[0] = lim
    e1_ref[0] = jnp.exp(s1 - v1[0]) / z
    e2_ref[0] = jnp.exp(s2 - v2[0])


def peer_route(h2, wq, sk, tt):
    t, d = h2.shape
    nk = sk.shape[2]
    out = jax.ShapeDtypeStruct((PEER_HEADS, nk, t), F32)
    ospec = pl.BlockSpec((1, nk, tt), lambda i, h: (h, 0, i))
    return pl.pallas_call(
        _peer_route_body,
        out_shape=(out, out, out, out),
        grid=(t // tt, PEER_HEADS),
        in_specs=[pl.BlockSpec((tt, d), lambda i, h: (i, 0)),
                  pl.BlockSpec(wq.shape, lambda i, h: (0, 0)),
                  pl.BlockSpec((2, 1) + sk.shape[2:], lambda i, h: (0, h, 0, 0))],
        out_specs=(ospec, ospec, ospec, ospec),
        scratch_shapes=[pltpu.VMEM((PEER_HEADS, tt, wq.shape[1] // PEER_HEADS), BF16)],
        compiler_params=_cp(("arbitrary", "arbitrary")),
        name="peer_route",
    )(h2, wq, sk)


def _peer_dense_body(h_ref, u_ref, v_ref, rk_ref, lim_ref, e1_ref, e2_ref, x1_ref, gate_ref, y_ref, acc_ref):
    j = pl.program_id(1)
    nk = rk_ref.shape[1]
    et, tt = u_ref.shape[0], h_ref.shape[0]

    @pl.when(j == 0)
    def _():
        acc_ref[...] = jnp.zeros_like(acc_ref)

    a = _nt(u_ref[...], h_ref[...])
    act = a * (lax.erf(a * np.float32(1.0 / np.sqrt(2.0))) + 1.0) * 0.5
    parts = []
    for rr in range(et // nk):
        r = j * (et // nk) + rr
        w = jnp.zeros((nk, tt), F32)
        for hd in range(PEER_HEADS):
            lim = lim_ref[hd, pl.ds(r, 1), :]
            e1 = e1_ref[hd, pl.ds(r, 1), :]
            w = w + jnp.where(rk_ref[hd] < lim, e2_ref[hd] * e1, 0.0)
        parts.append(w)
    wt = (act * jnp.concatenate(parts, axis=0)).astype(BF16)
    acc_ref[...] += _tn(wt, v_ref[...])

    @pl.when(j == pl.num_programs(1) - 1)
    def _():
        y_ref[...] = x1_ref[...] + gate_ref[0] * acc_ref[...]


def peer_dense(h2, u, v, rk, lim, e1, e2, x1, gate, tt, et, tiles_per_mod):
    t, d = h2.shape
    ne = u.shape[0]
    nk = rk.shape[1]
    tok = pl.BlockSpec((tt, d), lambda i, j: (i, 0))
    exp = pl.BlockSpec((et, d), lambda i, j: (j, 0))
    rt = pl.BlockSpec((PEER_HEADS, nk, tt), lambda i, j: (0, 0, i))
    if gate.shape[1] == 1:
        gspec = pl.BlockSpec((1, 1, d), lambda i, j: (i // tiles_per_mod, 0, 0))
    else:
        gspec = pl.BlockSpec((1, tt, d), lambda i, j: (i, 0, 0))
    return pl.pallas_call(
        _peer_dense_body,
        out_shape=jax.ShapeDtypeStruct((t, d), F32),
        grid=(t // tt, ne // et),
        in_specs=[tok, exp, exp, rt, rt, rt, rt, tok, gspec],
        out_specs=tok,
        scratch_shapes=[pltpu.VMEM((tt, d), F32)],
        compiler_params=_cp(("arbitrary", "arbitrary")),
        name="peer_dense",
    )(h2, u, v, rk, lim, e1, e2, x1, gate)


def _slope_lanes(tq):
    h = np.arange(1, N_HEADS + 1, dtype=np.float32)
    s = (2.0 ** (-8.0 * h / N_HEADS)).reshape(N_KV, GROUP)
    return jnp.asarray(np.repeat(s.T.reshape(-1), tq)[None, :], F32)


def _frac_t(nc, ns, nc_pad, ns_pad):
    pos = np.arange(nc)[:, None] * CMP_STRIDE + np.arange(CMP_BLOCK)[None, :]
    f = ((pos // SEL_BLOCK)[:, :, None] == np.arange(ns)[None, None, :]).mean(axis=1)
    out = np.zeros((ns_pad, nc_pad), np.float32)
    out[:ns, :nc] = f.T
    return jnp.asarray(out)


def _gate_expand(width):
    e = np.zeros((3, width, N_HEADS * HEAD_DIM), np.float32)
    for g in range(N_KV):
        for r in range(GROUP):
            for j in range(3):
                c0 = r * KV_W + g * HEAD_DIM
                e[j, (g * GROUP + r) * 3 + j, c0:c0 + HEAD_DIM] = 1.0
    return jnp.asarray(e, BF16)


def _rgd(a, axis):
    shp = a.shape
    a = a.reshape(shp[:axis] + (N_KV, GROUP, HEAD_DIM) + shp[axis + 1:])
    a = jnp.swapaxes(a, axis, axis + 1)
    return a.reshape(shp)


def kernel(x_prompt, x_sample, cache_kv_cmp, cache_kv_slc, cache_kv_win, state_pool, page_table, c_prompt, c_sample,
           w_ada, b_ada, g_norm1, g_norm2, w_in, w_out, g_qnorm, g_knorm, cmp_pe, w_cmp, w_pool, pool_scale,
           peer_wq, peer_subkeys, peer_u, peer_v):
    bp, tp, dm = x_prompt.shape
    bs, ts, _ = x_sample.shape
    n_pages = page_table.shape[1]
    page = cache_kv_cmp.shape[1]
    past = n_pages * page
    nsa_w = N_HEADS * HEAD_DIM
    kv3 = 3 * 2 * KV_W
    ngl = 3 * N_HEADS
    gpad = 128
    tm = 256

    wq = _rgd(w_in[:, :nsa_w], 1).astype(BF16)
    wkv = w_in[:, nsa_w:nsa_w + kv3].astype(BF16)
    wg = jnp.pad(w_in[:, nsa_w + kv3:nsa_w + kv3 + ngl], ((0, 0), (0, gpad - ngl))).astype(BF16)
    wu = w_in[:, nsa_w + kv3 + ngl:].astype(BF16)
    gq = jnp.tile(g_qnorm, N_HEADS)[None, :]
    gk = jnp.tile(g_knorm, (1, N_KV))
    g1 = g_norm1[None, :]
    g2 = g_norm2[None, :]
    wn = _rgd(w_out[:nsa_w], 0).astype(BF16)
    wp = w_out[nsa_w:].astype(BF16)
    wpool = w_pool.astype(BF16)
    ps = pool_scale[None, :]
    wc = jnp.einsum('cglde,gh->clgdhe', w_cmp, jnp.eye(N_KV, dtype=F32)).reshape(2, CMP_BLOCK, KV_W, KV_W).astype(BF16)
    pe = jnp.transpose(cmp_pe, (1, 0, 2, 3)).reshape(CMP_BLOCK, 2 * KV_W)
    e_gate = _gate_expand(gpad)
    pwq = peer_wq.astype(BF16)
    psk = peer_subkeys.astype(BF16)
    pu = peer_u.astype(BF16)
    pv = peer_v.astype(BF16)

    mod = adaln(jnp.concatenate([c_prompt, c_sample], axis=0), w_ada, b_ada).reshape(bp + bs, 6, dm)
    mod_p = [mod[:bp, k][:, None, :] for k in range(6)]
    mod_s = [jnp.repeat(mod[bp:, k], ts, axis=0).reshape(bs * ts // tm, tm, dm) for k in range(6)]

    xp = x_prompt.reshape(bp * tp, dm)
    q, kvc, kvs, kvw, gates, u = inproj(xp, mod_p[0], mod_p[1], g1, wq, wkv, wg, wu, gq, gk, tm, tp // tm)
    kc, vc = compress_prompt(kvc.reshape(bp, tp // CMP_STRIDE, CMP_STRIDE * 2 * KV_W), pe, wc, gk)
    tq = 16
    nc = (tp - CMP_BLOCK) // CMP_STRIDE + 1
    kvw3 = kvw.reshape(bp, tp, 2 * KV_W)
    o_nsa = nsa_prompt(q, gates, kc, vc, kvs.reshape(bp, tp, 2 * KV_W), jnp.pad(kvw3, ((0, 0), (WINDOW, 0), (0, 0))),
                       _slope_lanes(tq), _frac_t(nc, tp // SEL_BLOCK, tp // CMP_STRIDE, 40), e_gate, tq)
    u3 = u.reshape(bp, tp, -1)
    d_pool = pool_diff(jnp.pad(u3, ((0, 0), (POOL_BUF, 0), (0, 0))), 0).reshape(bp * tp, -1)
    x1, h2 = outproj(o_nsa, d_pool, xp, mod_p[2], mod_p[3], mod_p[4], g2, wn, wp, wpool, ps, tm, tp // tm)
    routes = peer_route(h2, pwq, psk, 256)
    tt = 512
    y_prompt = peer_dense(h2, pu, pv, *routes, x1, mod_p[5], tt, 256, tp // tt).reshape(bp, tp, dm)
    shp_p = (bp, tp, 2, N_KV, HEAD_DIM)
    win_p = kvw3[:, -min(WINDOW, tp):].reshape(bp, -1, 2, N_KV, HEAD_DIM)
    pool_p = u3[:, -POOL_BUF:]

    xs = x_sample.reshape(bs * ts, dm)
    q, kvc_s, kvs_s, kvw_s, gates, u = inproj(xs, mod_s[0], mod_s[1], g1, wq, wkv, wg, wu, gq, gk, tm, 1)
    cpp = page // CMP_STRIDE
    kc, vc = compress_sample(page_table, cache_kv_cmp.reshape(-1, cpp, CMP_STRIDE * 2 * KV_W), pe, wc, gk)
    nc = (past + ts - CMP_BLOCK) // CMP_STRIDE + 1
    ns = -(-(past + ts) // SEL_BLOCK)
    win = cache_kv_win.reshape(bs, -1, 2 * KV_W)
    kvw_s3 = kvw_s.reshape(bs, ts, 2 * KV_W)
    o_nsa = nsa_sample(page_table, q.reshape(bs, ts, -1), gates.reshape(bs, ts, -1), kc, vc,
                       cache_kv_slc.reshape(-1, page, 2 * KV_W), kvs_s.reshape(bs, ts, 2 * KV_W), win, kvw_s3,
                       _slope_lanes(ts), _frac_t(nc, ns, past // CMP_STRIDE, 40), e_gate).reshape(bs * ts, -1)
    ext = jnp.concatenate([state_pool, u.reshape(bs, ts, -1)], axis=1)
    d_pool = pool_diff(ext, past).reshape(bs * ts, -1)
    x1, h2 = outproj(o_nsa, d_pool, xs, mod_s[2], mod_s[3], mod_s[4], g2, wn, wp, wpool, ps, tm, 1)
    routes = peer_route(h2, pwq, psk, 256)
    gate5 = mod_s[5].reshape(bs * ts // tt, tt, dm)
    y_sample = peer_dense(h2, pu, pv, *routes, x1, gate5, tt, 256, 1).reshape(bs, ts, dm)
    shp_s = (bs, ts, 2, N_KV, HEAD_DIM)
    win_s = jnp.concatenate([win, kvw_s3], axis=1)[:, -win.shape[1]:].reshape(cache_kv_win.shape)
    pool_s = ext[:, -POOL_BUF:]

    return (y_prompt, y_sample, kvc.reshape(shp_p), kvs.reshape(shp_p), win_p, pool_p,
            kvc_s.reshape(shp_s), kvs_s.reshape(shp_s), win_s, pool_s)
```

```python
import functools

import numpy as np
import jax
import jax.numpy as jnp
from jax import lax
from jax.experimental import pallas as pl
from jax.experimental.pallas import tpu as pltpu

F32, BF16, I32 = jnp.float32, jnp.bfloat16, jnp.int32

HEAD_DIM = 64
N_KV = 4
GROUP = 4
N_HEADS = N_KV * GROUP
CMP_BLOCK = 32
CMP_STRIDE = 16
SEL_BLOCK = 64
SEL_TOP = 16
WINDOW = 512
POOL_SIZES = (2, 4, 8, 16)
POOL_BUF = max(POOL_SIZES) - 1
PEER_HEADS = 8
N_KEYS = 128
PEER_TOPK = 16
EPS = 1e-6
NEG = -1e30
FORCE_BONUS = 1e4
KV_W = N_KV * HEAD_DIM
VMEM_LIMIT = 56 * 1024 * 1024


def _cp(sem, vmem=VMEM_LIMIT):
    return pltpu.CompilerParams(dimension_semantics=sem, vmem_limit_bytes=vmem)


def _dot(a, b):
    return jnp.dot(a, b, preferred_element_type=F32)


def _nt(a, b):
    return lax.dot_general(a, b, (((1,), (1,)), ((), ())), preferred_element_type=F32)


def _tn(a, b):
    return lax.dot_general(a, b, (((0,), (0,)), ((), ())), preferred_element_type=F32)


def _split_dot(x, w):
    hi = x.astype(BF16)
    r1 = x - hi.astype(F32)
    mid = r1.astype(BF16)
    lo = (r1 - mid.astype(F32)).astype(BF16)
    return _dot(hi, w) + _dot(mid, w) + _dot(lo, w)


def _ones64():
    r = lax.broadcasted_iota(I32, (KV_W, KV_W), 0) // HEAD_DIM
    c = lax.broadcasted_iota(I32, (KV_W, KV_W), 1) // HEAD_DIM
    return (r == c).astype(BF16)


def _rms64(x, ones):
    ms = _split_dot(x * x, ones) * (1.0 / HEAD_DIM)
    return x * lax.rsqrt(ms + EPS)


def _rms_rows(x):
    return x * lax.rsqrt(jnp.mean(x * x, axis=-1, keepdims=True) + EPS)


def _adaln_body(c_ref, w_ref, b_ref, o_ref):
    c = c_ref[...]
    s = (c * jax.nn.sigmoid(c)).astype(BF16)
    o_ref[...] = _dot(s, w_ref[...].astype(BF16)) + b_ref[...]


def adaln(c_all, w_ada, b_ada, tn=1024):
    m, d = c_all.shape
    n = w_ada.shape[1]
    return pl.pallas_call(
        _adaln_body,
        out_shape=jax.ShapeDtypeStruct((m, n), F32),
        grid=(n // tn,),
        in_specs=[pl.BlockSpec((m, d), lambda j: (0, 0)),
                  pl.BlockSpec((d, tn), lambda j: (0, j)),
                  pl.BlockSpec((1, tn), lambda j: (0, j))],
        out_specs=pl.BlockSpec((m, tn), lambda j: (0, j)),
        compiler_params=_cp(("arbitrary",)),
        name="adaln",
    )(c_all, w_ada, b_ada.reshape(1, n))


def _inproj_body(x_ref, sh_ref, sc_ref, g1_ref, wq_ref, wkv_ref, wg_ref, wu_ref, gq_ref, gk_ref,
                 q_ref, kvc_ref, kvs_ref, kvw_ref, gt_ref, u_ref):
    h = _rms_rows(x_ref[...]) * g1_ref[...]
    h = h * (1.0 + sc_ref[0]) + sh_ref[0]
    hb = h.astype(BF16)
    ones = _ones64()
    zq = _dot(hb, wq_ref[...])
    for r in range(GROUP):
        sl = slice(r * KV_W, (r + 1) * KV_W)
        q_ref[:, sl] = _rms64(zq[:, sl], ones) * gq_ref[:, sl] * (HEAD_DIM ** -0.5)
    zkv = _dot(hb, wkv_ref[...])
    kvc_ref[...] = zkv[:, 0:2 * KV_W]
    kvs_ref[:, 0:KV_W] = _rms64(zkv[:, 2 * KV_W:3 * KV_W], ones) * gk_ref[1:2, :]
    kvs_ref[:, KV_W:] = zkv[:, 3 * KV_W:4 * KV_W]
    kvw_ref[:, 0:KV_W] = _rms64(zkv[:, 4 * KV_W:5 * KV_W], ones) * gk_ref[2:3, :]
    kvw_ref[:, KV_W:] = zkv[:, 5 * KV_W:6 * KV_W]
    gt_ref[...] = jax.nn.sigmoid(_dot(hb, wg_ref[...]))
    u_ref[...] = _dot(hb, wu_ref[...])


def _mod_spec(arr, tm, rows_per_mod):
    d = arr.shape[-1]
    if arr.shape[1] == 1:
        return pl.BlockSpec((1, 1, d), lambda i: (i // rows_per_mod, 0, 0))
    return pl.BlockSpec((1, tm, d), lambda i: (i, 0, 0))


def inproj(x, shift, scale, g1, wq, wkv, wg, wu, gq, gk, tm, tiles_per_mod):
    t, d = x.shape
    nq, nkv, ng, nu = wq.shape[1], wkv.shape[1], wg.shape[1], wu.shape[1]
    full = lambda a: pl.BlockSpec(a.shape, lambda i: (0,) * a.ndim)
    row = lambda n: pl.BlockSpec((tm, n), lambda i: (i, 0))
    return pl.pallas_call(
        _inproj_body,
        out_shape=(jax.ShapeDtypeStruct((t, nq), F32), jax.ShapeDtypeStruct((t, 2 * KV_W), F32),
                   jax.ShapeDtypeStruct((t, 2 * KV_W), F32), jax.ShapeDtypeStruct((t, 2 * KV_W), F32),
                   jax.ShapeDtypeStruct((t, ng), F32), jax.ShapeDtypeStruct((t, nu), F32)),
        grid=(t // tm,),
        in_specs=[row(d), _mod_spec(shift, tm, tiles_per_mod), _mod_spec(scale, tm, tiles_per_mod), full(g1),
                  full(wq), full(wkv), full(wg), full(wu), full(gq), full(gk)],
        out_specs=(row(nq), row(2 * KV_W), row(2 * KV_W), row(2 * KV_W), row(ng), row(nu)),
        compiler_params=_cp(("arbitrary",)),
        name="inproj",
    )(x, shift, scale, g1, wq, wkv, wg, wu, gq, gk)


def _compress(get_x, nchunk, pe_ref, w_ref, gk_ref, kc_ref, vc_ref):
    half = CMP_BLOCK // 2
    acc = [jnp.zeros((nchunk, KV_W), F32) for _ in range(4)]
    for l in range(half):
        xl = get_x(l)
        a = (xl + pe_ref[l:l + 1, :]).astype(BF16)
        b = (xl + pe_ref[half + l:half + l + 1, :]).astype(BF16)
        acc[0] += _dot(a[:, :KV_W], w_ref[0, l])
        acc[1] += _dot(a[:, KV_W:], w_ref[1, l])
        acc[2] += _dot(b[:, :KV_W], w_ref[0, half + l])
        acc[3] += _dot(b[:, KV_W:], w_ref[1, half + l])
    rio = lax.broadcasted_iota(I32, (nchunk, KV_W), 0)
    nxt = lambda v: jnp.where(rio < nchunk - 1, pltpu.roll(v, nchunk - 1, 0), 0.0)
    ck = acc[0] + nxt(acc[2])
    cv = acc[1] + nxt(acc[3])
    kc_ref[...] = _rms64(ck, _ones64()) * gk_ref[0:1, :]
    vc_ref[...] = cv


def _compress_prompt_body(x_ref, pe_ref, w_ref, gk_ref, kc_ref, vc_ref):
    row_w = 2 * KV_W
    _compress(lambda l: x_ref[0, :, l * row_w:(l + 1) * row_w], x_ref.shape[1], pe_ref, w_ref, gk_ref,
              kc_ref.at[0], vc_ref.at[0])


def compress_prompt(kv_chunks, pe, wc, gk):
    b, nchunk, width = kv_chunks.shape
    full = lambda a: pl.BlockSpec(a.shape, lambda i: (0,) * a.ndim)
    out = jax.ShapeDtypeStruct((b, nchunk, KV_W), F32)
    ospec = pl.BlockSpec((1, nchunk, KV_W), lambda i: (i, 0, 0))
    return pl.pallas_call(
        _compress_prompt_body,
        out_shape=(out, out),
        grid=(b,),
        in_specs=[pl.BlockSpec((1, nchunk, width), lambda i: (i, 0, 0)), full(pe), full(wc), full(gk)],
        out_specs=(ospec, ospec),
        compiler_params=_cp(("arbitrary",)),
        name="compress_prompt",
    )(kv_chunks, pe, wc, gk)


def _page_copy(cache_ref, buf_ref, sem_ref, pt_ref, seq, slot, j):
    return pltpu.make_async_copy(cache_ref.at[pt_ref[seq, j]], buf_ref.at[slot, j], sem_ref.at[slot])


def _gather_pages(cache_ref, buf_ref, sem_ref, pt_ref):
    b = pl.program_id(0)
    nb = pl.num_programs(0)
    n_pages = pt_ref.shape[1]
    slot = b % 2

    def start(seq, s):
        for j in range(n_pages):
            _page_copy(cache_ref, buf_ref, sem_ref, pt_ref, seq, s, j).start()

    @pl.when(b == 0)
    def _():
        start(b, slot)

    @pl.when(b + 1 < nb)
    def _():
        start(b + 1, 1 - slot)

    for j in range(n_pages):
        _page_copy(cache_ref, buf_ref, sem_ref, pt_ref, b, slot, j).wait()
    return slot


LANES = 128


def _compress_sample_body(pt_ref, cache_ref, pe_ref, w_ref, gk_ref, kc_ref, vc_ref, buf_ref, xs_ref, sem_ref):
    n_pages, page = pt_ref.shape[1], cache_ref.shape[2]
    slot = _gather_pages(cache_ref, buf_ref, sem_ref, pt_ref)
    nlb = xs_ref.shape[0]
    for j in range(n_pages):
        for k in range(nlb):
            xs_ref[k, pl.ds(j * page, page), :] = buf_ref[slot, j, pl.ds(k * LANES, LANES), :].T
    nchunk = n_pages * page // CMP_STRIDE
    get_x = lambda l: jnp.concatenate([xs_ref[k, pl.ds(l, nchunk, stride=CMP_STRIDE), :] for k in range(nlb)], axis=1)
    _compress(get_x, nchunk, pe_ref, w_ref, gk_ref, kc_ref.at[0], vc_ref.at[0])


def compress_sample(page_table, cache_t, pe, wc, gk):
    b, n_pages = page_table.shape
    _, width, page = cache_t.shape
    nchunk = n_pages * page // CMP_STRIDE
    full = lambda a: pl.BlockSpec(a.shape, lambda i, pt: (0,) * a.ndim)
    out = jax.ShapeDtypeStruct((b, nchunk, KV_W), F32)
    ospec = pl.BlockSpec((1, nchunk, KV_W), lambda i, pt: (i, 0, 0))
    return pl.pallas_call(
        _compress_sample_body,
        out_shape=(out, out),
        grid_spec=pltpu.PrefetchScalarGridSpec(
            num_scalar_prefetch=1, grid=(b,),
            in_specs=[pl.BlockSpec(memory_space=pl.ANY), full(pe), full(wc), full(gk)],
            out_specs=(ospec, ospec),
            scratch_shapes=[pltpu.VMEM((2, n_pages, width, page), F32),
                            pltpu.VMEM((width // LANES, n_pages * page, LANES), F32),
                            pltpu.SemaphoreType.DMA((2,))]),
        compiler_params=_cp(("arbitrary",)),
        name="compress_sample",
    )(page_table, cache_t, pe, wc, gk)


def _softmax_cols(s, valid):
    s = jnp.where(valid, s, NEG)
    m = jnp.max(s, axis=0, keepdims=True)
    e = jnp.where(valid, jnp.exp(s - m), 0.0)
    return e / jnp.maximum(jnp.sum(e, axis=0, keepdims=True), 1e-30)


def _nsa_front(q, kc, vc, pq0, slope, frac_t, tq, ns):
    cols = N_HEADS * tq
    lane = lax.broadcasted_iota(I32, (1, cols), 1)
    pq = (pq0 + lane % tq).astype(F32)
    lg = lax.broadcasted_iota(I32, (tq, KV_W), 1) // HEAD_DIM
    qfull = jnp.concatenate([jnp.where(lg == g, q[:, r * KV_W:(r + 1) * KV_W], 0.0)
                             for r in range(GROUP) for g in range(N_KV)], axis=0).astype(BF16)

    nc = kc.shape[0]
    pos_c = (lax.broadcasted_iota(I32, (nc, cols), 0) * CMP_STRIDE + (CMP_BLOCK - 1)).astype(F32)
    dist_c = pq - pos_c
    p_c = _softmax_cols(_nt(kc.astype(BF16), qfull) - slope * dist_c, dist_c >= 0).astype(BF16)
    o_c = _tn(p_c, vc.astype(BF16))

    imp = _dot(frac_t.astype(BF16), p_c)
    imp = imp + pltpu.roll(imp, 4 * tq, 1) + pltpu.roll(imp, 8 * tq, 1) + pltpu.roll(imp, 12 * tq, 1)
    nsp = imp.shape[0]
    jio = lax.broadcasted_iota(I32, (nsp, cols), 0)
    blk_q = (pq0 + lane % tq) // SEL_BLOCK
    forced = (jio == 0) | (jio == blk_q) | (jio == blk_q - 1)
    valid_blk = (jio * SEL_BLOCK).astype(F32) <= pq
    imp = jnp.where(valid_blk, imp + FORCE_BONUS * forced.astype(F32), NEG)
    rank = jnp.zeros((nsp, cols), I32)
    for i in range(ns):
        row = imp[i:i + 1, :]
        rank = rank + ((row > imp) | ((row == imp) & (i < jio))).astype(I32)
    sel_bias = jnp.where(rank < min(SEL_TOP, ns), 0.0, NEG)
    return qfull, pq, o_c, sel_bias


def _nsa_selected(qfull, pq, sel_bias, ks_ref, slope, nblk):
    cols = qfull.shape[0]
    tk = nblk * SEL_BLOCK
    bias = jnp.broadcast_to(sel_bias[:nblk][:, None, :], (nblk, SEL_BLOCK, cols)).reshape(tk, cols)
    kpos = lax.broadcasted_iota(I32, (tk, cols), 0).astype(F32)
    dist_s = pq - kpos
    s_s = _nt(ks_ref[0:tk, 0:KV_W].astype(BF16), qfull) - slope * dist_s
    p_s = _softmax_cols(s_s, (dist_s >= 0) & (bias == 0.0)).astype(BF16)
    return _tn(p_s, ks_ref[0:tk, KV_W:].astype(BF16))


def _nsa_window(qfull, pq, kw, vw, pw0, slope):
    tw, cols = kw.shape[0], qfull.shape[0]
    pos_w = (pw0 + lax.broadcasted_iota(I32, (tw, cols), 0)).astype(F32)
    dist_w = pq - pos_w
    valid_w = (dist_w >= 0) & (dist_w <= WINDOW) & (pos_w >= 0)
    p_w = _softmax_cols(_nt(kw.astype(BF16), qfull) - slope * dist_w, valid_w).astype(BF16)
    return _tn(p_w, vw.astype(BF16))


def _nsa_combine(gates, e_gate, outs, tq):
    cols = N_HEADS * tq
    row_g = (lax.broadcasted_iota(I32, (cols, KV_W), 0) // tq) % N_KV
    own = row_g == lax.broadcasted_iota(I32, (cols, KV_W), 1) // HEAD_DIM

    def fold(o):
        o = jnp.where(own, o, 0.0)
        return jnp.concatenate(
            [sum(o[(r * N_KV + g) * tq:(r * N_KV + g + 1) * tq, :] for g in range(N_KV)) for r in range(GROUP)], axis=1)

    out = jnp.zeros((tq, N_HEADS * HEAD_DIM), F32)
    for j, o in enumerate(outs):
        out = out + _split_dot(gates, e_gate[j]) * fold(o)
    return out


def _nsa_prompt_body(q_ref, gt_ref, kc_ref, vc_ref, ks_ref, kw_ref, sl_ref, fr_ref, eg_ref, o_ref, os_ref,
                     *, tq, ns, tw, bucket):
    q0 = pl.program_id(1) * tq
    slope = sl_ref[...]
    qfull, pq, o_c, sel_bias = _nsa_front(q_ref[...], kc_ref[0], vc_ref[0], q0, slope, fr_ref[...], tq, ns)
    for k in range(ns // bucket):
        @pl.when(q0 // (bucket * SEL_BLOCK) == k)
        def _():
            os_ref[...] = _nsa_selected(qfull, pq, sel_bias, ks_ref.at[0], slope, (k + 1) * bucket)
    w = kw_ref[0, pl.ds(pl.multiple_of(q0, 8), tw), :]
    o_w = _nsa_window(qfull, pq, w[:, :KV_W], w[:, KV_W:], q0 - WINDOW, slope)
    o_ref[...] = _nsa_combine(gt_ref[...], eg_ref, (o_c, os_ref[...], o_w), tq)


def nsa_prompt(q, gates, kc, vc, kvs, kvw_pad, slope, frac_t, e_gate, tq, bucket):
    b, t, _ = kvs.shape
    ns = t // SEL_BLOCK
    tw = WINDOW + max(tq, 8)
    nq = t // tq
    full = lambda a: pl.BlockSpec(a.shape, lambda i, j: (0,) * a.ndim)
    perb = lambda a: pl.BlockSpec((1,) + a.shape[1:], lambda i, j: (i, 0, 0))
    row = lambda n: pl.BlockSpec((tq, n), lambda i, j: (i * nq + j, 0))
    return pl.pallas_call(
        functools.partial(_nsa_prompt_body, tq=tq, ns=ns, tw=tw, bucket=bucket),
        out_shape=jax.ShapeDtypeStruct((b * t, N_HEADS * HEAD_DIM), F32),
        grid=(b, nq),
        in_specs=[row(q.shape[1]), row(gates.shape[1]), perb(kc), perb(vc), perb(kvs), perb(kvw_pad),
                  full(slope), full(frac_t), full(e_gate)],
        out_specs=row(N_HEADS * HEAD_DIM),
        scratch_shapes=[pltpu.VMEM((N_HEADS * tq, KV_W), F32)],
        compiler_params=_cp(("arbitrary", "arbitrary")),
        name="nsa_prompt",
    )(q, gates, kc, vc, kvs, kvw_pad, slope, frac_t, e_gate)


def _nsa_sample_body(pt_ref, q_ref, gt_ref, kc_ref, vc_ref, cache_ref, knew_ref, win_ref, wnew_ref, sl_ref, fr_ref,
                     eg_ref, o_ref, buf_ref, ks_ref, sem_ref, *, tq, ns, past):
    n_pages, page = pt_ref.shape[1], cache_ref.shape[2]
    tk = ns * SEL_BLOCK

    @pl.when(pl.program_id(0) == 0)
    def _():
        ks_ref[past:, :] = jnp.zeros((tk - past, ks_ref.shape[1]), F32)

    slot = _gather_pages(cache_ref, buf_ref, sem_ref, pt_ref)
    for j in range(n_pages):
        ks_ref[pl.ds(j * page, page), :] = buf_ref[slot, j].T
    ks_ref[past:past + tq, :] = knew_ref[0]
    wbuf = win_ref.shape[2]
    tw = wbuf + SEL_BLOCK
    w = jnp.concatenate([win_ref[0].T, wnew_ref[0], jnp.zeros((tw - wbuf - tq, 2 * KV_W), F32)], axis=0)
    slope = sl_ref[...]
    qfull, pq, o_c, sel_bias = _nsa_front(q_ref[0], kc_ref[0], vc_ref[0], past, slope, fr_ref[...], tq, ns)
    o_s = _nsa_selected(qfull, pq, sel_bias, ks_ref, slope, ns)
    o_w = _nsa_window(qfull, pq, w[:, :KV_W], w[:, KV_W:], past - wbuf, slope)
    o_ref[0] = _nsa_combine(gt_ref[0], eg_ref, (o_c, o_s, o_w), tq)


def nsa_sample(page_table, q, gates, kc, vc, cache_t, kvs_new, win_t, kvw_new, slope, frac_t, e_gate):
    b, n_pages = page_table.shape
    _, width, page = cache_t.shape
    past = n_pages * page
    tq = q.shape[1]
    ns = -(-(past + tq) // SEL_BLOCK)
    tk = ns * SEL_BLOCK
    full = lambda a: pl.BlockSpec(a.shape, lambda i, pt: (0,) * a.ndim)
    perb = lambda a: pl.BlockSpec((1,) + a.shape[1:], lambda i, pt: (i, 0, 0))
    return pl.pallas_call(
        functools.partial(_nsa_sample_body, tq=tq, ns=ns, past=past),
        out_shape=jax.ShapeDtypeStruct((b, tq, N_HEADS * HEAD_DIM), F32),
        grid_spec=pltpu.PrefetchScalarGridSpec(
            num_scalar_prefetch=1, grid=(b,),
            in_specs=[perb(q), perb(gates), perb(kc), perb(vc), pl.BlockSpec(memory_space=pl.ANY), perb(kvs_new),
                      perb(win_t), perb(kvw_new), full(slope), full(frac_t), full(e_gate)],
            out_specs=pl.BlockSpec((1, tq, N_HEADS * HEAD_DIM), lambda i, pt: (i, 0, 0)),
            scratch_shapes=[pltpu.VMEM((2, n_pages, width, page), F32), pltpu.VMEM((tk, width), F32),
                            pltpu.SemaphoreType.DMA((2,))]),
        compiler_params=_cp(("arbitrary",)),
        name="nsa_sample",
    )(page_table, q, gates, kc, vc, cache_t, kvs_new, win_t, kvw_new, slope, frac_t, e_gate)


def _pool_diff_body(ext_ref, d_ref, *, t, pos0):
    gw = ext_ref.shape[2] // len(POOL_SIZES)
    pos = pos0 + lax.broadcasted_iota(I32, (t, gw), 0)
    for gi, w in enumerate(POOL_SIZES):
        lanes = pl.ds(gi * gw, gw)
        cur = ext_ref[0, pl.ds(POOL_BUF, t), lanes]
        acc = cur
        for j in range(1, w):
            acc = acc + ext_ref[0, pl.ds(POOL_BUF - j, t), lanes]
        cnt = jnp.minimum(w, pos + 1).astype(F32)
        d_ref[0, :, lanes] = acc / cnt - cur


def pool_diff(ext, pos0):
    b, rows, width = ext.shape
    t = rows - POOL_BUF
    return pl.pallas_call(
        functools.partial(_pool_diff_body, t=t, pos0=pos0),
        out_shape=jax.ShapeDtypeStruct((b, t, width), F32),
        grid=(b,),
        in_specs=[pl.BlockSpec((1, rows, width), lambda i: (i, 0, 0))],
        out_specs=pl.BlockSpec((1, t, width), lambda i: (i, 0, 0)),
        compiler_params=_cp(("arbitrary",)),
        name="pool_diff",
    )(ext)


def _outproj_body(o_ref, d_ref, x_ref, gate_ref, sh_ref, sc_ref, g2_ref, wn_ref, wp_ref, wpool_ref, ps_ref,
                  x1_ref, h2_ref):
    gw = wpool_ref.shape[1]
    d = d_ref[...]
    yp = jnp.concatenate([_dot(d[:, g * gw:(g + 1) * gw].astype(BF16), wpool_ref[g]) for g in range(len(POOL_SIZES))],
                         axis=1) * ps_ref[...]
    mix = _dot(o_ref[...].astype(BF16), wn_ref[...]) + _dot(yp.astype(BF16), wp_ref[...])
    x1 = x_ref[...] + gate_ref[0] * mix
    x1_ref[...] = x1
    h2 = _rms_rows(x1) * g2_ref[...]
    h2_ref[...] = (h2 * (1.0 + sc_ref[0]) + sh_ref[0]).astype(BF16)


def outproj(o, d, x, gate, shift, scale, g2, wn, wp, wpool, ps, tm, tiles_per_mod):
    t, dm = x.shape
    full = lambda a: pl.BlockSpec(a.shape, lambda i: (0,) * a.ndim)
    row = lambda n: pl.BlockSpec((tm, n), lambda i: (i, 0))
    ms = lambda a: _mod_spec(a, tm, tiles_per_mod)
    return pl.pallas_call(
        _outproj_body,
        out_shape=(jax.ShapeDtypeStruct((t, dm), F32), jax.ShapeDtypeStruct((t, dm), BF16)),
        grid=(t // tm,),
        in_specs=[row(o.shape[1]), row(d.shape[1]), row(dm), ms(gate), ms(shift), ms(scale), full(g2),
                  full(wn), full(wp), full(wpool), full(ps)],
        out_specs=(row(dm), row(dm)),
        compiler_params=_cp(("arbitrary",)),
        name="outproj",
    )(o, d, x, gate, shift, scale, g2, wn, wp, wpool, ps)


_CAND_PIECES = ((0, 16), (1, 8), (2, 8), (3, 8), (4, 8), (5, 8), (6, 8), (7, 8))


def _top16_rows(s):
    kio = lax.broadcasted_iota(I32, s.shape, 0)
    nk = s.shape[0]
    rank = jnp.full(s.shape, float(PEER_TOPK), F32)
    vals, keys = [], []
    for it in range(PEER_TOPK):
        m = jnp.max(s, axis=0, keepdims=True)
        idx = jnp.min(jnp.where(s == m, kio, nk), axis=0, keepdims=True)
        hit = kio == idx
        rank = jnp.where(hit, float(it), rank)
        s = jnp.where(hit, -jnp.inf, s)
        vals.append(m)
        keys.append(idx)
    return vals, keys, rank


def _peer_route_body(h_ref, wq_ref, sk_ref, rk_ref, lim_ref, e1_ref, e2_ref, q_scr):
    hd = pl.program_id(1)

    qd = sk_ref.shape[3]

    @pl.when(hd == 0)
    def _():
        q = _dot(h_ref[...], wq_ref[...]).astype(BF16)
        for i in range(PEER_HEADS):
            q_scr[i] = q[:, i * 2 * qd:(i + 1) * 2 * qd]

    qh = q_scr[hd]
    s1 = _nt(sk_ref[0, 0], qh[:, :qd])
    s2 = _nt(sk_ref[1, 0], qh[:, qd:])
    v1, k1, _ = _top16_rows(s1)
    v2, _, rank2 = _top16_rows(s2)
    n = s1.shape[1]
    v2a = jnp.concatenate(v2, axis=0)
    top = v1[0] + v2[0]

    bio = {nb: lax.broadcasted_iota(I32, (nb, n), 0) for nb in (8, PEER_TOPK)}
    v2p = {8: jnp.concatenate(v2[:8], axis=0), PEER_TOPK: v2a}
    cands, flats = [], []
    for a, nb in _CAND_PIECES:
        c = v1[a] + v2p[nb]
        ok = (bio[nb] + 1) * (a + 1) <= PEER_TOPK
        cands.append(jnp.where(ok, c, -jnp.inf))
        flats.append(a * PEER_TOPK + bio[nb])
    v1b = jnp.concatenate(v1[8:], axis=0)
    cands.append(v1b + v2[0])
    flats.append((bio[8] + 8) * PEER_TOPK)
    cand = jnp.concatenate(cands, axis=0)
    flat = jnp.concatenate(flats, axis=0)
    big = PEER_TOPK * PEER_TOPK
    taken = jnp.zeros(cand.shape, jnp.bool_)
    z = jnp.zeros((1, n), F32)
    for _ in range(PEER_TOPK):
        m = jnp.max(cand, axis=0, keepdims=True)
        f = jnp.min(jnp.where(cand == m, flat, big), axis=0, keepdims=True)
        hit = flat == f
        taken = taken | hit
        cand = jnp.where(hit, -jnp.inf, cand)
        z = z + jnp.exp(m - top)
    takenf = taken.astype(F32)

    lim = jnp.zeros(s1.shape, F32)
    kio = lax.broadcasted_iota(I32, s1.shape, 0)
    off = 0
    for a, nb in _CAND_PIECES:
        cnt = jnp.sum(takenf[off:off + nb], axis=0, keepdims=True)
        lim = jnp.where(kio == k1[a], cnt, lim)
        off += nb
    for i in range(8):
        lim = jnp.where(kio == k1[8 + i], takenf[off + i:off + i + 1], lim)

    rk_ref[0] = rank2.astype(BF16)
    lim_ref[0] = lim
    e1_ref[0] = jnp.exp(s1 - v1[0]) / z
    e2_ref[0] = jnp.exp(s2 - v2[0]).astype(BF16)


def peer_route(h2, wq, sk, tt):
    t, d = h2.shape
    nk = sk.shape[2]
    out = jax.ShapeDtypeStruct((PEER_HEADS, nk, t), F32)
    outb = jax.ShapeDtypeStruct((PEER_HEADS, nk, t), BF16)
    ospec = pl.BlockSpec((1, nk, tt), lambda i, h: (h, 0, i))
    return pl.pallas_call(
        _peer_route_body,
        out_shape=(outb, out, out, outb),
        grid=(t // tt, PEER_HEADS),
        in_specs=[pl.BlockSpec((tt, d), lambda i, h: (i, 0)),
                  pl.BlockSpec(wq.shape, lambda i, h: (0, 0)),
                  pl.BlockSpec((2, 1) + sk.shape[2:], lambda i, h: (0, h, 0, 0))],
        out_specs=(ospec, ospec, ospec, ospec),
        scratch_shapes=[pltpu.VMEM((PEER_HEADS, tt, wq.shape[1] // PEER_HEADS), BF16)],
        compiler_params=_cp(("arbitrary", "arbitrary")),
        name="peer_route",
    )(h2, wq, sk)


def _peer_dense_body(h_ref, u_ref, v_ref, rk_ref, lim_ref, e1_ref, e2_ref, x1_ref, gate_ref, y_ref):
    j = pl.program_id(1)
    nk = rk_ref.shape[1]
    et, tt = u_ref.shape[0], h_ref.shape[0]

    a = _nt(u_ref[...], h_ref[...])
    act = a * (lax.erf(a * np.float32(1.0 / np.sqrt(2.0))) + 1.0) * 0.5
    zero = jnp.zeros((), BF16)
    parts = []
    for rr in range(et // nk):
        r = j * (et // nk) + rr
        w = jnp.zeros((nk, tt), BF16)
        for hd in range(PEER_HEADS):
            lim = lim_ref[hd, pl.ds(r, 1), :].astype(BF16)
            e1 = e1_ref[hd, pl.ds(r, 1), :].astype(BF16)
            w = w + jnp.where(rk_ref[hd] < lim, e2_ref[hd], zero) * e1
        parts.append(w)
    wt = (act * jnp.concatenate(parts, axis=0).astype(F32)).astype(BF16)
    contrib = _tn(wt, v_ref[...])

    @pl.when(j == 0)
    def _():
        y_ref[...] = contrib

    @pl.when(j > 0)
    def _():
        y_ref[...] += contrib

    @pl.when(j == pl.num_programs(1) - 1)
    def _():
        y_ref[...] = x1_ref[...] + gate_ref[0] * y_ref[...]


def peer_dense(h2, u, v, rk, lim, e1, e2, x1, gate, tt, et, tiles_per_mod):
    t, d = h2.shape
    ne = u.shape[0]
    nk = rk.shape[1]
    tok = pl.BlockSpec((tt, d), lambda i, j: (i, 0))
    exp = pl.BlockSpec((et, d), lambda i, j: (j, 0))
    rt = pl.BlockSpec((PEER_HEADS, nk, tt), lambda i, j: (0, 0, i))
    if gate.shape[1] == 1:
        gspec = pl.BlockSpec((1, 1, d), lambda i, j: (i // tiles_per_mod, 0, 0))
    else:
        gspec = pl.BlockSpec((1, tt, d), lambda i, j: (i, 0, 0))
    return pl.pallas_call(
        _peer_dense_body,
        out_shape=jax.ShapeDtypeStruct((t, d), F32),
        grid=(t // tt, ne // et),
        in_specs=[tok, exp, exp, rt, rt, rt, rt, tok, gspec],
        out_specs=tok,
        compiler_params=_cp(("arbitrary", "arbitrary")),
        name="peer_dense",
    )(h2, u, v, rk, lim, e1, e2, x1, gate)


def _slope_lanes(tq):
    h = np.arange(1, N_HEADS + 1, dtype=np.float32)
    s = (2.0 ** (-8.0 * h / N_HEADS)).reshape(N_KV, GROUP)
    return jnp.asarray(np.repeat(s.T.reshape(-1), tq)[None, :], F32)


def _frac_t(nc, ns, nc_pad, ns_pad):
    pos = np.arange(nc)[:, None] * CMP_STRIDE + np.arange(CMP_BLOCK)[None, :]
    f = ((pos // SEL_BLOCK)[:, :, None] == np.arange(ns)[None, None, :]).mean(axis=1)
    out = np.zeros((ns_pad, nc_pad), np.float32)
    out[:ns, :nc] = f.T
    return jnp.asarray(out)


def _gate_expand(width):
    e = np.zeros((3, width, N_HEADS * HEAD_DIM), np.float32)
    for g in range(N_KV):
        for r in range(GROUP):
            for j in range(3):
                c0 = r * KV_W + g * HEAD_DIM
                e[j, (g * GROUP + r) * 3 + j, c0:c0 + HEAD_DIM] = 1.0
    return jnp.asarray(e, BF16)


def _rgd(a, axis):
    shp = a.shape
    a = a.reshape(shp[:axis] + (N_KV, GROUP, HEAD_DIM) + shp[axis + 1:])
    a = jnp.swapaxes(a, axis, axis + 1)
    return a.reshape(shp)


def kernel(x_prompt, x_sample, cache_kv_cmp, cache_kv_slc, cache_kv_win, state_pool, page_table, c_prompt, c_sample,
           w_ada, b_ada, g_norm1, g_norm2, w_in, w_out, g_qnorm, g_knorm, cmp_pe, w_cmp, w_pool, pool_scale,
           peer_wq, peer_subkeys, peer_u, peer_v):
    bp, tp, dm = x_prompt.shape
    bs, ts, _ = x_sample.shape
    n_pages = page_table.shape[1]
    page = cache_kv_cmp.shape[1]
    past = n_pages * page
    nsa_w = N_HEADS * HEAD_DIM
    kv3 = 3 * 2 * KV_W
    ngl = 3 * N_HEADS
    gpad = 128
    tm, tq, sel_bucket, tt, et = 256, 32, 8, 512, 512

    wq = _rgd(w_in[:, :nsa_w], 1).astype(BF16)
    wkv = w_in[:, nsa_w:nsa_w + kv3].astype(BF16)
    wg = jnp.pad(w_in[:, nsa_w + kv3:nsa_w + kv3 + ngl], ((0, 0), (0, gpad - ngl))).astype(BF16)
    wu = w_in[:, nsa_w + kv3 + ngl:].astype(BF16)
    gq = jnp.tile(g_qnorm, N_HEADS)[None, :]
    gk = jnp.tile(g_knorm, (1, N_KV))
    g1 = g_norm1[None, :]
    g2 = g_norm2[None, :]
    wn = _rgd(w_out[:nsa_w], 0).astype(BF16)
    wp = w_out[nsa_w:].astype(BF16)
    wpool = w_pool.astype(BF16)
    ps = pool_scale[None, :]
    wc = jnp.einsum('cglde,gh->clgdhe', w_cmp, jnp.eye(N_KV, dtype=F32)).reshape(2, CMP_BLOCK, KV_W, KV_W).astype(BF16)
    pe = jnp.transpose(cmp_pe, (1, 0, 2, 3)).reshape(CMP_BLOCK, 2 * KV_W)
    e_gate = _gate_expand(gpad)
    pwq = peer_wq.astype(BF16)
    psk = peer_subkeys.astype(BF16)
    pu = peer_u.astype(BF16)
    pv = peer_v.astype(BF16)

    mod = adaln(jnp.concatenate([c_prompt, c_sample], axis=0), w_ada, b_ada).reshape(bp + bs, 6, dm)
    mod_p = [mod[:bp, k][:, None, :] for k in range(6)]
    mod_s = [jnp.repeat(mod[bp:, k], ts, axis=0).reshape(bs * ts // tm, tm, dm) for k in range(6)]

    xp = x_prompt.reshape(bp * tp, dm)
    q, kvc, kvs, kvw, gates, u = inproj(xp, mod_p[0], mod_p[1], g1, wq, wkv, wg, wu, gq, gk, tm, tp // tm)
    kc, vc = compress_prompt(kvc.reshape(bp, tp // CMP_STRIDE, CMP_STRIDE * 2 * KV_W), pe, wc, gk)
    nc = (tp - CMP_BLOCK) // CMP_STRIDE + 1
    kvw3 = kvw.reshape(bp, tp, 2 * KV_W)
    o_nsa = nsa_prompt(q, gates, kc, vc, kvs.reshape(bp, tp, 2 * KV_W), jnp.pad(kvw3, ((0, 0), (WINDOW, 0), (0, 0))),
                       _slope_lanes(tq), _frac_t(nc, tp // SEL_BLOCK, tp // CMP_STRIDE, 40), e_gate, tq, sel_bucket)
    u3 = u.reshape(bp, tp, -1)
    d_pool = pool_diff(jnp.pad(u3, ((0, 0), (POOL_BUF, 0), (0, 0))), 0).reshape(bp * tp, -1)
    x1, h2 = outproj(o_nsa, d_pool, xp, mod_p[2], mod_p[3], mod_p[4], g2, wn, wp, wpool, ps, tm, tp // tm)
    routes = peer_route(h2, pwq, psk, 256)
    y_prompt = peer_dense(h2, pu, pv, *routes, x1, mod_p[5], tt, et, tp // tt).reshape(bp, tp, dm)
    shp_p = (bp, tp, 2, N_KV, HEAD_DIM)
    win_p = kvw3[:, -min(WINDOW, tp):].reshape(bp, -1, 2, N_KV, HEAD_DIM)
    pool_p = u3[:, -POOL_BUF:]

    xs = x_sample.reshape(bs * ts, dm)
    q, kvc_s, kvs_s, kvw_s, gates, u = inproj(xs, mod_s[0], mod_s[1], g1, wq, wkv, wg, wu, gq, gk, tm, 1)
    feat_major = lambda c: jnp.transpose(c, (0, 2, 3, 4, 1)).reshape(c.shape[0], 2 * KV_W, c.shape[1])
    kc, vc = compress_sample(page_table, feat_major(cache_kv_cmp), pe, wc, gk)
    nc = (past + ts - CMP_BLOCK) // CMP_STRIDE + 1
    ns = -(-(past + ts) // SEL_BLOCK)
    kvw_s3 = kvw_s.reshape(bs, ts, 2 * KV_W)
    o_nsa = nsa_sample(page_table, q.reshape(bs, ts, -1), gates.reshape(bs, ts, -1), kc, vc,
                       feat_major(cache_kv_slc), kvs_s.reshape(bs, ts, 2 * KV_W), feat_major(cache_kv_win), kvw_s3,
                       _slope_lanes(ts), _frac_t(nc, ns, past // CMP_STRIDE, 40), e_gate).reshape(bs * ts, -1)
    ext = jnp.concatenate([state_pool, u.reshape(bs, ts, -1)], axis=1)
    d_pool = pool_diff(ext, past).reshape(bs * ts, -1)
    x1, h2 = outproj(o_nsa, d_pool, xs, mod_s[2], mod_s[3], mod_s[4], g2, wn, wp, wpool, ps, tm, 1)
    routes = peer_route(h2, pwq, psk, 256)
    gate5 = mod_s[5].reshape(bs * ts // tt, tt, dm)
    y_sample = peer_dense(h2, pu, pv, *routes, x1, gate5, tt, et, 1).reshape(bs, ts, dm)
    shp_s = (bs, ts, 2, N_KV, HEAD_DIM)
    wbuf = cache_kv_win.shape[1]
    win_s = jnp.concatenate([cache_kv_win, kvw_s.reshape(shp_s)], axis=1)[:, -wbuf:]
    pool_s = ext[:, -POOL_BUF:]

    return (y_prompt, y_sample, kvc.reshape(shp_p), kvs.reshape(shp_p), win_p, pool_p,
            kvc_s.reshape(shp_s), kvs_s.reshape(shp_s), win_s, pool_s)
```

```python
import functools

import numpy as np
import jax
import jax.numpy as jnp
from jax import lax
from jax.experimental import pallas as pl
from jax.experimental.pallas import tpu as pltpu

F32, BF16, I32 = jnp.float32, jnp.bfloat16, jnp.int32

HEAD_DIM = 64
N_KV = 4
GROUP = 4
N_HEADS = N_KV * GROUP
CMP_BLOCK = 32
CMP_STRIDE = 16
SEL_BLOCK = 64
SEL_TOP = 16
WINDOW = 512
POOL_SIZES = (2, 4, 8, 16)
POOL_BUF = max(POOL_SIZES) - 1
PEER_HEADS = 8
N_KEYS = 128
PEER_TOPK = 16
EPS = 1e-6
NEG = -1e30
FORCE_BONUS = 1e4
KV_W = N_KV * HEAD_DIM
VMEM_LIMIT = 56 * 1024 * 1024


def _cp(sem, vmem=VMEM_LIMIT):
    return pltpu.CompilerParams(dimension_semantics=sem, vmem_limit_bytes=vmem)


def _dot(a, b):
    return jnp.dot(a, b, preferred_element_type=F32)


def _nt(a, b):
    return lax.dot_general(a, b, (((1,), (1,)), ((), ())), preferred_element_type=F32)


def _tn(a, b):
    return lax.dot_general(a, b, (((0,), (0,)), ((), ())), preferred_element_type=F32)


def _split_dot(x, w):
    hi = x.astype(BF16)
    r1 = x - hi.astype(F32)
    mid = r1.astype(BF16)
    lo = (r1 - mid.astype(F32)).astype(BF16)
    return _dot(hi, w) + _dot(mid, w) + _dot(lo, w)


def _ones64():
    r = lax.broadcasted_iota(I32, (KV_W, KV_W), 0) // HEAD_DIM
    c = lax.broadcasted_iota(I32, (KV_W, KV_W), 1) // HEAD_DIM
    return (r == c).astype(BF16)


def _rms64(x, ones):
    ms = _split_dot(x * x, ones) * (1.0 / HEAD_DIM)
    return x * lax.rsqrt(ms + EPS)


def _rms_rows(x):
    return x * lax.rsqrt(jnp.mean(x * x, axis=-1, keepdims=True) + EPS)


def _adaln_body(c_ref, w_ref, b_ref, o_ref):
    c = c_ref[...]
    s = (c * jax.nn.sigmoid(c)).astype(BF16)
    o_ref[...] = _dot(s, w_ref[...].astype(BF16)) + b_ref[...]


def adaln(c_all, w_ada, b_ada, tn=1024):
    m, d = c_all.shape
    n = w_ada.shape[1]
    return pl.pallas_call(
        _adaln_body,
        out_shape=jax.ShapeDtypeStruct((m, n), F32),
        grid=(n // tn,),
        in_specs=[pl.BlockSpec((m, d), lambda j: (0, 0)),
                  pl.BlockSpec((d, tn), lambda j: (0, j)),
                  pl.BlockSpec((1, tn), lambda j: (0, j))],
        out_specs=pl.BlockSpec((m, tn), lambda j: (0, j)),
        compiler_params=_cp(("arbitrary",)),
        name="adaln",
    )(c_all, w_ada, b_ada.reshape(1, n))


def _inproj_body(x_ref, sh_ref, sc_ref, g1_ref, wq_ref, wkv_ref, wg_ref, wu_ref, gq_ref, gk_ref,
                 q_ref, kvc_ref, kvs_ref, kvw_ref, gt_ref, u_ref):
    h = _rms_rows(x_ref[...]) * g1_ref[...]
    h = h * (1.0 + sc_ref[0]) + sh_ref[0]
    hb = h.astype(BF16)
    ones = _ones64()
    zq = _dot(hb, wq_ref[...])
    for r in range(GROUP):
        sl = slice(r * KV_W, (r + 1) * KV_W)
        q_ref[:, sl] = _rms64(zq[:, sl], ones) * gq_ref[:, sl] * (HEAD_DIM ** -0.5)
    zkv = _dot(hb, wkv_ref[...])
    kvc_ref[...] = zkv[:, 0:2 * KV_W]
    kvs_ref[:, 0:KV_W] = _rms64(zkv[:, 2 * KV_W:3 * KV_W], ones) * gk_ref[1:2, :]
    kvs_ref[:, KV_W:] = zkv[:, 3 * KV_W:4 * KV_W]
    kvw_ref[:, 0:KV_W] = _rms64(zkv[:, 4 * KV_W:5 * KV_W], ones) * gk_ref[2:3, :]
    kvw_ref[:, KV_W:] = zkv[:, 5 * KV_W:6 * KV_W]
    gt_ref[...] = jax.nn.sigmoid(_dot(hb, wg_ref[...]))
    u_ref[...] = _dot(hb, wu_ref[...])


def _mod_spec(arr, tm, rows_per_mod):
    d = arr.shape[-1]
    if arr.shape[1] == 1:
        return pl.BlockSpec((1, 1, d), lambda i: (i // rows_per_mod, 0, 0))
    return pl.BlockSpec((1, tm, d), lambda i: (i, 0, 0))


def inproj(x, shift, scale, g1, wq, wkv, wg, wu, gq, gk, tm, tiles_per_mod):
    t, d = x.shape
    nq, nkv, ng, nu = wq.shape[1], wkv.shape[1], wg.shape[1], wu.shape[1]
    full = lambda a: pl.BlockSpec(a.shape, lambda i: (0,) * a.ndim)
    row = lambda n: pl.BlockSpec((tm, n), lambda i: (i, 0))
    return pl.pallas_call(
        _inproj_body,
        out_shape=(jax.ShapeDtypeStruct((t, nq), F32), jax.ShapeDtypeStruct((t, 2 * KV_W), F32),
                   jax.ShapeDtypeStruct((t, 2 * KV_W), F32), jax.ShapeDtypeStruct((t, 2 * KV_W), F32),
                   jax.ShapeDtypeStruct((t, ng), F32), jax.ShapeDtypeStruct((t, nu), F32)),
        grid=(t // tm,),
        in_specs=[row(d), _mod_spec(shift, tm, tiles_per_mod), _mod_spec(scale, tm, tiles_per_mod), full(g1),
                  full(wq), full(wkv), full(wg), full(wu), full(gq), full(gk)],
        out_specs=(row(nq), row(2 * KV_W), row(2 * KV_W), row(2 * KV_W), row(ng), row(nu)),
        compiler_params=_cp(("arbitrary",)),
        name="inproj",
    )(x, shift, scale, g1, wq, wkv, wg, wu, gq, gk)


def _compress(get_x, nchunk, pe_ref, w_ref, gk_ref, kc_ref, vc_ref):
    half = CMP_BLOCK // 2
    acc = [jnp.zeros((nchunk, KV_W), F32) for _ in range(4)]
    for l in range(half):
        xl = get_x(l)
        a = (xl + pe_ref[l:l + 1, :]).astype(BF16)
        b = (xl + pe_ref[half + l:half + l + 1, :]).astype(BF16)
        acc[0] += _dot(a[:, :KV_W], w_ref[0, l])
        acc[1] += _dot(a[:, KV_W:], w_ref[1, l])
        acc[2] += _dot(b[:, :KV_W], w_ref[0, half + l])
        acc[3] += _dot(b[:, KV_W:], w_ref[1, half + l])
    rio = lax.broadcasted_iota(I32, (nchunk, KV_W), 0)
    nxt = lambda v: jnp.where(rio < nchunk - 1, pltpu.roll(v, nchunk - 1, 0), 0.0)
    ck = acc[0] + nxt(acc[2])
    cv = acc[1] + nxt(acc[3])
    kc_ref[...] = _rms64(ck, _ones64()) * gk_ref[0:1, :]
    vc_ref[...] = cv


def _compress_prompt_body(x_ref, pe_ref, w_ref, gk_ref, kc_ref, vc_ref):
    row_w = 2 * KV_W
    _compress(lambda l: x_ref[0, :, l * row_w:(l + 1) * row_w], x_ref.shape[1], pe_ref, w_ref, gk_ref,
              kc_ref.at[0], vc_ref.at[0])


def compress_prompt(kv_chunks, pe, wc, gk):
    b, nchunk, width = kv_chunks.shape
    full = lambda a: pl.BlockSpec(a.shape, lambda i: (0,) * a.ndim)
    out = jax.ShapeDtypeStruct((b, nchunk, KV_W), F32)
    ospec = pl.BlockSpec((1, nchunk, KV_W), lambda i: (i, 0, 0))
    return pl.pallas_call(
        _compress_prompt_body,
        out_shape=(out, out),
        grid=(b,),
        in_specs=[pl.BlockSpec((1, nchunk, width), lambda i: (i, 0, 0)), full(pe), full(wc), full(gk)],
        out_specs=(ospec, ospec),
        compiler_params=_cp(("arbitrary",)),
        name="compress_prompt",
    )(kv_chunks, pe, wc, gk)


def _page_copy(cache_ref, buf_ref, sem_ref, pt_ref, seq, slot, j):
    return pltpu.make_async_copy(cache_ref.at[pt_ref[seq, j]], buf_ref.at[slot, j], sem_ref.at[slot])


def _gather_pages(cache_ref, buf_ref, sem_ref, pt_ref):
    b = pl.program_id(0)
    nb = pl.num_programs(0)
    n_pages = pt_ref.shape[1]
    slot = b % 2

    def start(seq, s):
        for j in range(n_pages):
            _page_copy(cache_ref, buf_ref, sem_ref, pt_ref, seq, s, j).start()

    @pl.when(b == 0)
    def _():
        start(b, slot)

    @pl.when(b + 1 < nb)
    def _():
        start(b + 1, 1 - slot)

    for j in range(n_pages):
        _page_copy(cache_ref, buf_ref, sem_ref, pt_ref, b, slot, j).wait()
    return slot


LANES = 128
BF16_ROWS = 16


def _compress_sample_body(pt_ref, cache_ref, pe_ref, w_ref, gk_ref, kc_ref, vc_ref, buf_ref, xs_ref, sem_ref):
    n_pages, page = pt_ref.shape[1], cache_ref.shape[2]
    slot = _gather_pages(cache_ref, buf_ref, sem_ref, pt_ref)
    nlb = xs_ref.shape[0]
    for j in range(n_pages):
        for k in range(nlb):
            xs_ref[k, pl.ds(j * page, page), :] = buf_ref[slot, j, pl.ds(k * LANES, LANES), :].T
    nchunk = n_pages * page // CMP_STRIDE
    get_x = lambda l: jnp.concatenate([xs_ref[k, pl.ds(l, nchunk, stride=CMP_STRIDE), :] for k in range(nlb)], axis=1)
    _compress(get_x, nchunk, pe_ref, w_ref, gk_ref, kc_ref.at[0], vc_ref.at[0])


def compress_sample(page_table, cache_t, pe, wc, gk):
    b, n_pages = page_table.shape
    _, width, page = cache_t.shape
    nchunk = n_pages * page // CMP_STRIDE
    full = lambda a: pl.BlockSpec(a.shape, lambda i, pt: (0,) * a.ndim)
    out = jax.ShapeDtypeStruct((b, nchunk, KV_W), F32)
    ospec = pl.BlockSpec((1, nchunk, KV_W), lambda i, pt: (i, 0, 0))
    return pl.pallas_call(
        _compress_sample_body,
        out_shape=(out, out),
        grid_spec=pltpu.PrefetchScalarGridSpec(
            num_scalar_prefetch=1, grid=(b,),
            in_specs=[pl.BlockSpec(memory_space=pl.ANY), full(pe), full(wc), full(gk)],
            out_specs=(ospec, ospec),
            scratch_shapes=[pltpu.VMEM((2, n_pages, width, page), F32),
                            pltpu.VMEM((width // LANES, n_pages * page, LANES), F32),
                            pltpu.SemaphoreType.DMA((2,))]),
        compiler_params=_cp(("arbitrary",)),
        name="compress_sample",
    )(page_table, cache_t, pe, wc, gk)


def _softmax_cols(s, valid):
    s = jnp.where(valid, s, NEG)
    m = jnp.max(s, axis=0, keepdims=True)
    e = jnp.where(valid, jnp.exp(s - m), 0.0)
    return e / jnp.maximum(jnp.sum(e, axis=0, keepdims=True), 1e-30)


def _nsa_front(q, kc, vc, pq0, slope, frac_t, tq, ns):
    cols = N_HEADS * tq
    lane = lax.broadcasted_iota(I32, (1, cols), 1)
    pq = (pq0 + lane % tq).astype(F32)
    lg = lax.broadcasted_iota(I32, (tq, KV_W), 1) // HEAD_DIM
    qfull = jnp.concatenate([jnp.where(lg == g, q[:, r * KV_W:(r + 1) * KV_W], 0.0)
                             for r in range(GROUP) for g in range(N_KV)], axis=0).astype(BF16)

    nc = kc.shape[0]
    pos_c = (lax.broadcasted_iota(I32, (nc, cols), 0) * CMP_STRIDE + (CMP_BLOCK - 1)).astype(F32)
    dist_c = pq - pos_c
    p_c = _softmax_cols(_nt(kc.astype(BF16), qfull) - slope * dist_c, dist_c >= 0).astype(BF16)
    o_c = _tn(p_c, vc.astype(BF16))

    imp = _dot(frac_t.astype(BF16), p_c)
    imp = imp + pltpu.roll(imp, 4 * tq, 1) + pltpu.roll(imp, 8 * tq, 1) + pltpu.roll(imp, 12 * tq, 1)
    nsp = imp.shape[0]
    jio = lax.broadcasted_iota(I32, (nsp, cols), 0)
    blk_q = (pq0 + lane % tq) // SEL_BLOCK
    forced = (jio == 0) | (jio == blk_q) | (jio == blk_q - 1)
    valid_blk = (jio * SEL_BLOCK).astype(F32) <= pq
    imp = jnp.where(valid_blk, imp + FORCE_BONUS * forced.astype(F32), NEG)
    rank = jnp.zeros((nsp, cols), I32)
    for i in range(ns):
        row = imp[i:i + 1, :]
        rank = rank + ((row > imp) | ((row == imp) & (i < jio))).astype(I32)
    sel_bias = jnp.where(rank < min(SEL_TOP, ns), 0.0, NEG)
    return qfull, pq, o_c, sel_bias


def _nsa_selected(qfull, pq, sel_bias, ks_ref, slope, nblk, blk_q):
    cols = qfull.shape[0]
    nsp = sel_bias.shape[0]
    tk = nblk * SEL_BLOCK
    jio = lax.broadcasted_iota(I32, (nsp, cols), 0)
    bias = jnp.where(jio < blk_q, sel_bias, NEG) + slope * (jio * SEL_BLOCK).astype(F32)
    inblk = slope * lax.broadcasted_iota(I32, (SEL_BLOCK, cols), 0).astype(F32)
    s_m = _nt(ks_ref[0:tk, 0:KV_W].astype(BF16), qfull).reshape(nblk, SEL_BLOCK, cols)
    s_m = s_m + inblk[None, :, :] + bias[:nblk][:, None, :]

    d0 = blk_q * SEL_BLOCK
    kd = ks_ref[pl.ds(d0 if isinstance(d0, int) else pl.multiple_of(d0, SEL_BLOCK), SEL_BLOCK), :]
    kpos_d = (blk_q * SEL_BLOCK + lax.broadcasted_iota(I32, (SEL_BLOCK, cols), 0)).astype(F32)
    valid_d = kpos_d <= pq
    s_d = jnp.where(valid_d, _nt(kd[:, :KV_W].astype(BF16), qfull) + slope * kpos_d, NEG)

    m = jnp.maximum(jnp.max(jnp.max(s_m, axis=0), axis=0, keepdims=True), jnp.max(s_d, axis=0, keepdims=True))
    e_m = jnp.exp(s_m - m[None, :, :])
    e_d = jnp.where(valid_d, jnp.exp(s_d - m), 0.0)
    l = jnp.sum(jnp.sum(e_m, axis=0), axis=0, keepdims=True) + jnp.sum(e_d, axis=0, keepdims=True)
    inv = 1.0 / jnp.maximum(l, 1e-30)
    p_m = (e_m * inv[None, :, :]).astype(BF16).reshape(tk, cols)
    p_d = (e_d * inv).astype(BF16)
    return _tn(p_m, ks_ref[0:tk, KV_W:].astype(BF16)) + _tn(p_d, kd[:, KV_W:].astype(BF16))


def _nsa_window(qfull, pq, kw, vw, pw0, slope):
    tw, cols = kw.shape[0], qfull.shape[0]
    pos_w = (pw0 + lax.broadcasted_iota(I32, (tw, cols), 0)).astype(F32)
    dist_w = pq - pos_w
    valid_w = (dist_w >= 0) & (dist_w <= WINDOW) & (pos_w >= 0)
    p_w = _softmax_cols(_nt(kw.astype(BF16), qfull) - slope * dist_w, valid_w).astype(BF16)
    return _tn(p_w, vw.astype(BF16))


def _nsa_combine(gates, e_gate, outs, tq):
    cols = N_HEADS * tq
    row_g = (lax.broadcasted_iota(I32, (cols, KV_W), 0) // tq) % N_KV
    own = row_g == lax.broadcasted_iota(I32, (cols, KV_W), 1) // HEAD_DIM

    def fold(o):
        o = jnp.where(own, o, 0.0)
        return jnp.concatenate(
            [sum(o[(r * N_KV + g) * tq:(r * N_KV + g + 1) * tq, :] for g in range(N_KV)) for r in range(GROUP)], axis=1)

    out = jnp.zeros((tq, N_HEADS * HEAD_DIM), F32)
    for j, o in enumerate(outs):
        out = out + _split_dot(gates, e_gate[j]) * fold(o)
    return out


def _nsa_prompt_body(q_ref, gt_ref, kc_ref, vc_ref, ks_ref, kw_ref, sl_ref, fr_ref, eg_ref, o_ref, os_ref,
                     *, tq, ns, tw, bucket):
    q0 = pl.program_id(1) * tq
    slope = sl_ref[...]
    qfull, pq, o_c, sel_bias = _nsa_front(q_ref[...], kc_ref[0], vc_ref[0], q0, slope, fr_ref[...], tq, ns)
    blk_q = q0 // SEL_BLOCK
    for k in range(ns // bucket):
        @pl.when(blk_q // bucket == k)
        def _():
            os_ref[...] = _nsa_selected(qfull, pq, sel_bias, ks_ref.at[0], slope, (k + 1) * bucket, blk_q)
    w = kw_ref[0, pl.ds(pl.multiple_of(q0, 8), tw), :]
    o_w = _nsa_window(qfull, pq, w[:, :KV_W], w[:, KV_W:], q0 - WINDOW, slope)
    o_ref[...] = _nsa_combine(gt_ref[...], eg_ref, (o_c, os_ref[...], o_w), tq)


def nsa_prompt(q, gates, kc, vc, kvs, kvw_pad, slope, frac_t, e_gate, tq, bucket):
    b, t, _ = kvs.shape
    ns = t // SEL_BLOCK
    assert SEL_BLOCK % tq == 0 and ns % bucket == 0 and t % SEL_BLOCK == 0
    tw = WINDOW + max(tq, 8)
    nq = t // tq
    full = lambda a: pl.BlockSpec(a.shape, lambda i, j: (0,) * a.ndim)
    perb = lambda a: pl.BlockSpec((1,) + a.shape[1:], lambda i, j: (i, 0, 0))
    row = lambda n: pl.BlockSpec((tq, n), lambda i, j: (i * nq + j, 0))
    return pl.pallas_call(
        functools.partial(_nsa_prompt_body, tq=tq, ns=ns, tw=tw, bucket=bucket),
        out_shape=jax.ShapeDtypeStruct((b * t, N_HEADS * HEAD_DIM), F32),
        grid=(b, nq),
        in_specs=[row(q.shape[1]), row(gates.shape[1]), perb(kc), perb(vc), perb(kvs), perb(kvw_pad),
                  full(slope), full(frac_t), full(e_gate)],
        out_specs=row(N_HEADS * HEAD_DIM),
        scratch_shapes=[pltpu.VMEM((N_HEADS * tq, KV_W), F32)],
        compiler_params=_cp(("arbitrary", "arbitrary")),
        name="nsa_prompt",
    )(q, gates, kc, vc, kvs, kvw_pad, slope, frac_t, e_gate)


def _nsa_sample_body(pt_ref, q_ref, gt_ref, kc_ref, vc_ref, cache_ref, knew_ref, win_ref, wnew_ref, sl_ref, fr_ref,
                     eg_ref, o_ref, buf_ref, ks_ref, sem_ref, *, tq, ns, past):
    n_pages, page = pt_ref.shape[1], cache_ref.shape[2]
    tk = ns * SEL_BLOCK

    @pl.when(pl.program_id(0) == 0)
    def _():
        ks_ref[past:, :] = jnp.zeros((tk - past, ks_ref.shape[1]), F32)

    slot = _gather_pages(cache_ref, buf_ref, sem_ref, pt_ref)
    for j in range(n_pages):
        ks_ref[pl.ds(j * page, page), :] = buf_ref[slot, j].T
    ks_ref[past:past + tq, :] = knew_ref[0]
    wbuf = win_ref.shape[2]
    tw = wbuf + SEL_BLOCK
    w = jnp.concatenate([win_ref[0].T, wnew_ref[0], jnp.zeros((tw - wbuf - tq, 2 * KV_W), F32)], axis=0)
    slope = sl_ref[...]
    qfull, pq, o_c, sel_bias = _nsa_front(q_ref[0], kc_ref[0], vc_ref[0], past, slope, fr_ref[...], tq, ns)
    o_s = _nsa_selected(qfull, pq, sel_bias, ks_ref, slope, past // SEL_BLOCK, past // SEL_BLOCK)
    o_w = _nsa_window(qfull, pq, w[:, :KV_W], w[:, KV_W:], past - wbuf, slope)
    o_ref[0] = _nsa_combine(gt_ref[0], eg_ref, (o_c, o_s, o_w), tq)


def nsa_sample(page_table, q, gates, kc, vc, cache_t, kvs_new, win_t, kvw_new, slope, frac_t, e_gate):
    b, n_pages = page_table.shape
    _, width, page = cache_t.shape
    past = n_pages * page
    tq = q.shape[1]
    assert past % SEL_BLOCK + tq <= SEL_BLOCK
    ns = -(-(past + tq) // SEL_BLOCK)
    tk = ns * SEL_BLOCK
    full = lambda a: pl.BlockSpec(a.shape, lambda i, pt: (0,) * a.ndim)
    perb = lambda a: pl.BlockSpec((1,) + a.shape[1:], lambda i, pt: (i, 0, 0))
    return pl.pallas_call(
        functools.partial(_nsa_sample_body, tq=tq, ns=ns, past=past),
        out_shape=jax.ShapeDtypeStruct((b, tq, N_HEADS * HEAD_DIM), F32),
        grid_spec=pltpu.PrefetchScalarGridSpec(
            num_scalar_prefetch=1, grid=(b,),
            in_specs=[perb(q), perb(gates), perb(kc), perb(vc), pl.BlockSpec(memory_space=pl.ANY), perb(kvs_new),
                      perb(win_t), perb(kvw_new), full(slope), full(frac_t), full(e_gate)],
            out_specs=pl.BlockSpec((1, tq, N_HEADS * HEAD_DIM), lambda i, pt: (i, 0, 0)),
            scratch_shapes=[pltpu.VMEM((2, n_pages, width, page), F32), pltpu.VMEM((tk, width), F32),
                            pltpu.SemaphoreType.DMA((2,))]),
        compiler_params=_cp(("arbitrary",)),
        name="nsa_sample",
    )(page_table, q, gates, kc, vc, cache_t, kvs_new, win_t, kvw_new, slope, frac_t, e_gate)


def _pool_diff_body(ext_ref, d_ref, *, t, pos0):
    gw = ext_ref.shape[2] // len(POOL_SIZES)
    pos = pos0 + lax.broadcasted_iota(I32, (t, gw), 0)
    for gi, w in enumerate(POOL_SIZES):
        lanes = pl.ds(gi * gw, gw)
        cur = ext_ref[0, pl.ds(POOL_BUF, t), lanes]
        acc = cur
        for j in range(1, w):
            acc = acc + ext_ref[0, pl.ds(POOL_BUF - j, t), lanes]
        cnt = jnp.minimum(w, pos + 1).astype(F32)
        d_ref[0, :, lanes] = acc / cnt - cur


def pool_diff(ext, pos0):
    b, rows, width = ext.shape
    t = rows - POOL_BUF
    return pl.pallas_call(
        functools.partial(_pool_diff_body, t=t, pos0=pos0),
        out_shape=jax.ShapeDtypeStruct((b, t, width), F32),
        grid=(b,),
        in_specs=[pl.BlockSpec((1, rows, width), lambda i: (i, 0, 0))],
        out_specs=pl.BlockSpec((1, t, width), lambda i: (i, 0, 0)),
        compiler_params=_cp(("arbitrary",)),
        name="pool_diff",
    )(ext)


def _outproj_body(o_ref, d_ref, x_ref, gate_ref, sh_ref, sc_ref, g2_ref, wn_ref, wp_ref, wpool_ref, ps_ref,
                  x1_ref, h2_ref):
    gw = wpool_ref.shape[1]
    d = d_ref[...]
    yp = jnp.concatenate([_dot(d[:, g * gw:(g + 1) * gw].astype(BF16), wpool_ref[g]) for g in range(len(POOL_SIZES))],
                         axis=1) * ps_ref[...]
    mix = _dot(o_ref[...].astype(BF16), wn_ref[...]) + _dot(yp.astype(BF16), wp_ref[...])
    x1 = x_ref[...] + gate_ref[0] * mix
    x1_ref[...] = x1
    h2 = _rms_rows(x1) * g2_ref[...]
    h2_ref[...] = (h2 * (1.0 + sc_ref[0]) + sh_ref[0]).astype(BF16)


def outproj(o, d, x, gate, shift, scale, g2, wn, wp, wpool, ps, tm, tiles_per_mod):
    t, dm = x.shape
    full = lambda a: pl.BlockSpec(a.shape, lambda i: (0,) * a.ndim)
    row = lambda n: pl.BlockSpec((tm, n), lambda i: (i, 0))
    ms = lambda a: _mod_spec(a, tm, tiles_per_mod)
    return pl.pallas_call(
        _outproj_body,
        out_shape=(jax.ShapeDtypeStruct((t, dm), F32), jax.ShapeDtypeStruct((t, dm), BF16)),
        grid=(t // tm,),
        in_specs=[row(o.shape[1]), row(d.shape[1]), row(dm), ms(gate), ms(shift), ms(scale), full(g2),
                  full(wn), full(wp), full(wpool), full(ps)],
        out_specs=(row(dm), row(dm)),
        compiler_params=_cp(("arbitrary",)),
        name="outproj",
    )(o, d, x, gate, shift, scale, g2, wn, wp, wpool, ps)


_CAND_PIECES = ((0, 16), (1, 8), (2, 8), (3, 8), (4, 8), (5, 8), (6, 8), (7, 8))


def _top16_rows(s):
    kio = lax.broadcasted_iota(I32, s.shape, 0)
    nk = s.shape[0]
    rank = jnp.full(s.shape, float(PEER_TOPK), F32)
    vals, keys = [], []
    for it in range(PEER_TOPK):
        m = jnp.max(s, axis=0, keepdims=True)
        idx = jnp.min(jnp.where(s == m, kio, nk), axis=0, keepdims=True)
        hit = kio == idx
        rank = jnp.where(hit, float(it), rank)
        s = jnp.where(hit, -jnp.inf, s)
        vals.append(m)
        keys.append(idx)
    return vals, keys, rank


def _peer_route_body(h_ref, wq_ref, sk_ref, rk_ref, lim_ref, e1_ref, e2_ref, q_scr):
    hd = pl.program_id(1)

    qd = sk_ref.shape[3]

    @pl.when(hd == 0)
    def _():
        q = _dot(h_ref[...], wq_ref[...]).astype(BF16)
        for i in range(PEER_HEADS):
            q_scr[i] = q[:, i * 2 * qd:(i + 1) * 2 * qd]

    qh = q_scr[hd]
    s1 = _nt(sk_ref[0, 0], qh[:, :qd])
    s2 = _nt(sk_ref[1, 0], qh[:, qd:])
    v1, k1, _ = _top16_rows(s1)
    v2, _, rank2 = _top16_rows(s2)
    n = s1.shape[1]
    v2a = jnp.concatenate(v2, axis=0)
    top = v1[0] + v2[0]

    bio = {nb: lax.broadcasted_iota(I32, (nb, n), 0) for nb in (8, PEER_TOPK)}
    v2p = {8: jnp.concatenate(v2[:8], axis=0), PEER_TOPK: v2a}
    cands, flats = [], []
    for a, nb in _CAND_PIECES:
        c = v1[a] + v2p[nb]
        ok = (bio[nb] + 1) * (a + 1) <= PEER_TOPK
        cands.append(jnp.where(ok, c, -jnp.inf))
        flats.append(a * PEER_TOPK + bio[nb])
    v1b = jnp.concatenate(v1[8:], axis=0)
    cands.append(v1b + v2[0])
    flats.append((bio[8] + 8) * PEER_TOPK)
    cand = jnp.concatenate(cands, axis=0)
    flat = jnp.concatenate(flats, axis=0)
    big = PEER_TOPK * PEER_TOPK
    taken = jnp.zeros(cand.shape, jnp.bool_)
    z = jnp.zeros((1, n), F32)
    for _ in range(PEER_TOPK):
        m = jnp.max(cand, axis=0, keepdims=True)
        f = jnp.min(jnp.where(cand == m, flat, big), axis=0, keepdims=True)
        hit = flat == f
        taken = taken | hit
        cand = jnp.where(hit, -jnp.inf, cand)
        z = z + jnp.exp(m - top)
    takenf = taken.astype(F32)

    lim = jnp.zeros(s1.shape, F32)
    kio = lax.broadcasted_iota(I32, s1.shape, 0)
    off = 0
    for a, nb in _CAND_PIECES:
        cnt = jnp.sum(takenf[off:off + nb], axis=0, keepdims=True)
        lim = jnp.where(kio == k1[a], cnt, lim)
        off += nb
    for i in range(8):
        lim = jnp.where(kio == k1[8 + i], takenf[off + i:off + i + 1], lim)

    rk_ref[0] = rank2.astype(BF16)
    lim_ref[0] = lim
    e1_ref[0] = jnp.exp(s1 - v1[0]) / z
    e2_ref[0] = jnp.exp(s2 - v2[0]).astype(BF16)


def peer_route(h2, wq, sk, tt):
    t, d = h2.shape
    nk = sk.shape[2]
    out = jax.ShapeDtypeStruct((PEER_HEADS, nk, t), F32)
    outb = jax.ShapeDtypeStruct((PEER_HEADS, nk, t), BF16)
    ospec = pl.BlockSpec((1, nk, tt), lambda i, h: (h, 0, i))
    return pl.pallas_call(
        _peer_route_body,
        out_shape=(outb, out, out, outb),
        grid=(t // tt, PEER_HEADS),
        in_specs=[pl.BlockSpec((tt, d), lambda i, h: (i, 0)),
                  pl.BlockSpec(wq.shape, lambda i, h: (0, 0)),
                  pl.BlockSpec((2, 1) + sk.shape[2:], lambda i, h: (0, h, 0, 0))],
        out_specs=(ospec, ospec, ospec, ospec),
        scratch_shapes=[pltpu.VMEM((PEER_HEADS, tt, wq.shape[1] // PEER_HEADS), BF16)],
        compiler_params=_cp(("arbitrary", "arbitrary")),
        name="peer_route",
    )(h2, wq, sk)


def _peer_dense_body(h_ref, u_ref, v_ref, rk_ref, lim_ref, e1_ref, e2_ref, x1_ref, gate_ref, y_ref, w_scr, lb_scr,
                     eb_scr):
    j = pl.program_id(1)
    nk = rk_ref.shape[1]
    et, tt = u_ref.shape[0], h_ref.shape[0]
    rows = lb_scr.shape[2]

    @pl.when(j == 0)
    def _():
        y_ref[...] = jnp.zeros_like(y_ref)

    zero = jnp.zeros((), BF16)
    for rr in range(et // nk):
        r = j * (et // nk) + rr
        for hd in range(PEER_HEADS):
            lb_scr[rr, hd] = jnp.broadcast_to(lim_ref[hd, pl.ds(r, 1), :], (rows, tt)).astype(BF16)
            eb_scr[rr, hd] = jnp.broadcast_to(e1_ref[hd, pl.ds(r, 1), :], (rows, tt)).astype(BF16)
    for rr in range(et // nk):
        for ch in range(nk // rows):
            sl = pl.ds(ch * rows, rows)
            w = jnp.zeros((rows, tt), BF16)
            for hd in range(PEER_HEADS):
                w = w + jnp.where(rk_ref[hd, sl, :] < lb_scr[rr, hd], e2_ref[hd, sl, :], zero) * eb_scr[rr, hd]
            w_scr[pl.ds(rr * nk + ch * rows, rows), :] = w
    a = _nt(u_ref[...], h_ref[...])
    act = a * (lax.erf(a * np.float32(1.0 / np.sqrt(2.0))) + 1.0) * 0.5
    wt = (act * w_scr[...].astype(F32)).astype(BF16)
    y_ref[...] += _tn(wt, v_ref[...])

    @pl.when(j == pl.num_programs(1) - 1)
    def _():
        y_ref[...] = x1_ref[...] + gate_ref[0] * y_ref[...]


def peer_dense(h2, u, v, rk, lim, e1, e2, x1, gate, tt, et, tiles_per_mod):
    t, d = h2.shape
    ne = u.shape[0]
    nk = rk.shape[1]
    tok = pl.BlockSpec((tt, d), lambda i, j: (i, 0))
    exp = pl.BlockSpec((et, d), lambda i, j: (j, 0))
    rt = pl.BlockSpec((PEER_HEADS, nk, tt), lambda i, j: (0, 0, i))
    if gate.shape[1] == 1:
        gspec = pl.BlockSpec((1, 1, d), lambda i, j: (i // tiles_per_mod, 0, 0))
    else:
        gspec = pl.BlockSpec((1, tt, d), lambda i, j: (i, 0, 0))
    return pl.pallas_call(
        _peer_dense_body,
        out_shape=jax.ShapeDtypeStruct((t, d), F32),
        grid=(t // tt, ne // et),
        in_specs=[tok, exp, exp, rt, rt, rt, rt, tok, gspec],
        out_specs=tok,
        scratch_shapes=[pltpu.VMEM((et, tt), BF16), pltpu.VMEM((et // nk, PEER_HEADS, BF16_ROWS, tt), BF16),
                        pltpu.VMEM((et // nk, PEER_HEADS, BF16_ROWS, tt), BF16)],
        compiler_params=_cp(("arbitrary", "arbitrary")),
        name="peer_dense",
    )(h2, u, v, rk, lim, e1, e2, x1, gate)


def _slope_lanes(tq):
    h = np.arange(1, N_HEADS + 1, dtype=np.float32)
    s = (2.0 ** (-8.0 * h / N_HEADS)).reshape(N_KV, GROUP)
    return jnp.asarray(np.repeat(s.T.reshape(-1), tq)[None, :], F32)


def _frac_t(nc, ns, nc_pad, ns_pad):
    pos = np.arange(nc)[:, None] * CMP_STRIDE + np.arange(CMP_BLOCK)[None, :]
    f = ((pos // SEL_BLOCK)[:, :, None] == np.arange(ns)[None, None, :]).mean(axis=1)
    out = np.zeros((ns_pad, nc_pad), np.float32)
    out[:ns, :nc] = f.T
    return jnp.asarray(out)


def _gate_expand(width):
    e = np.zeros((3, width, N_HEADS * HEAD_DIM), np.float32)
    for g in range(N_KV):
        for r in range(GROUP):
            for j in range(3):
                c0 = r * KV_W + g * HEAD_DIM
                e[j, (g * GROUP + r) * 3 + j, c0:c0 + HEAD_DIM] = 1.0
    return jnp.asarray(e, BF16)


def _rgd(a, axis):
    shp = a.shape
    a = a.reshape(shp[:axis] + (N_KV, GROUP, HEAD_DIM) + shp[axis + 1:])
    a = jnp.swapaxes(a, axis, axis + 1)
    return a.reshape(shp)


def kernel(x_prompt, x_sample, cache_kv_cmp, cache_kv_slc, cache_kv_win, state_pool, page_table, c_prompt, c_sample,
           w_ada, b_ada, g_norm1, g_norm2, w_in, w_out, g_qnorm, g_knorm, cmp_pe, w_cmp, w_pool, pool_scale,
           peer_wq, peer_subkeys, peer_u, peer_v):
    bp, tp, dm = x_prompt.shape
    bs, ts, _ = x_sample.shape
    n_pages = page_table.shape[1]
    page = cache_kv_cmp.shape[1]
    past = n_pages * page
    nsa_w = N_HEADS * HEAD_DIM
    kv3 = 3 * 2 * KV_W
    ngl = 3 * N_HEADS
    gpad = 128
    tm, tq, sel_bucket, tt, et = 256, 32, 4, 512, 512

    wq = _rgd(w_in[:, :nsa_w], 1).astype(BF16)
    wkv = w_in[:, nsa_w:nsa_w + kv3].astype(BF16)
    wg = jnp.pad(w_in[:, nsa_w + kv3:nsa_w + kv3 + ngl], ((0, 0), (0, gpad - ngl))).astype(BF16)
    wu = w_in[:, nsa_w + kv3 + ngl:].astype(BF16)
    gq = jnp.tile(g_qnorm, N_HEADS)[None, :]
    gk = jnp.tile(g_knorm, (1, N_KV))
    g1 = g_norm1[None, :]
    g2 = g_norm2[None, :]
    wn = _rgd(w_out[:nsa_w], 0).astype(BF16)
    wp = w_out[nsa_w:].astype(BF16)
    wpool = w_pool.astype(BF16)
    ps = pool_scale[None, :]
    wc = jnp.einsum('cglde,gh->clgdhe', w_cmp, jnp.eye(N_KV, dtype=F32)).reshape(2, CMP_BLOCK, KV_W, KV_W).astype(BF16)
    pe = jnp.transpose(cmp_pe, (1, 0, 2, 3)).reshape(CMP_BLOCK, 2 * KV_W)
    e_gate = _gate_expand(gpad)
    pwq = peer_wq.astype(BF16)
    psk = peer_subkeys.astype(BF16)
    pu = peer_u.astype(BF16)
    pv = peer_v.astype(BF16)

    mod = adaln(jnp.concatenate([c_prompt, c_sample], axis=0), w_ada, b_ada).reshape(bp + bs, 6, dm)
    mod_p = [mod[:bp, k][:, None, :] for k in range(6)]
    mod_s = [jnp.repeat(mod[bp:, k], ts, axis=0).reshape(bs * ts // tm, tm, dm) for k in range(6)]

    xp = x_prompt.reshape(bp * tp, dm)
    q, kvc, kvs, kvw, gates, u = inproj(xp, mod_p[0], mod_p[1], g1, wq, wkv, wg, wu, gq, gk, tm, tp // tm)
    kc, vc = compress_prompt(kvc.reshape(bp, tp // CMP_STRIDE, CMP_STRIDE * 2 * KV_W), pe, wc, gk)
    nc = (tp - CMP_BLOCK) // CMP_STRIDE + 1
    kvw3 = kvw.reshape(bp, tp, 2 * KV_W)
    o_nsa = nsa_prompt(q, gates, kc, vc, kvs.reshape(bp, tp, 2 * KV_W), jnp.pad(kvw3, ((0, 0), (WINDOW, 0), (0, 0))),
                       _slope_lanes(tq), _frac_t(nc, tp // SEL_BLOCK, tp // CMP_STRIDE, 40), e_gate, tq, sel_bucket)
    u3 = u.reshape(bp, tp, -1)
    d_pool = pool_diff(jnp.pad(u3, ((0, 0), (POOL_BUF, 0), (0, 0))), 0).reshape(bp * tp, -1)
    x1, h2 = outproj(o_nsa, d_pool, xp, mod_p[2], mod_p[3], mod_p[4], g2, wn, wp, wpool, ps, tm, tp // tm)
    routes = peer_route(h2, pwq, psk, 256)
    y_prompt = peer_dense(h2, pu, pv, *routes, x1, mod_p[5], tt, et, tp // tt).reshape(bp, tp, dm)
    shp_p = (bp, tp, 2, N_KV, HEAD_DIM)
    win_p = kvw3[:, -min(WINDOW, tp):].reshape(bp, -1, 2, N_KV, HEAD_DIM)
    pool_p = u3[:, -POOL_BUF:]

    xs = x_sample.reshape(bs * ts, dm)
    q, kvc_s, kvs_s, kvw_s, gates, u = inproj(xs, mod_s[0], mod_s[1], g1, wq, wkv, wg, wu, gq, gk, tm, 1)
    feat_major = lambda c: jnp.transpose(c, (0, 2, 3, 4, 1)).reshape(c.shape[0], 2 * KV_W, c.shape[1])
    kc, vc = compress_sample(page_table, feat_major(cache_kv_cmp), pe, wc, gk)
    nc = (past + ts - CMP_BLOCK) // CMP_STRIDE + 1
    ns = -(-(past + ts) // SEL_BLOCK)
    kvw_s3 = kvw_s.reshape(bs, ts, 2 * KV_W)
    o_nsa = nsa_sample(page_table, q.reshape(bs, ts, -1), gates.reshape(bs, ts, -1), kc, vc,
                       feat_major(cache_kv_slc), kvs_s.reshape(bs, ts, 2 * KV_W), feat_major(cache_kv_win), kvw_s3,
                       _slope_lanes(ts), _frac_t(nc, ns, past // CMP_STRIDE, 40), e_gate).reshape(bs * ts, -1)
    ext = jnp.concatenate([state_pool, u.reshape(bs, ts, -1)], axis=1)
    d_pool = pool_diff(ext, past).reshape(bs * ts, -1)
    x1, h2 = outproj(o_nsa, d_pool, xs, mod_s[2], mod_s[3], mod_s[4], g2, wn, wp, wpool, ps, tm, 1)
    routes = peer_route(h2, pwq, psk, 256)
    gate5 = mod_s[5].reshape(bs * ts // tt, tt, dm)
    y_sample = peer_dense(h2, pu, pv, *routes, x1, gate5, tt, et, 1).reshape(bs, ts, dm)
    shp_s = (bs, ts, 2, N_KV, HEAD_DIM)
    wbuf = cache_kv_win.shape[1]
    win_s = jnp.concatenate([cache_kv_win, kvw_s.reshape(shp_s)], axis=1)[:, -wbuf:]
    pool_s = ext[:, -POOL_BUF:]

    return (y_prompt, y_sample, kvc.reshape(shp_p), kvs.reshape(shp_p), win_p, pool_p,
            kvc_s.reshape(shp_s), kvs_s.reshape(shp_s), win_s, pool_s)
```

```python
import functools

import numpy as np
import jax
import jax.numpy as jnp
from jax import lax
from jax.experimental import pallas as pl
from jax.experimental.pallas import tpu as pltpu

F32, BF16, I32 = jnp.float32, jnp.bfloat16, jnp.int32

HEAD_DIM = 64
N_KV = 4
GROUP = 4
N_HEADS = N_KV * GROUP
CMP_BLOCK = 32
CMP_STRIDE = 16
SEL_BLOCK = 64
SEL_TOP = 16
WINDOW = 512
POOL_SIZES = (2, 4, 8, 16)
POOL_BUF = max(POOL_SIZES) - 1
PEER_HEADS = 8
N_KEYS = 128
PEER_TOPK = 16
EPS = 1e-6
NEG = -1e30
FORCE_BONUS = 1e4
KV_W = N_KV * HEAD_DIM
VMEM_LIMIT = 56 * 1024 * 1024


def _cp(sem, vmem=VMEM_LIMIT):
    return pltpu.CompilerParams(dimension_semantics=sem, vmem_limit_bytes=vmem)


def _dot(a, b):
    return jnp.dot(a, b, preferred_element_type=F32)


def _nt(a, b):
    return lax.dot_general(a, b, (((1,), (1,)), ((), ())), preferred_element_type=F32)


def _tn(a, b):
    return lax.dot_general(a, b, (((0,), (0,)), ((), ())), preferred_element_type=F32)


def _split_dot(x, w):
    hi = x.astype(BF16)
    r1 = x - hi.astype(F32)
    mid = r1.astype(BF16)
    lo = (r1 - mid.astype(F32)).astype(BF16)
    return _dot(hi, w) + _dot(mid, w) + _dot(lo, w)


def _ones64():
    r = lax.broadcasted_iota(I32, (KV_W, KV_W), 0) // HEAD_DIM
    c = lax.broadcasted_iota(I32, (KV_W, KV_W), 1) // HEAD_DIM
    return (r == c).astype(BF16)


def _rms64(x, ones):
    ms = _split_dot(x * x, ones) * (1.0 / HEAD_DIM)
    return x * lax.rsqrt(ms + EPS)


def _rms_rows(x):
    return x * lax.rsqrt(jnp.mean(x * x, axis=-1, keepdims=True) + EPS)


def _adaln_body(c_ref, w_ref, b_ref, o_ref):
    c = c_ref[...]
    s = (c * jax.nn.sigmoid(c)).astype(BF16)
    o_ref[...] = _dot(s, w_ref[...].astype(BF16)) + b_ref[...]


def adaln(c_all, w_ada, b_ada, tn=1024):
    m, d = c_all.shape
    n = w_ada.shape[1]
    return pl.pallas_call(
        _adaln_body,
        out_shape=jax.ShapeDtypeStruct((m, n), F32),
        grid=(n // tn,),
        in_specs=[pl.BlockSpec((m, d), lambda j: (0, 0)),
                  pl.BlockSpec((d, tn), lambda j: (0, j)),
                  pl.BlockSpec((1, tn), lambda j: (0, j))],
        out_specs=pl.BlockSpec((m, tn), lambda j: (0, j)),
        compiler_params=_cp(("arbitrary",)),
        name="adaln",
    )(c_all, w_ada, b_ada.reshape(1, n))


def _inproj_body(x_ref, sh_ref, sc_ref, g1_ref, wq_ref, wkv_ref, wg_ref, wu_ref, gq_ref, gk_ref,
                 q_ref, kvc_ref, kvs_ref, kvw_ref, gt_ref, u_ref):
    h = _rms_rows(x_ref[...]) * g1_ref[...]
    h = h * (1.0 + sc_ref[0]) + sh_ref[0]
    hb = h.astype(BF16)
    ones = _ones64()
    zq = _dot(hb, wq_ref[...])
    for r in range(GROUP):
        sl = slice(r * KV_W, (r + 1) * KV_W)
        q_ref[:, sl] = _rms64(zq[:, sl], ones) * gq_ref[:, sl] * (HEAD_DIM ** -0.5)
    zkv = _dot(hb, wkv_ref[...])
    kvc_ref[...] = zkv[:, 0:2 * KV_W]
    kvs_ref[:, 0:KV_W] = _rms64(zkv[:, 2 * KV_W:3 * KV_W], ones) * gk_ref[1:2, :]
    kvs_ref[:, KV_W:] = zkv[:, 3 * KV_W:4 * KV_W]
    kvw_ref[:, 0:KV_W] = _rms64(zkv[:, 4 * KV_W:5 * KV_W], ones) * gk_ref[2:3, :]
    kvw_ref[:, KV_W:] = zkv[:, 5 * KV_W:6 * KV_W]
    gt_ref[...] = jax.nn.sigmoid(_dot(hb, wg_ref[...]))
    u_ref[...] = _dot(hb, wu_ref[...])


def _mod_spec(arr, tm, rows_per_mod):
    d = arr.shape[-1]
    if arr.shape[1] == 1:
        return pl.BlockSpec((1, 1, d), lambda i: (i // rows_per_mod, 0, 0))
    return pl.BlockSpec((1, tm, d), lambda i: (i, 0, 0))


def inproj(x, shift, scale, g1, wq, wkv, wg, wu, gq, gk, tm, tiles_per_mod):
    t, d = x.shape
    nq, nkv, ng, nu = wq.shape[1], wkv.shape[1], wg.shape[1], wu.shape[1]
    full = lambda a: pl.BlockSpec(a.shape, lambda i: (0,) * a.ndim)
    row = lambda n: pl.BlockSpec((tm, n), lambda i: (i, 0))
    return pl.pallas_call(
        _inproj_body,
        out_shape=(jax.ShapeDtypeStruct((t, nq), F32), jax.ShapeDtypeStruct((t, 2 * KV_W), F32),
                   jax.ShapeDtypeStruct((t, 2 * KV_W), F32), jax.ShapeDtypeStruct((t, 2 * KV_W), F32),
                   jax.ShapeDtypeStruct((t, ng), F32), jax.ShapeDtypeStruct((t, nu), F32)),
        grid=(t // tm,),
        in_specs=[row(d), _mod_spec(shift, tm, tiles_per_mod), _mod_spec(scale, tm, tiles_per_mod), full(g1),
                  full(wq), full(wkv), full(wg), full(wu), full(gq), full(gk)],
        out_specs=(row(nq), row(2 * KV_W), row(2 * KV_W), row(2 * KV_W), row(ng), row(nu)),
        compiler_params=_cp(("arbitrary",)),
        name="inproj",
    )(x, shift, scale, g1, wq, wkv, wg, wu, gq, gk)


def _compress(get_x, nchunk, pe_ref, w_ref, gk_ref, kc_ref, vc_ref):
    half = CMP_BLOCK // 2
    acc = [jnp.zeros((nchunk, KV_W), F32) for _ in range(4)]
    for l in range(half):
        xl = get_x(l)
        a = (xl + pe_ref[l:l + 1, :]).astype(BF16)
        b = (xl + pe_ref[half + l:half + l + 1, :]).astype(BF16)
        acc[0] += _dot(a[:, :KV_W], w_ref[0, l])
        acc[1] += _dot(a[:, KV_W:], w_ref[1, l])
        acc[2] += _dot(b[:, :KV_W], w_ref[0, half + l])
        acc[3] += _dot(b[:, KV_W:], w_ref[1, half + l])
    rio = lax.broadcasted_iota(I32, (nchunk, KV_W), 0)
    nxt = lambda v: jnp.where(rio < nchunk - 1, pltpu.roll(v, nchunk - 1, 0), 0.0)
    ck = acc[0] + nxt(acc[2])
    cv = acc[1] + nxt(acc[3])
    kc_ref[...] = _rms64(ck, _ones64()) * gk_ref[0:1, :]
    vc_ref[...] = cv


def _compress_prompt_body(x_ref, pe_ref, w_ref, gk_ref, kc_ref, vc_ref):
    row_w = 2 * KV_W
    _compress(lambda l: x_ref[0, :, l * row_w:(l + 1) * row_w], x_ref.shape[1], pe_ref, w_ref, gk_ref,
              kc_ref.at[0], vc_ref.at[0])


def compress_prompt(kv_chunks, pe, wc, gk):
    b, nchunk, width = kv_chunks.shape
    full = lambda a: pl.BlockSpec(a.shape, lambda i: (0,) * a.ndim)
    out = jax.ShapeDtypeStruct((b, nchunk, KV_W), F32)
    ospec = pl.BlockSpec((1, nchunk, KV_W), lambda i: (i, 0, 0))
    return pl.pallas_call(
        _compress_prompt_body,
        out_shape=(out, out),
        grid=(b,),
        in_specs=[pl.BlockSpec((1, nchunk, width), lambda i: (i, 0, 0)), full(pe), full(wc), full(gk)],
        out_specs=(ospec, ospec),
        compiler_params=_cp(("arbitrary",)),
        name="compress_prompt",
    )(kv_chunks, pe, wc, gk)


def _page_copy(cache_ref, buf_ref, sem_ref, pt_ref, seq, slot, j):
    return pltpu.make_async_copy(cache_ref.at[pt_ref[seq, j]], buf_ref.at[slot, j], sem_ref.at[slot])


def _gather_pages(cache_ref, buf_ref, sem_ref, pt_ref):
    b = pl.program_id(0)
    nb = pl.num_programs(0)
    n_pages = pt_ref.shape[1]
    slot = b % 2

    def start(seq, s):
        for j in range(n_pages):
            _page_copy(cache_ref, buf_ref, sem_ref, pt_ref, seq, s, j).start()

    @pl.when(b == 0)
    def _():
        start(b, slot)

    @pl.when(b + 1 < nb)
    def _():
        start(b + 1, 1 - slot)

    for j in range(n_pages):
        _page_copy(cache_ref, buf_ref, sem_ref, pt_ref, b, slot, j).wait()
    return slot


LANES = 128
BF16_ROWS = 16


def _compress_sample_body(pt_ref, cache_ref, pe_ref, w_ref, gk_ref, kc_ref, vc_ref, buf_ref, xs_ref, sem_ref):
    n_pages, page = pt_ref.shape[1], cache_ref.shape[2]
    slot = _gather_pages(cache_ref, buf_ref, sem_ref, pt_ref)
    nlb = xs_ref.shape[0]
    for j in range(n_pages):
        for k in range(nlb):
            xs_ref[k, pl.ds(j * page, page), :] = buf_ref[slot, j, pl.ds(k * LANES, LANES), :].T
    nchunk = n_pages * page // CMP_STRIDE
    get_x = lambda l: jnp.concatenate([xs_ref[k, pl.ds(l, nchunk, stride=CMP_STRIDE), :] for k in range(nlb)], axis=1)
    _compress(get_x, nchunk, pe_ref, w_ref, gk_ref, kc_ref.at[0], vc_ref.at[0])


def compress_sample(page_table, cache_t, pe, wc, gk):
    b, n_pages = page_table.shape
    _, width, page = cache_t.shape
    nchunk = n_pages * page // CMP_STRIDE
    full = lambda a: pl.BlockSpec(a.shape, lambda i, pt: (0,) * a.ndim)
    out = jax.ShapeDtypeStruct((b, nchunk, KV_W), F32)
    ospec = pl.BlockSpec((1, nchunk, KV_W), lambda i, pt: (i, 0, 0))
    return pl.pallas_call(
        _compress_sample_body,
        out_shape=(out, out),
        grid_spec=pltpu.PrefetchScalarGridSpec(
            num_scalar_prefetch=1, grid=(b,),
            in_specs=[pl.BlockSpec(memory_space=pl.ANY), full(pe), full(wc), full(gk)],
            out_specs=(ospec, ospec),
            scratch_shapes=[pltpu.VMEM((2, n_pages, width, page), F32),
                            pltpu.VMEM((width // LANES, n_pages * page, LANES), F32),
                            pltpu.SemaphoreType.DMA((2,))]),
        compiler_params=_cp(("arbitrary",)),
        name="compress_sample",
    )(page_table, cache_t, pe, wc, gk)


def _softmax_cols(s, valid):
    s = jnp.where(valid, s, NEG)
    m = jnp.max(s, axis=0, keepdims=True)
    e = jnp.where(valid, jnp.exp(s - m), 0.0)
    return e / jnp.maximum(jnp.sum(e, axis=0, keepdims=True), 1e-30)


def _nsa_front(q, kc, vc, pq0, slope, frac_t, tq, ns):
    cols = N_HEADS * tq
    lane = lax.broadcasted_iota(I32, (1, cols), 1)
    pq = (pq0 + lane % tq).astype(F32)
    lg = lax.broadcasted_iota(I32, (tq, KV_W), 1) // HEAD_DIM
    qfull = jnp.concatenate([jnp.where(lg == g, q[:, r * KV_W:(r + 1) * KV_W], 0.0)
                             for r in range(GROUP) for g in range(N_KV)], axis=0).astype(BF16)

    nc = kc.shape[0]
    pos_c = (lax.broadcasted_iota(I32, (nc, cols), 0) * CMP_STRIDE + (CMP_BLOCK - 1)).astype(F32)
    dist_c = pq - pos_c
    p_c = _softmax_cols(_nt(kc.astype(BF16), qfull) - slope * dist_c, dist_c >= 0).astype(BF16)
    o_c = _tn(p_c, vc.astype(BF16))

    imp = _dot(frac_t.astype(BF16), p_c)
    imp = imp + pltpu.roll(imp, 4 * tq, 1) + pltpu.roll(imp, 8 * tq, 1) + pltpu.roll(imp, 12 * tq, 1)
    nsp = imp.shape[0]
    jio = lax.broadcasted_iota(I32, (nsp, cols), 0)
    blk_q = (pq0 + lane % tq) // SEL_BLOCK
    forced = (jio == 0) | (jio == blk_q) | (jio == blk_q - 1)
    valid_blk = (jio * SEL_BLOCK).astype(F32) <= pq
    imp = jnp.where(valid_blk, imp + FORCE_BONUS * forced.astype(F32), NEG)
    rank = jnp.zeros((nsp, cols), I32)
    for i in range(ns):
        row = imp[i:i + 1, :]
        rank = rank + ((row > imp) | ((row == imp) & (i < jio))).astype(I32)
    sel_bias = jnp.where(rank < min(SEL_TOP, ns), 0.0, NEG)
    return qfull, pq, o_c, sel_bias


def _as_column(row):
    return jnp.transpose(jnp.broadcast_to(row, (LANES, row.shape[1])))[:, 0:1]


def _nsa_selected(qfull, pq, sel_bias, ks_ref, slope, nblk, blk_q):
    cols = qfull.shape[0]
    nsp = sel_bias.shape[0]
    tk = nblk * SEL_BLOCK
    jio = lax.broadcasted_iota(I32, (nsp, cols), 0)
    bias = jnp.where(jio < blk_q, sel_bias, NEG) + slope * (jio * SEL_BLOCK).astype(F32)
    inblk = slope * lax.broadcasted_iota(I32, (SEL_BLOCK, cols), 0).astype(F32)
    s_m = _nt(ks_ref[0:tk, 0:KV_W].astype(BF16), qfull).reshape(nblk, SEL_BLOCK, cols)
    s_m = s_m + inblk[None, :, :] + bias[:nblk][:, None, :]

    d0 = blk_q * SEL_BLOCK
    kd = ks_ref[pl.ds(d0 if isinstance(d0, int) else pl.multiple_of(d0, SEL_BLOCK), SEL_BLOCK), :]
    kpos_d = (blk_q * SEL_BLOCK + lax.broadcasted_iota(I32, (SEL_BLOCK, cols), 0)).astype(F32)
    valid_d = kpos_d <= pq
    s_d = jnp.where(valid_d, _nt(kd[:, :KV_W].astype(BF16), qfull) + slope * kpos_d, NEG)

    m = jnp.maximum(jnp.max(jnp.max(s_m, axis=0), axis=0, keepdims=True), jnp.max(s_d, axis=0, keepdims=True))
    e_m = jnp.exp(s_m - m[None, :, :])
    e_d = jnp.where(valid_d, jnp.exp(s_d - m), 0.0)
    l = jnp.sum(jnp.sum(e_m, axis=0), axis=0, keepdims=True) + jnp.sum(e_d, axis=0, keepdims=True)
    o = _tn(e_m.astype(BF16).reshape(tk, cols), ks_ref[0:tk, KV_W:].astype(BF16)) + _tn(e_d.astype(BF16),
                                                                                       kd[:, KV_W:].astype(BF16))
    return o * _as_column(1.0 / jnp.maximum(l, 1e-30))


def _nsa_window(qfull, pq, kw, vw, pw0, slope):
    tw, cols = kw.shape[0], qfull.shape[0]
    pos_w = (pw0 + lax.broadcasted_iota(I32, (tw, cols), 0)).astype(F32)
    dist_w = pq - pos_w
    valid_w = (dist_w >= 0) & (dist_w <= WINDOW) & (pos_w >= 0)
    p_w = _softmax_cols(_nt(kw.astype(BF16), qfull) - slope * dist_w, valid_w).astype(BF16)
    return _tn(p_w, vw.astype(BF16))


def _nsa_window_steady(qfull, kw, vw, slope, tq):
    cols = qfull.shape[0]
    nb = WINDOW // tq
    qq = lax.broadcasted_iota(I32, (1, cols), 1) % tq
    rio = lax.broadcasted_iota(I32, (tq, cols), 0)
    s = _nt(kw.astype(BF16), qfull).reshape(nb + 1, tq, cols) + (slope * rio.astype(F32))[None, :, :]
    s = s + (slope * (lax.broadcasted_iota(I32, (nb + 1, cols), 0) * tq).astype(F32))[:, None, :]
    ok_f, ok_l = rio >= qq, rio <= qq
    s_f, s_m, s_l = jnp.where(ok_f, s[0], NEG), s[1:nb], jnp.where(ok_l, s[nb], NEG)
    m = jnp.maximum(jnp.max(jnp.max(s_m, axis=0), axis=0, keepdims=True),
                    jnp.max(jnp.maximum(s_f, s_l), axis=0, keepdims=True))
    e_f = jnp.where(ok_f, jnp.exp(s_f - m), 0.0)
    e_l = jnp.where(ok_l, jnp.exp(s_l - m), 0.0)
    e_m = jnp.exp(s_m - m[None, :, :])
    l = jnp.sum(jnp.sum(e_m, axis=0), axis=0, keepdims=True) + jnp.sum(e_f + e_l, axis=0, keepdims=True)
    v = vw.astype(BF16)
    o = (_tn(e_m.astype(BF16).reshape((nb - 1) * tq, cols), v[tq:nb * tq]) + _tn(e_f.astype(BF16), v[0:tq])
         + _tn(e_l.astype(BF16), v[nb * tq:]))
    return o * _as_column(1.0 / jnp.maximum(l, 1e-30))


def _nsa_combine(gates, e_gate, outs, tq):
    cols = N_HEADS * tq
    row_g = (lax.broadcasted_iota(I32, (cols, KV_W), 0) // tq) % N_KV
    own = row_g == lax.broadcasted_iota(I32, (cols, KV_W), 1) // HEAD_DIM

    def fold(o):
        o = jnp.where(own, o, 0.0)
        return jnp.concatenate(
            [sum(o[(r * N_KV + g) * tq:(r * N_KV + g + 1) * tq, :] for g in range(N_KV)) for r in range(GROUP)], axis=1)

    out = jnp.zeros((tq, N_HEADS * HEAD_DIM), F32)
    for j, o in enumerate(outs):
        out = out + _split_dot(gates, e_gate[j]) * fold(o)
    return out


def _nsa_prompt_body(q_ref, gt_ref, kc_ref, vc_ref, ks_ref, kw_ref, sl_ref, fr_ref, eg_ref, o_ref, os_ref, ow_ref,
                     *, tq, ns, tw, bucket):
    q0 = pl.program_id(1) * tq
    slope = sl_ref[...]
    qfull, pq, o_c, sel_bias = _nsa_front(q_ref[...], kc_ref[0], vc_ref[0], q0, slope, fr_ref[...], tq, ns)
    blk_q = q0 // SEL_BLOCK
    for k in range(ns // bucket):
        @pl.when(blk_q // bucket == k)
        def _():
            os_ref[...] = _nsa_selected(qfull, pq, sel_bias, ks_ref.at[0], slope, (k + 1) * bucket, blk_q)
    w = kw_ref[0, pl.ds(pl.multiple_of(q0, 8), tw), :]

    @pl.when(q0 >= WINDOW)
    def _():
        ow_ref[...] = _nsa_window_steady(qfull, w[:, :KV_W], w[:, KV_W:], slope, tq)

    @pl.when(q0 < WINDOW)
    def _():
        ow_ref[...] = _nsa_window(qfull, pq, w[:, :KV_W], w[:, KV_W:], q0 - WINDOW, slope)

    o_ref[...] = _nsa_combine(gt_ref[...], eg_ref, (o_c, os_ref[...], ow_ref[...]), tq)


def nsa_prompt(q, gates, kc, vc, kvs, kvw_pad, slope, frac_t, e_gate, tq, bucket):
    b, t, _ = kvs.shape
    ns = t // SEL_BLOCK
    assert SEL_BLOCK % tq == 0 and ns % bucket == 0 and t % SEL_BLOCK == 0
    assert WINDOW % tq == 0 and tq % 8 == 0
    tw = WINDOW + tq
    nq = t // tq
    full = lambda a: pl.BlockSpec(a.shape, lambda i, j: (0,) * a.ndim)
    perb = lambda a: pl.BlockSpec((1,) + a.shape[1:], lambda i, j: (i, 0, 0))
    row = lambda n: pl.BlockSpec((tq, n), lambda i, j: (i * nq + j, 0))
    return pl.pallas_call(
        functools.partial(_nsa_prompt_body, tq=tq, ns=ns, tw=tw, bucket=bucket),
        out_shape=jax.ShapeDtypeStruct((b * t, N_HEADS * HEAD_DIM), F32),
        grid=(b, nq),
        in_specs=[row(q.shape[1]), row(gates.shape[1]), perb(kc), perb(vc), perb(kvs), perb(kvw_pad),
                  full(slope), full(frac_t), full(e_gate)],
        out_specs=row(N_HEADS * HEAD_DIM),
        scratch_shapes=[pltpu.VMEM((N_HEADS * tq, KV_W), F32), pltpu.VMEM((N_HEADS * tq, KV_W), F32)],
        compiler_params=_cp(("arbitrary", "arbitrary")),
        name="nsa_prompt",
    )(q, gates, kc, vc, kvs, kvw_pad, slope, frac_t, e_gate)


def _nsa_sample_body(pt_ref, q_ref, gt_ref, kc_ref, vc_ref, cache_ref, knew_ref, win_ref, wnew_ref, sl_ref, fr_ref,
                     eg_ref, o_ref, buf_ref, ks_ref, sem_ref, *, tq, ns, past):
    n_pages, page = pt_ref.shape[1], cache_ref.shape[2]
    tk = ns * SEL_BLOCK

    @pl.when(pl.program_id(0) == 0)
    def _():
        ks_ref[past:, :] = jnp.zeros((tk - past, ks_ref.shape[1]), F32)

    slot = _gather_pages(cache_ref, buf_ref, sem_ref, pt_ref)
    for j in range(n_pages):
        ks_ref[pl.ds(j * page, page), :] = buf_ref[slot, j].T
    ks_ref[past:past + tq, :] = knew_ref[0]
    wbuf = win_ref.shape[2]
    tw = wbuf + SEL_BLOCK
    w = jnp.concatenate([win_ref[0].T, wnew_ref[0], jnp.zeros((tw - wbuf - tq, 2 * KV_W), F32)], axis=0)
    slope = sl_ref[...]
    qfull, pq, o_c, sel_bias = _nsa_front(q_ref[0], kc_ref[0], vc_ref[0], past, slope, fr_ref[...], tq, ns)
    o_s = _nsa_selected(qfull, pq, sel_bias, ks_ref, slope, past // SEL_BLOCK, past // SEL_BLOCK)
    o_w = _nsa_window(qfull, pq, w[:, :KV_W], w[:, KV_W:], past - wbuf, slope)
    o_ref[0] = _nsa_combine(gt_ref[0], eg_ref, (o_c, o_s, o_w), tq)


def nsa_sample(page_table, q, gates, kc, vc, cache_t, kvs_new, win_t, kvw_new, slope, frac_t, e_gate):
    b, n_pages = page_table.shape
    _, width, page = cache_t.shape
    past = n_pages * page
    tq = q.shape[1]
    assert past % SEL_BLOCK + tq <= SEL_BLOCK
    ns = -(-(past + tq) // SEL_BLOCK)
    tk = ns * SEL_BLOCK
    full = lambda a: pl.BlockSpec(a.shape, lambda i, pt: (0,) * a.ndim)
    perb = lambda a: pl.BlockSpec((1,) + a.shape[1:], lambda i, pt: (i, 0, 0))
    return pl.pallas_call(
        functools.partial(_nsa_sample_body, tq=tq, ns=ns, past=past),
        out_shape=jax.ShapeDtypeStruct((b, tq, N_HEADS * HEAD_DIM), F32),
        grid_spec=pltpu.PrefetchScalarGridSpec(
            num_scalar_prefetch=1, grid=(b,),
            in_specs=[perb(q), perb(gates), perb(kc), perb(vc), pl.BlockSpec(memory_space=pl.ANY), perb(kvs_new),
                      perb(win_t), perb(kvw_new), full(slope), full(frac_t), full(e_gate)],
            out_specs=pl.BlockSpec((1, tq, N_HEADS * HEAD_DIM), lambda i, pt: (i, 0, 0)),
            scratch_shapes=[pltpu.VMEM((2, n_pages, width, page), F32), pltpu.VMEM((tk, width), F32),
                            pltpu.SemaphoreType.DMA((2,))]),
        compiler_params=_cp(("arbitrary",)),
        name="nsa_sample",
    )(page_table, q, gates, kc, vc, cache_t, kvs_new, win_t, kvw_new, slope, frac_t, e_gate)


def _pool_diff_body(ext_ref, d_ref, *, t, pos0):
    gw = ext_ref.shape[2] // len(POOL_SIZES)
    pos = pos0 + lax.broadcasted_iota(I32, (t, gw), 0)
    for gi, w in enumerate(POOL_SIZES):
        lanes = pl.ds(gi * gw, gw)
        cur = ext_ref[0, pl.ds(POOL_BUF, t), lanes]
        acc = cur
        for j in range(1, w):
            acc = acc + ext_ref[0, pl.ds(POOL_BUF - j, t), lanes]
        cnt = jnp.minimum(w, pos + 1).astype(F32)
        d_ref[0, :, lanes] = acc / cnt - cur


def pool_diff(ext, pos0):
    b, rows, width = ext.shape
    t = rows - POOL_BUF
    return pl.pallas_call(
        functools.partial(_pool_diff_body, t=t, pos0=pos0),
        out_shape=jax.ShapeDtypeStruct((b, t, width), F32),
        grid=(b,),
        in_specs=[pl.BlockSpec((1, rows, width), lambda i: (i, 0, 0))],
        out_specs=pl.BlockSpec((1, t, width), lambda i: (i, 0, 0)),
        compiler_params=_cp(("arbitrary",)),
        name="pool_diff",
    )(ext)


def _outproj_body(o_ref, d_ref, x_ref, gate_ref, sh_ref, sc_ref, g2_ref, wn_ref, wp_ref, wpool_ref, ps_ref,
                  x1_ref, h2_ref):
    gw = wpool_ref.shape[1]
    d = d_ref[...]
    yp = jnp.concatenate([_dot(d[:, g * gw:(g + 1) * gw].astype(BF16), wpool_ref[g]) for g in range(len(POOL_SIZES))],
                         axis=1) * ps_ref[...]
    mix = _dot(o_ref[...].astype(BF16), wn_ref[...]) + _dot(yp.astype(BF16), wp_ref[...])
    x1 = x_ref[...] + gate_ref[0] * mix
    x1_ref[...] = x1
    h2 = _rms_rows(x1) * g2_ref[...]
    h2_ref[...] = (h2 * (1.0 + sc_ref[0]) + sh_ref[0]).astype(BF16)


def outproj(o, d, x, gate, shift, scale, g2, wn, wp, wpool, ps, tm, tiles_per_mod):
    t, dm = x.shape
    full = lambda a: pl.BlockSpec(a.shape, lambda i: (0,) * a.ndim)
    row = lambda n: pl.BlockSpec((tm, n), lambda i: (i, 0))
    ms = lambda a: _mod_spec(a, tm, tiles_per_mod)
    return pl.pallas_call(
        _outproj_body,
        out_shape=(jax.ShapeDtypeStruct((t, dm), F32), jax.ShapeDtypeStruct((t, dm), BF16)),
        grid=(t // tm,),
        in_specs=[row(o.shape[1]), row(d.shape[1]), row(dm), ms(gate), ms(shift), ms(scale), full(g2),
                  full(wn), full(wp), full(wpool), full(ps)],
        out_specs=(row(dm), row(dm)),
        compiler_params=_cp(("arbitrary",)),
        name="outproj",
    )(o, d, x, gate, shift, scale, g2, wn, wp, wpool, ps)


_CAND_PIECES = ((0, 16), (1, 8), (2, 8), (3, 8), (4, 8), (5, 8), (6, 8), (7, 8))


def _top16_rows(s):
    kio = lax.broadcasted_iota(I32, s.shape, 0)
    nk = s.shape[0]
    rank = jnp.full(s.shape, float(PEER_TOPK), F32)
    vals, keys = [], []
    for it in range(PEER_TOPK):
        m = jnp.max(s, axis=0, keepdims=True)
        idx = jnp.min(jnp.where(s == m, kio, nk), axis=0, keepdims=True)
        hit = kio == idx
        rank = jnp.where(hit, float(it), rank)
        s = jnp.where(hit, -jnp.inf, s)
        vals.append(m)
        keys.append(idx)
    return vals, keys, rank


def _peer_route_body(h_ref, wq_ref, sk_ref, rk_ref, lim_ref, e1_ref, e2_ref, q_scr):
    hd = pl.program_id(1)

    qd = sk_ref.shape[3]

    @pl.when(hd == 0)
    def _():
        q = _dot(h_ref[...], wq_ref[...]).astype(BF16)
        for i in range(PEER_HEADS):
            q_scr[i] = q[:, i * 2 * qd:(i + 1) * 2 * qd]

    qh = q_scr[hd]
    s1 = _nt(sk_ref[0, 0], qh[:, :qd])
    s2 = _nt(sk_ref[1, 0], qh[:, qd:])
    v1, k1, _ = _top16_rows(s1)
    v2, _, rank2 = _top16_rows(s2)
    n = s1.shape[1]
    v2a = jnp.concatenate(v2, axis=0)
    top = v1[0] + v2[0]

    bio = {nb: lax.broadcasted_iota(I32, (nb, n), 0) for nb in (8, PEER_TOPK)}
    v2p = {8: jnp.concatenate(v2[:8], axis=0), PEER_TOPK: v2a}
    cands, flats = [], []
    for a, nb in _CAND_PIECES:
        c = v1[a] + v2p[nb]
        ok = (bio[nb] + 1) * (a + 1) <= PEER_TOPK
        cands.append(jnp.where(ok, c, -jnp.inf))
        flats.append(a * PEER_TOPK + bio[nb])
    v1b = jnp.concatenate(v1[8:], axis=0)
    cands.append(v1b + v2[0])
    flats.append((bio[8] + 8) * PEER_TOPK)
    cand = jnp.concatenate(cands, axis=0)
    flat = jnp.concatenate(flats, axis=0)
    big = PEER_TOPK * PEER_TOPK
    taken = jnp.zeros(cand.shape, jnp.bool_)
    z = jnp.zeros((1, n), F32)
    for _ in range(PEER_TOPK):
        m = jnp.max(cand, axis=0, keepdims=True)
        f = jnp.min(jnp.where(cand == m, flat, big), axis=0, keepdims=True)
        hit = flat == f
        taken = taken | hit
        cand = jnp.where(hit, -jnp.inf, cand)
        z = z + jnp.exp(m - top)
    takenf = taken.astype(F32)

    lim = jnp.zeros(s1.shape, F32)
    kio = lax.broadcasted_iota(I32, s1.shape, 0)
    off = 0
    for a, nb in _CAND_PIECES:
        cnt = jnp.sum(takenf[off:off + nb], axis=0, keepdims=True)
        lim = jnp.where(kio == k1[a], cnt, lim)
        off += nb
    for i in range(8):
        lim = jnp.where(kio == k1[8 + i], takenf[off + i:off + i + 1], lim)

    rk_ref[0] = rank2.astype(BF16)
    lim_ref[0] = lim
    e1_ref[0] = jnp.exp(s1 - v1[0]) / z
    e2_ref[0] = jnp.exp(s2 - v2[0]).astype(BF16)


def peer_route(h2, wq, sk, tt):
    t, d = h2.shape
    nk = sk.shape[2]
    out = jax.ShapeDtypeStruct((PEER_HEADS, nk, t), F32)
    outb = jax.ShapeDtypeStruct((PEER_HEADS, nk, t), BF16)
    ospec = pl.BlockSpec((1, nk, tt), lambda i, h: (h, 0, i))
    return pl.pallas_call(
        _peer_route_body,
        out_shape=(outb, out, out, outb),
        grid=(t // tt, PEER_HEADS),
        in_specs=[pl.BlockSpec((tt, d), lambda i, h: (i, 0)),
                  pl.BlockSpec(wq.shape, lambda i, h: (0, 0)),
                  pl.BlockSpec((2, 1) + sk.shape[2:], lambda i, h: (0, h, 0, 0))],
        out_specs=(ospec, ospec, ospec, ospec),
        scratch_shapes=[pltpu.VMEM((PEER_HEADS, tt, wq.shape[1] // PEER_HEADS), BF16)],
        compiler_params=_cp(("arbitrary", "arbitrary")),
        name="peer_route",
    )(h2, wq, sk)


def _peer_dense_body(h_ref, u_ref, v_ref, rk_ref, lim_ref, e1_ref, e2_ref, x1_ref, gate_ref, y_ref, w_scr, lb_scr,
                     eb_scr):
    j = pl.program_id(1)
    nk = rk_ref.shape[1]
    et, tt = u_ref.shape[0], h_ref.shape[0]
    rows = lb_scr.shape[2]

    @pl.when(j == 0)
    def _():
        y_ref[...] = jnp.zeros_like(y_ref)

    zero = jnp.zeros((), BF16)
    for rr in range(et // nk):
        r = j * (et // nk) + rr
        for hd in range(PEER_HEADS):
            lb_scr[rr, hd] = jnp.broadcast_to(lim_ref[hd, pl.ds(r, 1), :], (rows, tt)).astype(BF16)
            eb_scr[rr, hd] = jnp.broadcast_to(e1_ref[hd, pl.ds(r, 1), :], (rows, tt)).astype(BF16)
    for rr in range(et // nk):
        for ch in range(nk // rows):
            sl = pl.ds(ch * rows, rows)
            w = jnp.zeros((rows, tt), BF16)
            for hd in range(PEER_HEADS):
                w = w + jnp.where(rk_ref[hd, sl, :] < lb_scr[rr, hd], e2_ref[hd, sl, :], zero) * eb_scr[rr, hd]
            w_scr[pl.ds(rr * nk + ch * rows, rows), :] = w
    a = _nt(u_ref[...], h_ref[...])
    act = a * (lax.erf(a * np.float32(1.0 / np.sqrt(2.0))) + 1.0) * 0.5
    wt = (act * w_scr[...].astype(F32)).astype(BF16)
    y_ref[...] += _tn(wt, v_ref[...])

    @pl.when(j == pl.num_programs(1) - 1)
    def _():
        y_ref[...] = x1_ref[...] + gate_ref[0] * y_ref[...]


def peer_dense(h2, u, v, rk, lim, e1, e2, x1, gate, tt, et, tiles_per_mod):
    t, d = h2.shape
    ne = u.shape[0]
    nk = rk.shape[1]
    once = pl.Buffered(1)
    tok = pl.BlockSpec((tt, d), lambda i, j: (i, 0), pipeline_mode=once)
    exp = pl.BlockSpec((et, d), lambda i, j: (j, 0))
    rt = pl.BlockSpec((PEER_HEADS, nk, tt), lambda i, j: (0, 0, i), pipeline_mode=once)
    if gate.shape[1] == 1:
        gspec = pl.BlockSpec((1, 1, d), lambda i, j: (i // tiles_per_mod, 0, 0), pipeline_mode=once)
    else:
        gspec = pl.BlockSpec((1, tt, d), lambda i, j: (i, 0, 0), pipeline_mode=once)
    return pl.pallas_call(
        _peer_dense_body,
        out_shape=jax.ShapeDtypeStruct((t, d), F32),
        grid=(t // tt, ne // et),
        in_specs=[tok, exp, exp, rt, rt, rt, rt, tok, gspec],
        out_specs=pl.BlockSpec((tt, d), lambda i, j: (i, 0)),
        scratch_shapes=[pltpu.VMEM((et, tt), BF16), pltpu.VMEM((et // nk, PEER_HEADS, BF16_ROWS, tt), BF16),
                        pltpu.VMEM((et // nk, PEER_HEADS, BF16_ROWS, tt), BF16)],
        compiler_params=_cp(("arbitrary", "arbitrary")),
        name="peer_dense",
    )(h2, u, v, rk, lim, e1, e2, x1, gate)


def _slope_lanes(tq):
    h = np.arange(1, N_HEADS + 1, dtype=np.float32)
    s = (2.0 ** (-8.0 * h / N_HEADS)).reshape(N_KV, GROUP)
    return jnp.asarray(np.repeat(s.T.reshape(-1), tq)[None, :], F32)


def _frac_t(nc, ns, nc_pad, ns_pad):
    pos = np.arange(nc)[:, None] * CMP_STRIDE + np.arange(CMP_BLOCK)[None, :]
    f = ((pos // SEL_BLOCK)[:, :, None] == np.arange(ns)[None, None, :]).mean(axis=1)
    out = np.zeros((ns_pad, nc_pad), np.float32)
    out[:ns, :nc] = f.T
    return jnp.asarray(out)


def _gate_expand(width):
    e = np.zeros((3, width, N_HEADS * HEAD_DIM), np.float32)
    for g in range(N_KV):
        for r in range(GROUP):
            for j in range(3):
                c0 = r * KV_W + g * HEAD_DIM
                e[j, (g * GROUP + r) * 3 + j, c0:c0 + HEAD_DIM] = 1.0
    return jnp.asarray(e, BF16)


def _rgd(a, axis):
    shp = a.shape
    a = a.reshape(shp[:axis] + (N_KV, GROUP, HEAD_DIM) + shp[axis + 1:])
    a = jnp.swapaxes(a, axis, axis + 1)
    return a.reshape(shp)


def kernel(x_prompt, x_sample, cache_kv_cmp, cache_kv_slc, cache_kv_win, state_pool, page_table, c_prompt, c_sample,
           w_ada, b_ada, g_norm1, g_norm2, w_in, w_out, g_qnorm, g_knorm, cmp_pe, w_cmp, w_pool, pool_scale,
           peer_wq, peer_subkeys, peer_u, peer_v):
    bp, tp, dm = x_prompt.shape
    bs, ts, _ = x_sample.shape
    n_pages = page_table.shape[1]
    page = cache_kv_cmp.shape[1]
    past = n_pages * page
    nsa_w = N_HEADS * HEAD_DIM
    kv3 = 3 * 2 * KV_W
    ngl = 3 * N_HEADS
    gpad = 128
    tm, tq, sel_bucket, tt, et = 256, 32, 4, 512, 1024

    wq = _rgd(w_in[:, :nsa_w], 1).astype(BF16)
    wkv = w_in[:, nsa_w:nsa_w + kv3].astype(BF16)
    wg = jnp.pad(w_in[:, nsa_w + kv3:nsa_w + kv3 + ngl], ((0, 0), (0, gpad - ngl))).astype(BF16)
    wu = w_in[:, nsa_w + kv3 + ngl:].astype(BF16)
    gq = jnp.tile(g_qnorm, N_HEADS)[None, :]
    gk = jnp.tile(g_knorm, (1, N_KV))
    g1 = g_norm1[None, :]
    g2 = g_norm2[None, :]
    wn = _rgd(w_out[:nsa_w], 0).astype(BF16)
    wp = w_out[nsa_w:].astype(BF16)
    wpool = w_pool.astype(BF16)
    ps = pool_scale[None, :]
    wc = jnp.einsum('cglde,gh->clgdhe', w_cmp, jnp.eye(N_KV, dtype=F32)).reshape(2, CMP_BLOCK, KV_W, KV_W).astype(BF16)
    pe = jnp.transpose(cmp_pe, (1, 0, 2, 3)).reshape(CMP_BLOCK, 2 * KV_W)
    e_gate = _gate_expand(gpad)
    pwq = peer_wq.astype(BF16)
    psk = peer_subkeys.astype(BF16)
    pu = peer_u.astype(BF16)
    pv = peer_v.astype(BF16)

    mod = adaln(jnp.concatenate([c_prompt, c_sample], axis=0), w_ada, b_ada).reshape(bp + bs, 6, dm)
    mod_p = [mod[:bp, k][:, None, :] for k in range(6)]
    mod_s = [jnp.repeat(mod[bp:, k], ts, axis=0).reshape(bs * ts // tm, tm, dm) for k in range(6)]

    xp = x_prompt.reshape(bp * tp, dm)
    q, kvc, kvs, kvw, gates, u = inproj(xp, mod_p[0], mod_p[1], g1, wq, wkv, wg, wu, gq, gk, tm, tp // tm)
    kc, vc = compress_prompt(kvc.reshape(bp, tp // CMP_STRIDE, CMP_STRIDE * 2 * KV_W), pe, wc, gk)
    nc = (tp - CMP_BLOCK) // CMP_STRIDE + 1
    kvw3 = kvw.reshape(bp, tp, 2 * KV_W)
    o_nsa = nsa_prompt(q, gates, kc, vc, kvs.reshape(bp, tp, 2 * KV_W), jnp.pad(kvw3, ((0, 0), (WINDOW, 0), (0, 0))),
                       _slope_lanes(tq), _frac_t(nc, tp // SEL_BLOCK, tp // CMP_STRIDE, 40), e_gate, tq, sel_bucket)
    u3 = u.reshape(bp, tp, -1)
    d_pool = pool_diff(jnp.pad(u3, ((0, 0), (POOL_BUF, 0), (0, 0))), 0).reshape(bp * tp, -1)
    x1, h2 = outproj(o_nsa, d_pool, xp, mod_p[2], mod_p[3], mod_p[4], g2, wn, wp, wpool, ps, tm, tp // tm)
    routes = peer_route(h2, pwq, psk, 256)
    y_prompt = peer_dense(h2, pu, pv, *routes, x1, mod_p[5], tt, et, tp // tt).reshape(bp, tp, dm)
    shp_p = (bp, tp, 2, N_KV, HEAD_DIM)
    win_p = kvw3[:, -min(WINDOW, tp):].reshape(bp, -1, 2, N_KV, HEAD_DIM)
    pool_p = u3[:, -POOL_BUF:]

    xs = x_sample.reshape(bs * ts, dm)
    q, kvc_s, kvs_s, kvw_s, gates, u = inproj(xs, mod_s[0], mod_s[1], g1, wq, wkv, wg, wu, gq, gk, tm, 1)
    feat_major = lambda c: jnp.transpose(c, (0, 2, 3, 4, 1)).reshape(c.shape[0], 2 * KV_W, c.shape[1])
    kc, vc = compress_sample(page_table, feat_major(cache_kv_cmp), pe, wc, gk)
    nc = (past + ts - CMP_BLOCK) // CMP_STRIDE + 1
    ns = -(-(past + ts) // SEL_BLOCK)
    kvw_s3 = kvw_s.reshape(bs, ts, 2 * KV_W)
    o_nsa = nsa_sample(page_table, q.reshape(bs, ts, -1), gates.reshape(bs, ts, -1), kc, vc,
                       feat_major(cache_kv_slc), kvs_s.reshape(bs, ts, 2 * KV_W), feat_major(cache_kv_win), kvw_s3,
                       _slope_lanes(ts), _frac_t(nc, ns, past // CMP_STRIDE, 40), e_gate).reshape(bs * ts, -1)
    ext = jnp.concatenate([state_pool, u.reshape(bs, ts, -1)], axis=1)
    d_pool = pool_diff(ext, past).reshape(bs * ts, -1)
    x1, h2 = outproj(o_nsa, d_pool, xs, mod_s[2], mod_s[3], mod_s[4], g2, wn, wp, wpool, ps, tm, 1)
    routes = peer_route(h2, pwq, psk, 256)
    gate5 = mod_s[5].reshape(bs * ts // tt, tt, dm)
    y_sample = peer_dense(h2, pu, pv, *routes, x1, gate5, tt, et, 1).reshape(bs, ts, dm)
    shp_s = (bs, ts, 2, N_KV, HEAD_DIM)
    wbuf = cache_kv_win.shape[1]
    win_s = jnp.concatenate([cache_kv_win, kvw_s.reshape(shp_s)], axis=1)[:, -wbuf:]
    pool_s = ext[:, -POOL_BUF:]

    return (y_prompt, y_sample, kvc.reshape(shp_p), kvs.reshape(shp_p), win_p, pool_p,
            kvc_s.reshape(shp_s), kvs_s.reshape(shp_s), win_s, pool_s)
```

```python
import functools

import numpy as np
import jax
import jax.numpy as jnp
from jax import lax
from jax.experimental import pallas as pl
from jax.experimental.pallas import tpu as pltpu

F32, BF16, I32 = jnp.float32, jnp.bfloat16, jnp.int32

HEAD_DIM = 64
N_KV = 4
GROUP = 4
N_HEADS = N_KV * GROUP
CMP_BLOCK = 32
CMP_STRIDE = 16
SEL_BLOCK = 64
SEL_TOP = 16
WINDOW = 512
POOL_SIZES = (2, 4, 8, 16)
POOL_BUF = max(POOL_SIZES) - 1
PEER_HEADS = 8
N_KEYS = 128
PEER_TOPK = 16
EPS = 1e-6
NEG = -1e30
FORCE_BONUS = 1e4
KV_W = N_KV * HEAD_DIM
VMEM_LIMIT = 56 * 1024 * 1024


def _cp(sem, vmem=VMEM_LIMIT):
    return pltpu.CompilerParams(dimension_semantics=sem, vmem_limit_bytes=vmem)


def _dot(a, b):
    return jnp.dot(a, b, preferred_element_type=F32)


def _nt(a, b):
    return lax.dot_general(a, b, (((1,), (1,)), ((), ())), preferred_element_type=F32)


def _tn(a, b):
    return lax.dot_general(a, b, (((0,), (0,)), ((), ())), preferred_element_type=F32)


def _split_dot(x, w):
    hi = x.astype(BF16)
    r1 = x - hi.astype(F32)
    mid = r1.astype(BF16)
    lo = (r1 - mid.astype(F32)).astype(BF16)
    return _dot(hi, w) + _dot(mid, w) + _dot(lo, w)


def _ones64():
    r = lax.broadcasted_iota(I32, (KV_W, KV_W), 0) // HEAD_DIM
    c = lax.broadcasted_iota(I32, (KV_W, KV_W), 1) // HEAD_DIM
    return (r == c).astype(BF16)


def _rms64(x, ones):
    ms = _split_dot(x * x, ones) * (1.0 / HEAD_DIM)
    return x * lax.rsqrt(ms + EPS)


def _rms_rows(x):
    return x * lax.rsqrt(jnp.mean(x * x, axis=-1, keepdims=True) + EPS)


def _adaln_body(c_ref, w_ref, b_ref, o_ref):
    c = c_ref[...]
    s = (c * jax.nn.sigmoid(c)).astype(BF16)
    o_ref[...] = _dot(s, w_ref[...].astype(BF16)) + b_ref[...]


def adaln(c_all, w_ada, b_ada, tn=1024):
    m, d = c_all.shape
    n = w_ada.shape[1]
    return pl.pallas_call(
        _adaln_body,
        out_shape=jax.ShapeDtypeStruct((m, n), F32),
        grid=(n // tn,),
        in_specs=[pl.BlockSpec((m, d), lambda j: (0, 0)),
                  pl.BlockSpec((d, tn), lambda j: (0, j)),
                  pl.BlockSpec((1, tn), lambda j: (0, j))],
        out_specs=pl.BlockSpec((m, tn), lambda j: (0, j)),
        compiler_params=_cp(("arbitrary",)),
        name="adaln",
    )(c_all, w_ada, b_ada.reshape(1, n))


def _inproj_body(x_ref, sh_ref, sc_ref, g1_ref, wq_ref, wkv_ref, wg_ref, wu_ref, gq_ref, gk_ref,
                 q_ref, kvc_ref, kvs_ref, kvw_ref, gt_ref, u_ref, *t_refs):
    h = _rms_rows(x_ref[...]) * g1_ref[...]
    h = h * (1.0 + sc_ref[0]) + sh_ref[0]
    hb = h.astype(BF16)
    ones = _ones64()
    zq = _dot(hb, wq_ref[...])
    for r in range(GROUP):
        sl = slice(r * KV_W, (r + 1) * KV_W)
        q_ref[:, sl] = _rms64(zq[:, sl], ones) * gq_ref[:, sl] * (HEAD_DIM ** -0.5)
    zkv = _dot(hb, wkv_ref[...])
    kvc_ref[...] = zkv[:, 0:2 * KV_W]
    kvs_ref[:, 0:KV_W] = _rms64(zkv[:, 2 * KV_W:3 * KV_W], ones) * gk_ref[1:2, :]
    kvs_ref[:, KV_W:] = zkv[:, 3 * KV_W:4 * KV_W]
    kvw_ref[:, 0:KV_W] = _rms64(zkv[:, 4 * KV_W:5 * KV_W], ones) * gk_ref[2:3, :]
    kvw_ref[:, KV_W:] = zkv[:, 5 * KV_W:6 * KV_W]
    gt_ref[...] = jax.nn.sigmoid(_dot(hb, wg_ref[...]))
    u_ref[...] = _dot(hb, wu_ref[...])
    for src, dst in zip((kvc_ref, kvs_ref, kvw_ref), t_refs):
        dst[0] = src[...].T


def _mod_spec(arr, tm, rows_per_mod):
    d = arr.shape[-1]
    if arr.shape[1] == 1:
        return pl.BlockSpec((1, 1, d), lambda i: (i // rows_per_mod, 0, 0))
    return pl.BlockSpec((1, tm, d), lambda i: (i, 0, 0))


def inproj(x, shift, scale, g1, wq, wkv, wg, wu, gq, gk, tm, tiles_per_mod, seq_len=None):
    t, d = x.shape
    nq, nkv, ng, nu = wq.shape[1], wkv.shape[1], wg.shape[1], wu.shape[1]
    full = lambda a: pl.BlockSpec(a.shape, lambda i: (0,) * a.ndim)
    row = lambda n: pl.BlockSpec((tm, n), lambda i: (i, 0))
    kvrow = jax.ShapeDtypeStruct((t, 2 * KV_W), F32)
    out_shape = [jax.ShapeDtypeStruct((t, nq), F32), kvrow, kvrow, kvrow,
                 jax.ShapeDtypeStruct((t, ng), F32), jax.ShapeDtypeStruct((t, nu), F32)]
    out_specs = [row(nq), row(2 * KV_W), row(2 * KV_W), row(2 * KV_W), row(ng), row(nu)]
    if seq_len is not None:
        tiles = seq_len // tm
        out_shape += [jax.ShapeDtypeStruct((t // seq_len, 2 * KV_W, seq_len), F32)] * 3
        out_specs += [pl.BlockSpec((1, 2 * KV_W, tm), lambda i: (i // tiles, 0, i % tiles))] * 3
    return pl.pallas_call(
        _inproj_body,
        out_shape=tuple(out_shape),
        grid=(t // tm,),
        in_specs=[row(d), _mod_spec(shift, tm, tiles_per_mod), _mod_spec(scale, tm, tiles_per_mod), full(g1),
                  full(wq), full(wkv), full(wg), full(wu), full(gq), full(gk)],
        out_specs=tuple(out_specs),
        compiler_params=_cp(("arbitrary",)),
        name="inproj",
    )(x, shift, scale, g1, wq, wkv, wg, wu, gq, gk)


def _compress(get_x, nchunk, pe_ref, w_ref, gk_ref, kc_ref, vc_ref):
    half = CMP_BLOCK // 2
    acc = [jnp.zeros((nchunk, KV_W), F32) for _ in range(4)]
    for l in range(half):
        xl = get_x(l)
        a = (xl + pe_ref[l:l + 1, :]).astype(BF16)
        b = (xl + pe_ref[half + l:half + l + 1, :]).astype(BF16)
        acc[0] += _dot(a[:, :KV_W], w_ref[0, l])
        acc[1] += _dot(a[:, KV_W:], w_ref[1, l])
        acc[2] += _dot(b[:, :KV_W], w_ref[0, half + l])
        acc[3] += _dot(b[:, KV_W:], w_ref[1, half + l])
    rio = lax.broadcasted_iota(I32, (nchunk, KV_W), 0)
    nxt = lambda v: jnp.where(rio < nchunk - 1, pltpu.roll(v, nchunk - 1, 0), 0.0)
    ck = acc[0] + nxt(acc[2])
    cv = acc[1] + nxt(acc[3])
    kc_ref[...] = _rms64(ck, _ones64()) * gk_ref[0:1, :]
    vc_ref[...] = cv


def _compress_prompt_body(x_ref, pe_ref, w_ref, gk_ref, kc_ref, vc_ref):
    row_w = 2 * KV_W
    _compress(lambda l: x_ref[0, :, l * row_w:(l + 1) * row_w], x_ref.shape[1], pe_ref, w_ref, gk_ref,
              kc_ref.at[0], vc_ref.at[0])


def compress_prompt(kv_chunks, pe, wc, gk):
    b, nchunk, width = kv_chunks.shape
    full = lambda a: pl.BlockSpec(a.shape, lambda i: (0,) * a.ndim)
    out = jax.ShapeDtypeStruct((b, nchunk, KV_W), F32)
    ospec = pl.BlockSpec((1, nchunk, KV_W), lambda i: (i, 0, 0))
    return pl.pallas_call(
        _compress_prompt_body,
        out_shape=(out, out),
        grid=(b,),
        in_specs=[pl.BlockSpec((1, nchunk, width), lambda i: (i, 0, 0)), full(pe), full(wc), full(gk)],
        out_specs=(ospec, ospec),
        compiler_params=_cp(("arbitrary",)),
        name="compress_prompt",
    )(kv_chunks, pe, wc, gk)


def _page_copy(cache_ref, buf_ref, sem_ref, pt_ref, seq, slot, j):
    return pltpu.make_async_copy(cache_ref.at[pt_ref[seq, j]], buf_ref.at[slot, j], sem_ref.at[slot])


def _gather_pages(cache_ref, buf_ref, sem_ref, pt_ref):
    b = pl.program_id(0)
    nb = pl.num_programs(0)
    n_pages = pt_ref.shape[1]
    slot = b % 2

    def start(seq, s):
        for j in range(n_pages):
            _page_copy(cache_ref, buf_ref, sem_ref, pt_ref, seq, s, j).start()

    @pl.when(b == 0)
    def _():
        start(b, slot)

    @pl.when(b + 1 < nb)
    def _():
        start(b + 1, 1 - slot)

    for j in range(n_pages):
        _page_copy(cache_ref, buf_ref, sem_ref, pt_ref, b, slot, j).wait()
    return slot


LANES = 128
BF16_ROWS = 16


def _compress_sample_body(pt_ref, cache_ref, pe_ref, w_ref, gk_ref, kc_ref, vc_ref, buf_ref, xs_ref, sem_ref):
    n_pages, page = pt_ref.shape[1], cache_ref.shape[2]
    slot = _gather_pages(cache_ref, buf_ref, sem_ref, pt_ref)
    nlb = xs_ref.shape[0]
    for j in range(n_pages):
        for k in range(nlb):
            xs_ref[k, pl.ds(j * page, page), :] = buf_ref[slot, j, pl.ds(k * LANES, LANES), :].T
    nchunk = n_pages * page // CMP_STRIDE
    get_x = lambda l: jnp.concatenate([xs_ref[k, pl.ds(l, nchunk, stride=CMP_STRIDE), :] for k in range(nlb)], axis=1)
    _compress(get_x, nchunk, pe_ref, w_ref, gk_ref, kc_ref.at[0], vc_ref.at[0])


def compress_sample(page_table, cache_t, pe, wc, gk):
    b, n_pages = page_table.shape
    _, width, page = cache_t.shape
    nchunk = n_pages * page // CMP_STRIDE
    full = lambda a: pl.BlockSpec(a.shape, lambda i, pt: (0,) * a.ndim)
    out = jax.ShapeDtypeStruct((b, nchunk, KV_W), F32)
    ospec = pl.BlockSpec((1, nchunk, KV_W), lambda i, pt: (i, 0, 0))
    return pl.pallas_call(
        _compress_sample_body,
        out_shape=(out, out),
        grid_spec=pltpu.PrefetchScalarGridSpec(
            num_scalar_prefetch=1, grid=(b,),
            in_specs=[pl.BlockSpec(memory_space=pl.ANY), full(pe), full(wc), full(gk)],
            out_specs=(ospec, ospec),
            scratch_shapes=[pltpu.VMEM((2, n_pages, width, page), F32),
                            pltpu.VMEM((width // LANES, n_pages * page, LANES), F32),
                            pltpu.SemaphoreType.DMA((2,))]),
        compiler_params=_cp(("arbitrary",)),
        name="compress_sample",
    )(page_table, cache_t, pe, wc, gk)


def _softmax_cols(s, valid):
    s = jnp.where(valid, s, NEG)
    m = jnp.max(s, axis=0, keepdims=True)
    e = jnp.where(valid, jnp.exp(s - m), 0.0)
    return e / jnp.maximum(jnp.sum(e, axis=0, keepdims=True), 1e-30)


def _nsa_front(q, kc, vc, pq0, slope, frac_t, tq, ns):
    cols = N_HEADS * tq
    lane = lax.broadcasted_iota(I32, (1, cols), 1)
    pq = (pq0 + lane % tq).astype(F32)
    lg = lax.broadcasted_iota(I32, (tq, KV_W), 1) // HEAD_DIM
    qfull = jnp.concatenate([jnp.where(lg == g, q[:, r * KV_W:(r + 1) * KV_W], 0.0)
                             for r in range(GROUP) for g in range(N_KV)], axis=0).astype(BF16)

    nc = kc.shape[0]
    pos_c = (lax.broadcasted_iota(I32, (nc, cols), 0) * CMP_STRIDE + (CMP_BLOCK - 1)).astype(F32)
    dist_c = pq - pos_c
    p_c = _softmax_cols(_nt(kc.astype(BF16), qfull) - slope * dist_c, dist_c >= 0).astype(BF16)
    o_c = _tn(p_c, vc.astype(BF16))

    imp = _dot(frac_t.astype(BF16), p_c)
    imp = imp + pltpu.roll(imp, 4 * tq, 1) + pltpu.roll(imp, 8 * tq, 1) + pltpu.roll(imp, 12 * tq, 1)
    gq = N_KV * tq
    compact = gq % LANES == 0
    wsel = gq if compact else cols
    imp = imp[:, :wsel]
    nsp = imp.shape[0]
    jio = lax.broadcasted_iota(I32, (nsp, wsel), 0)
    pos_q = pq0 + lax.broadcasted_iota(I32, (1, wsel), 1) % tq
    blk_q = pos_q // SEL_BLOCK
    forced = (jio == 0) | (jio == blk_q) | (jio == blk_q - 1)
    valid_blk = jio * SEL_BLOCK <= pos_q
    imp = jnp.where(valid_blk, imp + FORCE_BONUS * forced.astype(F32), NEG)
    rank = jnp.zeros((nsp, wsel), I32)
    for i in range(ns):
        row = imp[i:i + 1, :]
        rank = rank + ((row > imp) | ((row == imp) & (i < jio))).astype(I32)
    sel_bias = jnp.where(rank < min(SEL_TOP, ns), 0.0, NEG)
    if compact:
        sel_bias = jnp.concatenate([sel_bias] * GROUP, axis=1)
    return qfull, pq, o_c, sel_bias


def _as_column(row):
    return jnp.transpose(jnp.broadcast_to(row, (LANES, row.shape[1])))[:, 0:1]


def _nsa_selected(qfull, pq, sel_bias, ks_ref, slope, nblk, blk_q):
    cols = qfull.shape[0]
    nsp = sel_bias.shape[0]
    tk = nblk * SEL_BLOCK
    jio = lax.broadcasted_iota(I32, (nsp, cols), 0)
    bias = jnp.where(jio < blk_q, sel_bias, NEG) + slope * (jio * SEL_BLOCK).astype(F32)
    inblk = slope * lax.broadcasted_iota(I32, (SEL_BLOCK, cols), 0).astype(F32)
    s_m = _nt(ks_ref[0:tk, 0:KV_W].astype(BF16), qfull).reshape(nblk, SEL_BLOCK, cols)
    s_m = s_m + inblk[None, :, :] + bias[:nblk][:, None, :]

    d0 = blk_q * SEL_BLOCK
    kd = ks_ref[pl.ds(d0 if isinstance(d0, int) else pl.multiple_of(d0, SEL_BLOCK), SEL_BLOCK), :]
    kpos_d = (blk_q * SEL_BLOCK + lax.broadcasted_iota(I32, (SEL_BLOCK, cols), 0)).astype(F32)
    valid_d = kpos_d <= pq
    s_d = jnp.where(valid_d, _nt(kd[:, :KV_W].astype(BF16), qfull) + slope * kpos_d, NEG)

    m = jnp.maximum(jnp.max(jnp.max(s_m, axis=0), axis=0, keepdims=True), jnp.max(s_d, axis=0, keepdims=True))
    e_m = jnp.exp(s_m - m[None, :, :])
    e_d = jnp.where(valid_d, jnp.exp(s_d - m), 0.0)
    l = jnp.sum(jnp.sum(e_m, axis=0), axis=0, keepdims=True) + jnp.sum(e_d, axis=0, keepdims=True)
    o = _tn(e_m.astype(BF16).reshape(tk, cols), ks_ref[0:tk, KV_W:].astype(BF16)) + _tn(e_d.astype(BF16),
                                                                                       kd[:, KV_W:].astype(BF16))
    return o * _as_column(1.0 / jnp.maximum(l, 1e-30))


def _nsa_window(qfull, pq, kw, vw, pw0, slope):
    tw, cols = kw.shape[0], qfull.shape[0]
    pos_w = (pw0 + lax.broadcasted_iota(I32, (tw, cols), 0)).astype(F32)
    dist_w = pq - pos_w
    valid_w = (dist_w >= 0) & (dist_w <= WINDOW) & (pos_w >= 0)
    p_w = _softmax_cols(_nt(kw.astype(BF16), qfull) - slope * dist_w, valid_w).astype(BF16)
    return _tn(p_w, vw.astype(BF16))


def _nsa_window_steady(qfull, kw, vw, slope, tq):
    cols = qfull.shape[0]
    nb = WINDOW // tq
    qq = lax.broadcasted_iota(I32, (1, cols), 1) % tq
    rio = lax.broadcasted_iota(I32, (tq, cols), 0)
    s = _nt(kw.astype(BF16), qfull).reshape(nb + 1, tq, cols) + (slope * rio.astype(F32))[None, :, :]
    s = s + (slope * (lax.broadcasted_iota(I32, (nb + 1, cols), 0) * tq).astype(F32))[:, None, :]
    ok_f, ok_l = rio >= qq, rio <= qq
    s_f, s_m, s_l = jnp.where(ok_f, s[0], NEG), s[1:nb], jnp.where(ok_l, s[nb], NEG)
    m = jnp.maximum(jnp.max(jnp.max(s_m, axis=0), axis=0, keepdims=True),
                    jnp.max(jnp.maximum(s_f, s_l), axis=0, keepdims=True))
    e_f = jnp.where(ok_f, jnp.exp(s_f - m), 0.0)
    e_l = jnp.where(ok_l, jnp.exp(s_l - m), 0.0)
    e_m = jnp.exp(s_m - m[None, :, :])
    l = jnp.sum(jnp.sum(e_m, axis=0), axis=0, keepdims=True) + jnp.sum(e_f + e_l, axis=0, keepdims=True)
    v = vw.astype(BF16)
    o = (_tn(e_m.astype(BF16).reshape((nb - 1) * tq, cols), v[tq:nb * tq]) + _tn(e_f.astype(BF16), v[0:tq])
         + _tn(e_l.astype(BF16), v[nb * tq:]))
    return o * _as_column(1.0 / jnp.maximum(l, 1e-30))


def _nsa_combine(gates, e_gate, outs, tq):
    cols = N_HEADS * tq
    row_g = (lax.broadcasted_iota(I32, (cols, KV_W), 0) // tq) % N_KV
    own = row_g == lax.broadcasted_iota(I32, (cols, KV_W), 1) // HEAD_DIM

    def fold(o):
        o = jnp.where(own, o, 0.0)
        return jnp.concatenate(
            [sum(o[(r * N_KV + g) * tq:(r * N_KV + g + 1) * tq, :] for g in range(N_KV)) for r in range(GROUP)], axis=1)

    out = jnp.zeros((tq, N_HEADS * HEAD_DIM), F32)
    for j, o in enumerate(outs):
        out = out + _split_dot(gates, e_gate[j]) * fold(o)
    return out


def _nsa_prompt_body(q_ref, gt_ref, kc_ref, vc_ref, ks_ref, kw_ref, sl_ref, fr_ref, eg_ref, o_ref, os_ref, ow_ref,
                     *, tq, ns, tw, bucket):
    q0 = pl.program_id(1) * tq
    slope = sl_ref[...]
    qfull, pq, o_c, sel_bias = _nsa_front(q_ref[...], kc_ref[0], vc_ref[0], q0, slope, fr_ref[...], tq, ns)
    blk_q = q0 // SEL_BLOCK
    for k in range(ns // bucket):
        @pl.when(blk_q // bucket == k)
        def _():
            os_ref[...] = _nsa_selected(qfull, pq, sel_bias, ks_ref.at[0], slope, (k + 1) * bucket, blk_q)
    w = kw_ref[0, pl.ds(pl.multiple_of(q0, 8), tw), :]

    @pl.when(q0 >= WINDOW)
    def _():
        ow_ref[...] = _nsa_window_steady(qfull, w[:, :KV_W], w[:, KV_W:], slope, tq)

    @pl.when(q0 < WINDOW)
    def _():
        ow_ref[...] = _nsa_window(qfull, pq, w[:, :KV_W], w[:, KV_W:], q0 - WINDOW, slope)

    o_ref[...] = _nsa_combine(gt_ref[...], eg_ref, (o_c, os_ref[...], ow_ref[...]), tq)


def nsa_prompt(q, gates, kc, vc, kvs, kvw_pad, slope, frac_t, e_gate, tq, bucket):
    b, t, _ = kvs.shape
    ns = t // SEL_BLOCK
    assert SEL_BLOCK % tq == 0 and ns % bucket == 0 and t % SEL_BLOCK == 0
    assert WINDOW % tq == 0 and tq % 8 == 0
    tw = WINDOW + tq
    nq = t // tq
    full = lambda a: pl.BlockSpec(a.shape, lambda i, j: (0,) * a.ndim)
    perb = lambda a: pl.BlockSpec((1,) + a.shape[1:], lambda i, j: (i, 0, 0))
    row = lambda n: pl.BlockSpec((tq, n), lambda i, j: (i * nq + j, 0))
    return pl.pallas_call(
        functools.partial(_nsa_prompt_body, tq=tq, ns=ns, tw=tw, bucket=bucket),
        out_shape=jax.ShapeDtypeStruct((b * t, N_HEADS * HEAD_DIM), F32),
        grid=(b, nq),
        in_specs=[row(q.shape[1]), row(gates.shape[1]), perb(kc), perb(vc), perb(kvs), perb(kvw_pad),
                  full(slope), full(frac_t), full(e_gate)],
        out_specs=row(N_HEADS * HEAD_DIM),
        scratch_shapes=[pltpu.VMEM((N_HEADS * tq, KV_W), F32), pltpu.VMEM((N_HEADS * tq, KV_W), F32)],
        compiler_params=_cp(("arbitrary", "arbitrary")),
        name="nsa_prompt",
    )(q, gates, kc, vc, kvs, kvw_pad, slope, frac_t, e_gate)


def _nsa_sample_body(pt_ref, q_ref, gt_ref, kc_ref, vc_ref, cache_ref, knew_ref, win_ref, wnew_ref, sl_ref, fr_ref,
                     eg_ref, o_ref, buf_ref, ks_ref, sem_ref, *, tq, ns, past):
    n_pages, page = pt_ref.shape[1], cache_ref.shape[2]
    tk = ns * SEL_BLOCK

    @pl.when(pl.program_id(0) == 0)
    def _():
        ks_ref[past:, :] = jnp.zeros((tk - past, ks_ref.shape[1]), F32)

    slot = _gather_pages(cache_ref, buf_ref, sem_ref, pt_ref)
    for j in range(n_pages):
        ks_ref[pl.ds(j * page, page), :] = buf_ref[slot, j].T
    ks_ref[past:past + tq, :] = knew_ref[0]
    wbuf = win_ref.shape[2]
    tw = wbuf + SEL_BLOCK
    w = jnp.concatenate([win_ref[0].T, wnew_ref[0], jnp.zeros((tw - wbuf - tq, 2 * KV_W), F32)], axis=0)
    slope = sl_ref[...]
    qfull, pq, o_c, sel_bias = _nsa_front(q_ref[0], kc_ref[0], vc_ref[0], past, slope, fr_ref[...], tq, ns)
    o_s = _nsa_selected(qfull, pq, sel_bias, ks_ref, slope, past // SEL_BLOCK, past // SEL_BLOCK)
    o_w = _nsa_window(qfull, pq, w[:, :KV_W], w[:, KV_W:], past - wbuf, slope)
    o_ref[0] = _nsa_combine(gt_ref[0], eg_ref, (o_c, o_s, o_w), tq)


def nsa_sample(page_table, q, gates, kc, vc, cache_t, kvs_new, win_t, kvw_new, slope, frac_t, e_gate):
    b, n_pages = page_table.shape
    _, width, page = cache_t.shape
    past = n_pages * page
    tq = q.shape[1]
    assert past % SEL_BLOCK + tq <= SEL_BLOCK
    ns = -(-(past + tq) // SEL_BLOCK)
    tk = ns * SEL_BLOCK
    full = lambda a: pl.BlockSpec(a.shape, lambda i, pt: (0,) * a.ndim)
    perb = lambda a: pl.BlockSpec((1,) + a.shape[1:], lambda i, pt: (i, 0, 0))
    return pl.pallas_call(
        functools.partial(_nsa_sample_body, tq=tq, ns=ns, past=past),
        out_shape=jax.ShapeDtypeStruct((b, tq, N_HEADS * HEAD_DIM), F32),
        grid_spec=pltpu.PrefetchScalarGridSpec(
            num_scalar_prefetch=1, grid=(b,),
            in_specs=[perb(q), perb(gates), perb(kc), perb(vc), pl.BlockSpec(memory_space=pl.ANY), perb(kvs_new),
                      perb(win_t), perb(kvw_new), full(slope), full(frac_t), full(e_gate)],
            out_specs=pl.BlockSpec((1, tq, N_HEADS * HEAD_DIM), lambda i, pt: (i, 0, 0)),
            scratch_shapes=[pltpu.VMEM((2, n_pages, width, page), F32), pltpu.VMEM((tk, width), F32),
                            pltpu.SemaphoreType.DMA((2,))]),
        compiler_params=_cp(("arbitrary",)),
        name="nsa_sample",
    )(page_table, q, gates, kc, vc, cache_t, kvs_new, win_t, kvw_new, slope, frac_t, e_gate)


def _pool_diff_body(ext_ref, d_ref, *, t, pos0):
    gw = ext_ref.shape[2] // len(POOL_SIZES)
    pos = pos0 + lax.broadcasted_iota(I32, (t, gw), 0)
    for gi, w in enumerate(POOL_SIZES):
        lanes = pl.ds(gi * gw, gw)
        cur = ext_ref[0, pl.ds(POOL_BUF, t), lanes]
        acc = cur
        for j in range(1, w):
            acc = acc + ext_ref[0, pl.ds(POOL_BUF - j, t), lanes]
        cnt = jnp.minimum(w, pos + 1).astype(F32)
        d_ref[0, :, lanes] = acc / cnt - cur


def pool_diff(ext, pos0):
    b, rows, width = ext.shape
    t = rows - POOL_BUF
    return pl.pallas_call(
        functools.partial(_pool_diff_body, t=t, pos0=pos0),
        out_shape=jax.ShapeDtypeStruct((b, t, width), F32),
        grid=(b,),
        in_specs=[pl.BlockSpec((1, rows, width), lambda i: (i, 0, 0))],
        out_specs=pl.BlockSpec((1, t, width), lambda i: (i, 0, 0)),
        compiler_params=_cp(("arbitrary",)),
        name="pool_diff",
    )(ext)


def _outproj_body(o_ref, d_ref, x_ref, gate_ref, sh_ref, sc_ref, g2_ref, wn_ref, wp_ref, wpool_ref, ps_ref,
                  x1_ref, h2_ref):
    gw = wpool_ref.shape[1]
    d = d_ref[...]
    yp = jnp.concatenate([_dot(d[:, g * gw:(g + 1) * gw].astype(BF16), wpool_ref[g]) for g in range(len(POOL_SIZES))],
                         axis=1) * ps_ref[...]
    mix = _dot(o_ref[...].astype(BF16), wn_ref[...]) + _dot(yp.astype(BF16), wp_ref[...])
    x1 = x_ref[...] + gate_ref[0] * mix
    x1_ref[...] = x1
    h2 = _rms_rows(x1) * g2_ref[...]
    h2_ref[...] = (h2 * (1.0 + sc_ref[0]) + sh_ref[0]).astype(BF16)


def outproj(o, d, x, gate, shift, scale, g2, wn, wp, wpool, ps, tm, tiles_per_mod):
    t, dm = x.shape
    full = lambda a: pl.BlockSpec(a.shape, lambda i: (0,) * a.ndim)
    row = lambda n: pl.BlockSpec((tm, n), lambda i: (i, 0))
    ms = lambda a: _mod_spec(a, tm, tiles_per_mod)
    return pl.pallas_call(
        _outproj_body,
        out_shape=(jax.ShapeDtypeStruct((t, dm), F32), jax.ShapeDtypeStruct((t, dm), BF16)),
        grid=(t // tm,),
        in_specs=[row(o.shape[1]), row(d.shape[1]), row(dm), ms(gate), ms(shift), ms(scale), full(g2),
                  full(wn), full(wp), full(wpool), full(ps)],
        out_specs=(row(dm), row(dm)),
        compiler_params=_cp(("arbitrary",)),
        name="outproj",
    )(o, d, x, gate, shift, scale, g2, wn, wp, wpool, ps)


_CAND_PIECES = ((0, 16), (1, 8), (2, 8), (3, 8), (4, 8), (5, 8), (6, 8), (7, 8))


def _top16_rows(s, exact):
    kio = lax.broadcasted_iota(I32, s.shape, 0)
    nk = s.shape[0]
    s0 = s
    rank = jnp.full(s.shape, float(PEER_TOPK), F32)
    vals = []
    for it in range(PEER_TOPK):
        m = jnp.max(s, axis=0, keepdims=True)
        hit = s == m
        if exact:
            hit = kio == jnp.min(jnp.where(hit, kio, nk), axis=0, keepdims=True)
        rank = jnp.where(hit, float(it), rank)
        s = jnp.where(hit, -jnp.inf, s)
        vals.append(m)
    cnt = jnp.sum((s0 >= vals[-1]).astype(F32), axis=0, keepdims=True)
    return vals, rank, cnt


def _route_head(s1, s2, exact):
    v1, rank1, cnt1 = _top16_rows(s1, exact)
    v2, rank2, cnt2 = _top16_rows(s2, exact)
    n = s1.shape[1]
    v2a = jnp.concatenate(v2, axis=0)
    top = v1[0] + v2[0]

    bio = {nb: lax.broadcasted_iota(I32, (nb, n), 0) for nb in (8, PEER_TOPK)}
    v2p = {8: jnp.concatenate(v2[:8], axis=0), PEER_TOPK: v2a}
    cands, flats = [], []
    for a, nb in _CAND_PIECES:
        c = v1[a] + v2p[nb]
        ok = (bio[nb] + 1) * (a + 1) <= PEER_TOPK
        cands.append(jnp.where(ok, c, -jnp.inf))
        flats.append(a * PEER_TOPK + bio[nb])
    v1b = jnp.concatenate(v1[8:], axis=0)
    cands.append(v1b + v2[0])
    flats.append((bio[8] + 8) * PEER_TOPK)
    cand = jnp.concatenate(cands, axis=0)
    flat = jnp.concatenate(flats, axis=0)
    big = PEER_TOPK * PEER_TOPK
    taken = jnp.zeros(cand.shape, jnp.bool_)
    z = jnp.zeros((1, n), F32)
    for _ in range(PEER_TOPK):
        m = jnp.max(cand, axis=0, keepdims=True)
        hit = cand == m
        if exact:
            hit = flat == jnp.min(jnp.where(hit, flat, big), axis=0, keepdims=True)
        taken = taken | hit
        cand = jnp.where(hit, -jnp.inf, cand)
        z = z + jnp.exp(m - top)
    takenf = taken.astype(F32)
    cnt3 = jnp.sum(takenf, axis=0, keepdims=True)

    lim = jnp.zeros(s1.shape, F32)
    off = 0
    for a, nb in _CAND_PIECES:
        lim = jnp.where(rank1 == float(a), jnp.sum(takenf[off:off + nb], axis=0, keepdims=True), lim)
        off += nb
    for i in range(8):
        lim = jnp.where(rank1 == float(8 + i), takenf[off + i:off + i + 1], lim)

    k = float(PEER_TOPK)
    ok = ((cnt1 == k) & (cnt2 == k) & (cnt3 == k)).astype(F32)
    return rank2.astype(BF16), lim, jnp.exp(s1 - v1[0]) / z, jnp.exp(s2 - v2[0]).astype(BF16), ok


def _peer_route_body(h_ref, wq_ref, sk_ref, rk_ref, lim_ref, e1_ref, e2_ref, q_scr):
    hd = pl.program_id(1)
    qd = sk_ref.shape[3]

    @pl.when(hd == 0)
    def _():
        q = _dot(h_ref[...], wq_ref[...]).astype(BF16)
        for i in range(PEER_HEADS):
            q_scr[i] = q[:, i * 2 * qd:(i + 1) * 2 * qd]

    qh = q_scr[hd]
    s1 = _nt(sk_ref[0, 0], qh[:, :qd])
    s2 = _nt(sk_ref[1, 0], qh[:, qd:])

    def emit(exact):
        rk, lim, e1, e2, ok = _route_head(s1, s2, exact)
        rk_ref[0], lim_ref[0], e1_ref[0], e2_ref[0] = rk, lim, e1, e2
        return ok

    ok = emit(False)

    @pl.when(jnp.min(ok) < 0.5)
    def _():
        emit(True)


def peer_route(h2, wq, sk, tt):
    t, d = h2.shape
    nk = sk.shape[2]
    out = jax.ShapeDtypeStruct((PEER_HEADS, nk, t), F32)
    outb = jax.ShapeDtypeStruct((PEER_HEADS, nk, t), BF16)
    ospec = pl.BlockSpec((1, nk, tt), lambda i, h: (h, 0, i))
    return pl.pallas_call(
        _peer_route_body,
        out_shape=(outb, out, out, outb),
        grid=(t // tt, PEER_HEADS),
        in_specs=[pl.BlockSpec((tt, d), lambda i, h: (i, 0)),
                  pl.BlockSpec(wq.shape, lambda i, h: (0, 0)),
                  pl.BlockSpec((2, 1) + sk.shape[2:], lambda i, h: (0, h, 0, 0))],
        out_specs=(ospec, ospec, ospec, ospec),
        scratch_shapes=[pltpu.VMEM((PEER_HEADS, tt, wq.shape[1] // PEER_HEADS), BF16)],
        compiler_params=_cp(("arbitrary", "arbitrary")),
        name="peer_route",
    )(h2, wq, sk)


def _peer_dense_body(h_ref, u_ref, v_ref, rk_ref, lim_ref, e1_ref, e2_ref, x1_ref, gate_ref, y_ref, w_scr, lb_scr,
                     eb_scr):
    j = pl.program_id(1)
    nk = rk_ref.shape[1]
    et, tt = u_ref.shape[0], h_ref.shape[0]
    rows = lb_scr.shape[2]

    @pl.when(j == 0)
    def _():
        y_ref[...] = jnp.zeros_like(y_ref)

    zero = jnp.zeros((), BF16)
    for rr in range(et // nk):
        r = j * (et // nk) + rr
        for hd in range(PEER_HEADS):
            lb_scr[rr, hd] = jnp.broadcast_to(lim_ref[hd, pl.ds(r, 1), :], (rows, tt)).astype(BF16)
            eb_scr[rr, hd] = jnp.broadcast_to(e1_ref[hd, pl.ds(r, 1), :], (rows, tt)).astype(BF16)
    for rr in range(et // nk):
        for ch in range(nk // rows):
            sl = pl.ds(ch * rows, rows)
            w = jnp.zeros((rows, tt), BF16)
            for hd in range(PEER_HEADS):
                w = w + jnp.where(rk_ref[hd, sl, :] < lb_scr[rr, hd], e2_ref[hd, sl, :], zero) * eb_scr[rr, hd]
            w_scr[pl.ds(rr * nk + ch * rows, rows), :] = w
    a = _nt(u_ref[...], h_ref[...])
    act = a * (lax.erf(a * np.float32(1.0 / np.sqrt(2.0))) + 1.0) * 0.5
    wt = (act * w_scr[...].astype(F32)).astype(BF16)
    y_ref[...] += _tn(wt, v_ref[...])

    @pl.when(j == pl.num_programs(1) - 1)
    def _():
        y_ref[...] = x1_ref[...] + gate_ref[0] * y_ref[...]


def peer_dense(h2, u, v, rk, lim, e1, e2, x1, gate, tt, et, tiles_per_mod):
    t, d = h2.shape
    ne = u.shape[0]
    nk = rk.shape[1]
    once = pl.Buffered(1)
    tok = pl.BlockSpec((tt, d), lambda i, j: (i, 0), pipeline_mode=once)
    exp = pl.BlockSpec((et, d), lambda i, j: (j, 0))
    rt = pl.BlockSpec((PEER_HEADS, nk, tt), lambda i, j: (0, 0, i), pipeline_mode=once)
    if gate.shape[1] == 1:
        gspec = pl.BlockSpec((1, 1, d), lambda i, j: (i // tiles_per_mod, 0, 0), pipeline_mode=once)
    else:
        gspec = pl.BlockSpec((1, tt, d), lambda i, j: (i, 0, 0), pipeline_mode=once)
    return pl.pallas_call(
        _peer_dense_body,
        out_shape=jax.ShapeDtypeStruct((t, d), F32),
        grid=(t // tt, ne // et),
        in_specs=[tok, exp, exp, rt, rt, rt, rt, tok, gspec],
        out_specs=pl.BlockSpec((tt, d), lambda i, j: (i, 0)),
        scratch_shapes=[pltpu.VMEM((et, tt), BF16), pltpu.VMEM((et // nk, PEER_HEADS, BF16_ROWS, tt), BF16),
                        pltpu.VMEM((et // nk, PEER_HEADS, BF16_ROWS, tt), BF16)],
        compiler_params=_cp(("arbitrary", "arbitrary")),
        name="peer_dense",
    )(h2, u, v, rk, lim, e1, e2, x1, gate)


def _slope_lanes(tq):
    h = np.arange(1, N_HEADS + 1, dtype=np.float32)
    s = (2.0 ** (-8.0 * h / N_HEADS)).reshape(N_KV, GROUP)
    return jnp.asarray(np.repeat(s.T.reshape(-1), tq)[None, :], F32)


def _frac_t(nc, ns, nc_pad, ns_pad):
    pos = np.arange(nc)[:, None] * CMP_STRIDE + np.arange(CMP_BLOCK)[None, :]
    f = ((pos // SEL_BLOCK)[:, :, None] == np.arange(ns)[None, None, :]).mean(axis=1)
    out = np.zeros((ns_pad, nc_pad), np.float32)
    out[:ns, :nc] = f.T
    return jnp.asarray(out)


def _gate_expand(width):
    e = np.zeros((3, width, N_HEADS * HEAD_DIM), np.float32)
    for g in range(N_KV):
        for r in range(GROUP):
            for j in range(3):
                c0 = r * KV_W + g * HEAD_DIM
                e[j, (g * GROUP + r) * 3 + j, c0:c0 + HEAD_DIM] = 1.0
    return jnp.asarray(e, BF16)


def _rgd(a, axis):
    shp = a.shape
    a = a.reshape(shp[:axis] + (N_KV, GROUP, HEAD_DIM) + shp[axis + 1:])
    a = jnp.swapaxes(a, axis, axis + 1)
    return a.reshape(shp)


def kernel(x_prompt, x_sample, cache_kv_cmp, cache_kv_slc, cache_kv_win, state_pool, page_table, c_prompt, c_sample,
           w_ada, b_ada, g_norm1, g_norm2, w_in, w_out, g_qnorm, g_knorm, cmp_pe, w_cmp, w_pool, pool_scale,
           peer_wq, peer_subkeys, peer_u, peer_v):
    bp, tp, dm = x_prompt.shape
    bs, ts, _ = x_sample.shape
    n_pages = page_table.shape[1]
    page = cache_kv_cmp.shape[1]
    past = n_pages * page
    nsa_w = N_HEADS * HEAD_DIM
    kv3 = 3 * 2 * KV_W
    ngl = 3 * N_HEADS
    gpad = 128
    tm, tq, sel_bucket, tt, et = 256, 32, 4, 512, 1024

    wq = _rgd(w_in[:, :nsa_w], 1).astype(BF16)
    wkv = w_in[:, nsa_w:nsa_w + kv3].astype(BF16)
    wg = jnp.pad(w_in[:, nsa_w + kv3:nsa_w + kv3 + ngl], ((0, 0), (0, gpad - ngl))).astype(BF16)
    wu = w_in[:, nsa_w + kv3 + ngl:].astype(BF16)
    gq = jnp.tile(g_qnorm, N_HEADS)[None, :]
    gk = jnp.tile(g_knorm, (1, N_KV))
    g1 = g_norm1[None, :]
    g2 = g_norm2[None, :]
    wn = _rgd(w_out[:nsa_w], 0).astype(BF16)
    wp = w_out[nsa_w:].astype(BF16)
    wpool = w_pool.astype(BF16)
    ps = pool_scale[None, :]
    wc = jnp.einsum('cglde,gh->clgdhe', w_cmp, jnp.eye(N_KV, dtype=F32)).reshape(2, CMP_BLOCK, KV_W, KV_W).astype(BF16)
    pe = jnp.transpose(cmp_pe, (1, 0, 2, 3)).reshape(CMP_BLOCK, 2 * KV_W)
    e_gate = _gate_expand(gpad)
    pwq = peer_wq.astype(BF16)
    psk = peer_subkeys.astype(BF16)
    pu = peer_u.astype(BF16)
    pv = peer_v.astype(BF16)

    mod = adaln(jnp.concatenate([c_prompt, c_sample], axis=0), w_ada, b_ada).reshape(bp + bs, 6, dm)
    mod_p = [mod[:bp, k][:, None, :] for k in range(6)]
    mod_s = [jnp.repeat(mod[bp:, k], ts, axis=0).reshape(bs * ts // tm, tm, dm) for k in range(6)]

    xp = x_prompt.reshape(bp * tp, dm)
    q, kvc, kvs, kvw, gates, u, kvc_t, kvs_t, kvw_t = inproj(xp, mod_p[0], mod_p[1], g1, wq, wkv, wg, wu, gq, gk, tm,
                                                             tp // tm, seq_len=tp)
    kc, vc = compress_prompt(kvc.reshape(bp, tp // CMP_STRIDE, CMP_STRIDE * 2 * KV_W), pe, wc, gk)
    nc = (tp - CMP_BLOCK) // CMP_STRIDE + 1
    kvw3 = kvw.reshape(bp, tp, 2 * KV_W)
    o_nsa = nsa_prompt(q, gates, kc, vc, kvs.reshape(bp, tp, 2 * KV_W), jnp.pad(kvw3, ((0, 0), (WINDOW, 0), (0, 0))),
                       _slope_lanes(tq), _frac_t(nc, tp // SEL_BLOCK, tp // CMP_STRIDE, 40), e_gate, tq, sel_bucket)
    u3 = u.reshape(bp, tp, -1)
    d_pool = pool_diff(jnp.pad(u3, ((0, 0), (POOL_BUF, 0), (0, 0))), 0).reshape(bp * tp, -1)
    x1, h2 = outproj(o_nsa, d_pool, xp, mod_p[2], mod_p[3], mod_p[4], g2, wn, wp, wpool, ps, tm, tp // tm)
    routes = peer_route(h2, pwq, psk, 256)
    y_prompt = peer_dense(h2, pu, pv, *routes, x1, mod_p[5], tt, et, tp // tt).reshape(bp, tp, dm)
    rows_major = lambda a: jnp.transpose(a.reshape(bp, 2, N_KV, HEAD_DIM, -1), (0, 4, 1, 2, 3))
    win_p = rows_major(kvw_t[:, :, -min(WINDOW, tp):])
    pool_p = u3[:, -POOL_BUF:]

    xs = x_sample.reshape(bs * ts, dm)
    q, kvc_s, kvs_s, kvw_s, gates, u = inproj(xs, mod_s[0], mod_s[1], g1, wq, wkv, wg, wu, gq, gk, tm, 1)
    feat_major = lambda c: jnp.transpose(c, (0, 2, 3, 4, 1)).reshape(c.shape[0], 2 * KV_W, c.shape[1])
    kc, vc = compress_sample(page_table, feat_major(cache_kv_cmp), pe, wc, gk)
    nc = (past + ts - CMP_BLOCK) // CMP_STRIDE + 1
    ns = -(-(past + ts) // SEL_BLOCK)
    kvw_s3 = kvw_s.reshape(bs, ts, 2 * KV_W)
    o_nsa = nsa_sample(page_table, q.reshape(bs, ts, -1), gates.reshape(bs, ts, -1), kc, vc,
                       feat_major(cache_kv_slc), kvs_s.reshape(bs, ts, 2 * KV_W), feat_major(cache_kv_win), kvw_s3,
                       _slope_lanes(ts), _frac_t(nc, ns, past // CMP_STRIDE, 40), e_gate).reshape(bs * ts, -1)
    ext = jnp.concatenate([state_pool, u.reshape(bs, ts, -1)], axis=1)
    d_pool = pool_diff(ext, past).reshape(bs * ts, -1)
    x1, h2 = outproj(o_nsa, d_pool, xs, mod_s[2], mod_s[3], mod_s[4], g2, wn, wp, wpool, ps, tm, 1)
    routes = peer_route(h2, pwq, psk, 256)
    gate5 = mod_s[5].reshape(bs * ts // tt, tt, dm)
    y_sample = peer_dense(h2, pu, pv, *routes, x1, gate5, tt, et, 1).reshape(bs, ts, dm)
    shp_s = (bs, ts, 2, N_KV, HEAD_DIM)
    wbuf = cache_kv_win.shape[1]
    win_s = jnp.concatenate([cache_kv_win, kvw_s.reshape(shp_s)], axis=1)[:, -wbuf:]
    pool_s = ext[:, -POOL_BUF:]

    return (y_prompt, y_sample, rows_major(kvc_t), rows_major(kvs_t), win_p, pool_p,
            kvc_s.reshape(shp_s), kvs_s.reshape(shp_s), win_s, pool_s)
```

```python
import functools

import numpy as np
import jax
import jax.numpy as jnp
from jax import lax
from jax.experimental import pallas as pl
from jax.experimental.pallas import tpu as pltpu

F32, BF16, I32 = jnp.float32, jnp.bfloat16, jnp.int32

HEAD_DIM = 64
N_KV = 4
GROUP = 4
N_HEADS = N_KV * GROUP
CMP_BLOCK = 32
CMP_STRIDE = 16
SEL_BLOCK = 64
SEL_TOP = 16
WINDOW = 512
POOL_SIZES = (2, 4, 8, 16)
POOL_BUF = max(POOL_SIZES) - 1
PEER_HEADS = 8
N_KEYS = 128
PEER_TOPK = 16
EPS = 1e-6
NEG = -1e30
FORCE_BONUS = 1e4
KV_W = N_KV * HEAD_DIM
VMEM_LIMIT = 56 * 1024 * 1024


def _cp(sem, vmem=VMEM_LIMIT):
    return pltpu.CompilerParams(dimension_semantics=sem, vmem_limit_bytes=vmem)


def _dot(a, b):
    return jnp.dot(a, b, preferred_element_type=F32)


def _nt(a, b):
    return lax.dot_general(a, b, (((1,), (1,)), ((), ())), preferred_element_type=F32)


def _tn(a, b):
    return lax.dot_general(a, b, (((0,), (0,)), ((), ())), preferred_element_type=F32)


def _split_dot(x, w):
    hi = x.astype(BF16)
    r1 = x - hi.astype(F32)
    mid = r1.astype(BF16)
    lo = (r1 - mid.astype(F32)).astype(BF16)
    return _dot(hi, w) + _dot(mid, w) + _dot(lo, w)


def _ones64():
    r = lax.broadcasted_iota(I32, (KV_W, KV_W), 0) // HEAD_DIM
    c = lax.broadcasted_iota(I32, (KV_W, KV_W), 1) // HEAD_DIM
    return (r == c).astype(BF16)


def _rms64(x, ones):
    ms = _split_dot(x * x, ones) * (1.0 / HEAD_DIM)
    return x * lax.rsqrt(ms + EPS)


def _rms_rows(x):
    return x * lax.rsqrt(jnp.mean(x * x, axis=-1, keepdims=True) + EPS)


def _adaln_body(c_ref, w_ref, b_ref, o_ref):
    c = c_ref[...]
    s = (c * jax.nn.sigmoid(c)).astype(BF16)
    o_ref[...] = _dot(s, w_ref[...].astype(BF16)) + b_ref[...]


def adaln(c_all, w_ada, b_ada, tn=1024):
    m, d = c_all.shape
    n = w_ada.shape[1]
    return pl.pallas_call(
        _adaln_body,
        out_shape=jax.ShapeDtypeStruct((m, n), F32),
        grid=(n // tn,),
        in_specs=[pl.BlockSpec((m, d), lambda j: (0, 0)),
                  pl.BlockSpec((d, tn), lambda j: (0, j)),
                  pl.BlockSpec((1, tn), lambda j: (0, j))],
        out_specs=pl.BlockSpec((m, tn), lambda j: (0, j)),
        compiler_params=_cp(("arbitrary",)),
        name="adaln",
    )(c_all, w_ada, b_ada.reshape(1, n))


def _inproj_body(x_ref, sh_ref, sc_ref, g1_ref, wq_ref, wkv_ref, wg_ref, wu_ref, gq_ref, gk_ref,
                 q_ref, kvc_ref, kvs_ref, kvw_ref, gt_ref, u_ref, *t_refs):
    h = _rms_rows(x_ref[...]) * g1_ref[...]
    h = h * (1.0 + sc_ref[0]) + sh_ref[0]
    hb = h.astype(BF16)
    ones = _ones64()
    zq = _dot(hb, wq_ref[...])
    for r in range(GROUP):
        sl = slice(r * KV_W, (r + 1) * KV_W)
        q_ref[:, sl] = _rms64(zq[:, sl], ones) * gq_ref[:, sl] * (HEAD_DIM ** -0.5)
    zkv = _dot(hb, wkv_ref[...])
    kvc_ref[...] = zkv[:, 0:2 * KV_W]
    kvs_ref[:, 0:KV_W] = _rms64(zkv[:, 2 * KV_W:3 * KV_W], ones) * gk_ref[1:2, :]
    kvs_ref[:, KV_W:] = zkv[:, 3 * KV_W:4 * KV_W]
    kvw_ref[:, 0:KV_W] = _rms64(zkv[:, 4 * KV_W:5 * KV_W], ones) * gk_ref[2:3, :]
    kvw_ref[:, KV_W:] = zkv[:, 5 * KV_W:6 * KV_W]
    gt_ref[...] = jax.nn.sigmoid(_dot(hb, wg_ref[...]))
    u_ref[...] = _dot(hb, wu_ref[...])
    for src, dst in zip((kvc_ref, kvs_ref, kvw_ref), t_refs):
        dst[0] = src[...].T


def _mod_spec(arr, tm, rows_per_mod):
    d = arr.shape[-1]
    if arr.shape[1] == 1:
        return pl.BlockSpec((1, 1, d), lambda i: (i // rows_per_mod, 0, 0))
    return pl.BlockSpec((1, tm, d), lambda i: (i, 0, 0))


def inproj(x, shift, scale, g1, wq, wkv, wg, wu, gq, gk, tm, tiles_per_mod, seq_len=None):
    t, d = x.shape
    nq, nkv, ng, nu = wq.shape[1], wkv.shape[1], wg.shape[1], wu.shape[1]
    full = lambda a: pl.BlockSpec(a.shape, lambda i: (0,) * a.ndim)
    row = lambda n: pl.BlockSpec((tm, n), lambda i: (i, 0))
    kvrow = jax.ShapeDtypeStruct((t, 2 * KV_W), F32)
    out_shape = [jax.ShapeDtypeStruct((t, nq), F32), kvrow, kvrow, kvrow,
                 jax.ShapeDtypeStruct((t, ng), F32), jax.ShapeDtypeStruct((t, nu), F32)]
    out_specs = [row(nq), row(2 * KV_W), row(2 * KV_W), row(2 * KV_W), row(ng), row(nu)]
    if seq_len is not None:
        tiles = seq_len // tm
        out_shape += [jax.ShapeDtypeStruct((t // seq_len, 2 * KV_W, seq_len), F32)] * 3
        out_specs += [pl.BlockSpec((1, 2 * KV_W, tm), lambda i: (i // tiles, 0, i % tiles))] * 3
    return pl.pallas_call(
        _inproj_body,
        out_shape=tuple(out_shape),
        grid=(t // tm,),
        in_specs=[row(d), _mod_spec(shift, tm, tiles_per_mod), _mod_spec(scale, tm, tiles_per_mod), full(g1),
                  full(wq), full(wkv), full(wg), full(wu), full(gq), full(gk)],
        out_specs=tuple(out_specs),
        compiler_params=_cp(("arbitrary",)),
        name="inproj",
    )(x, shift, scale, g1, wq, wkv, wg, wu, gq, gk)


def _compress(get_x, nchunk, pe_ref, w_ref, gk_ref, kc_ref, vc_ref):
    half = CMP_BLOCK // 2
    acc = [jnp.zeros((nchunk, KV_W), F32) for _ in range(4)]
    for l in range(half):
        xl = get_x(l)
        a = (xl + pe_ref[l:l + 1, :]).astype(BF16)
        b = (xl + pe_ref[half + l:half + l + 1, :]).astype(BF16)
        acc[0] += _dot(a[:, :KV_W], w_ref[0, l])
        acc[1] += _dot(a[:, KV_W:], w_ref[1, l])
        acc[2] += _dot(b[:, :KV_W], w_ref[0, half + l])
        acc[3] += _dot(b[:, KV_W:], w_ref[1, half + l])
    rio = lax.broadcasted_iota(I32, (nchunk, KV_W), 0)
    nxt = lambda v: jnp.where(rio < nchunk - 1, pltpu.roll(v, nchunk - 1, 0), 0.0)
    ck = acc[0] + nxt(acc[2])
    cv = acc[1] + nxt(acc[3])
    kc_ref[...] = _rms64(ck, _ones64()) * gk_ref[0:1, :]
    vc_ref[...] = cv


def _compress_prompt_body(x_ref, pe_ref, w_ref, gk_ref, kc_ref, vc_ref):
    row_w = 2 * KV_W
    _compress(lambda l: x_ref[0, :, l * row_w:(l + 1) * row_w], x_ref.shape[1], pe_ref, w_ref, gk_ref,
              kc_ref.at[0], vc_ref.at[0])


def compress_prompt(kv_chunks, pe, wc, gk):
    b, nchunk, width = kv_chunks.shape
    full = lambda a: pl.BlockSpec(a.shape, lambda i: (0,) * a.ndim)
    out = jax.ShapeDtypeStruct((b, nchunk, KV_W), F32)
    ospec = pl.BlockSpec((1, nchunk, KV_W), lambda i: (i, 0, 0))
    return pl.pallas_call(
        _compress_prompt_body,
        out_shape=(out, out),
        grid=(b,),
        in_specs=[pl.BlockSpec((1, nchunk, width), lambda i: (i, 0, 0)), full(pe), full(wc), full(gk)],
        out_specs=(ospec, ospec),
        compiler_params=_cp(("arbitrary",)),
        name="compress_prompt",
    )(kv_chunks, pe, wc, gk)


def _page_copy(cache_ref, buf_ref, sem_ref, pt_ref, seq, slot, j):
    return pltpu.make_async_copy(cache_ref.at[pt_ref[seq, j]], buf_ref.at[slot, j], sem_ref.at[slot])


def _gather_pages(cache_ref, buf_ref, sem_ref, pt_ref):
    b = pl.program_id(0)
    nb = pl.num_programs(0)
    n_pages = pt_ref.shape[1]
    slot = b % 2

    def start(seq, s):
        for j in range(n_pages):
            _page_copy(cache_ref, buf_ref, sem_ref, pt_ref, seq, s, j).start()

    @pl.when(b == 0)
    def _():
        start(b, slot)

    @pl.when(b + 1 < nb)
    def _():
        start(b + 1, 1 - slot)

    for j in range(n_pages):
        _page_copy(cache_ref, buf_ref, sem_ref, pt_ref, b, slot, j).wait()
    return slot


LANES = 128
BF16_ROWS = 16


def _compress_sample_body(pt_ref, cache_ref, pe_ref, w_ref, gk_ref, kc_ref, vc_ref, buf_ref, xs_ref, sem_ref):
    n_pages, page = pt_ref.shape[1], cache_ref.shape[2]
    slot = _gather_pages(cache_ref, buf_ref, sem_ref, pt_ref)
    nlb = xs_ref.shape[0]
    for j in range(n_pages):
        for k in range(nlb):
            xs_ref[k, pl.ds(j * page, page), :] = buf_ref[slot, j, pl.ds(k * LANES, LANES), :].T
    nchunk = n_pages * page // CMP_STRIDE
    get_x = lambda l: jnp.concatenate([xs_ref[k, pl.ds(l, nchunk, stride=CMP_STRIDE), :] for k in range(nlb)], axis=1)
    _compress(get_x, nchunk, pe_ref, w_ref, gk_ref, kc_ref.at[0], vc_ref.at[0])


def compress_sample(page_table, cache_t, pe, wc, gk):
    b, n_pages = page_table.shape
    _, width, page = cache_t.shape
    nchunk = n_pages * page // CMP_STRIDE
    full = lambda a: pl.BlockSpec(a.shape, lambda i, pt: (0,) * a.ndim)
    out = jax.ShapeDtypeStruct((b, nchunk, KV_W), F32)
    ospec = pl.BlockSpec((1, nchunk, KV_W), lambda i, pt: (i, 0, 0))
    return pl.pallas_call(
        _compress_sample_body,
        out_shape=(out, out),
        grid_spec=pltpu.PrefetchScalarGridSpec(
            num_scalar_prefetch=1, grid=(b,),
            in_specs=[pl.BlockSpec(memory_space=pl.ANY), full(pe), full(wc), full(gk)],
            out_specs=(ospec, ospec),
            scratch_shapes=[pltpu.VMEM((2, n_pages, width, page), F32),
                            pltpu.VMEM((width // LANES, n_pages * page, LANES), F32),
                            pltpu.SemaphoreType.DMA((2,))]),
        compiler_params=_cp(("arbitrary",)),
        name="compress_sample",
    )(page_table, cache_t, pe, wc, gk)


def _softmax_cols(s, valid):
    s = jnp.where(valid, s, NEG)
    m = jnp.max(s, axis=0, keepdims=True)
    e = jnp.where(valid, jnp.exp(s - m), 0.0)
    return e / jnp.maximum(jnp.sum(e, axis=0, keepdims=True), 1e-30)


def _nsa_front(q, kc, vc, pq0, slope, frac_t, tq, ns):
    cols = N_HEADS * tq
    lane = lax.broadcasted_iota(I32, (1, cols), 1)
    pq = (pq0 + lane % tq).astype(F32)
    lg = lax.broadcasted_iota(I32, (tq, KV_W), 1) // HEAD_DIM
    qfull = jnp.concatenate([jnp.where(lg == g, q[:, r * KV_W:(r + 1) * KV_W], 0.0)
                             for r in range(GROUP) for g in range(N_KV)], axis=0).astype(BF16)

    nc = kc.shape[0]
    pos_c = (lax.broadcasted_iota(I32, (nc, cols), 0) * CMP_STRIDE + (CMP_BLOCK - 1)).astype(F32)
    dist_c = pq - pos_c
    p_c = _softmax_cols(_nt(kc.astype(BF16), qfull) - slope * dist_c, dist_c >= 0).astype(BF16)
    o_c = _tn(p_c, vc.astype(BF16))

    imp = _dot(frac_t.astype(BF16), p_c)
    imp = imp + pltpu.roll(imp, 4 * tq, 1) + pltpu.roll(imp, 8 * tq, 1) + pltpu.roll(imp, 12 * tq, 1)
    gq = N_KV * tq
    compact = gq % LANES == 0
    wsel = gq if compact else cols
    imp = imp[:, :wsel]
    nsp = imp.shape[0]
    jio = lax.broadcasted_iota(I32, (nsp, wsel), 0)
    pos_q = pq0 + lax.broadcasted_iota(I32, (1, wsel), 1) % tq
    blk_q = pos_q // SEL_BLOCK
    forced = (jio == 0) | (jio == blk_q) | (jio == blk_q - 1)
    valid_blk = jio * SEL_BLOCK <= pos_q
    imp = jnp.where(valid_blk, imp + FORCE_BONUS * forced.astype(F32), NEG)
    rank = jnp.zeros((nsp, wsel), I32)
    for i in range(ns):
        row = imp[i:i + 1, :]
        rank = rank + ((row > imp) | ((row == imp) & (i < jio))).astype(I32)
    sel_bias = jnp.where(rank < min(SEL_TOP, ns), 0.0, NEG)
    if compact:
        sel_bias = jnp.concatenate([sel_bias] * GROUP, axis=1)
    return qfull, pq, o_c, sel_bias


def _as_column(row):
    return jnp.transpose(jnp.broadcast_to(row, (LANES, row.shape[1])))[:, 0:1]


def _nsa_selected(qfull, pq, sel_bias, ks_ref, slope, nblk, blk_q):
    cols = qfull.shape[0]
    nsp = sel_bias.shape[0]
    tk = nblk * SEL_BLOCK
    jio = lax.broadcasted_iota(I32, (nsp, cols), 0)
    bias = jnp.where(jio < blk_q, sel_bias, NEG) + slope * (jio * SEL_BLOCK).astype(F32)
    inblk = slope * lax.broadcasted_iota(I32, (SEL_BLOCK, cols), 0).astype(F32)
    s_m = _nt(ks_ref[0:tk, 0:KV_W].astype(BF16), qfull).reshape(nblk, SEL_BLOCK, cols)
    s_m = s_m + inblk[None, :, :] + bias[:nblk][:, None, :]

    d0 = blk_q * SEL_BLOCK
    kd = ks_ref[pl.ds(d0 if isinstance(d0, int) else pl.multiple_of(d0, SEL_BLOCK), SEL_BLOCK), :]
    kpos_d = (blk_q * SEL_BLOCK + lax.broadcasted_iota(I32, (SEL_BLOCK, cols), 0)).astype(F32)
    valid_d = kpos_d <= pq
    s_d = jnp.where(valid_d, _nt(kd[:, :KV_W].astype(BF16), qfull) + slope * kpos_d, NEG)

    m = jnp.maximum(jnp.max(jnp.max(s_m, axis=0), axis=0, keepdims=True), jnp.max(s_d, axis=0, keepdims=True))
    e_m = jnp.exp(s_m - m[None, :, :])
    e_d = jnp.where(valid_d, jnp.exp(s_d - m), 0.0)
    l = jnp.sum(jnp.sum(e_m, axis=0), axis=0, keepdims=True) + jnp.sum(e_d, axis=0, keepdims=True)
    o = _tn(e_m.astype(BF16).reshape(tk, cols), ks_ref[0:tk, KV_W:].astype(BF16)) + _tn(e_d.astype(BF16),
                                                                                       kd[:, KV_W:].astype(BF16))
    return o * _as_column(1.0 / jnp.maximum(l, 1e-30))


def _nsa_window(qfull, pq, kw, vw, pw0, slope):
    tw, cols = kw.shape[0], qfull.shape[0]
    pos_w = (pw0 + lax.broadcasted_iota(I32, (tw, cols), 0)).astype(F32)
    dist_w = pq - pos_w
    valid_w = (dist_w >= 0) & (dist_w <= WINDOW) & (pos_w >= 0)
    p_w = _softmax_cols(_nt(kw.astype(BF16), qfull) - slope * dist_w, valid_w).astype(BF16)
    return _tn(p_w, vw.astype(BF16))


def _nsa_window_steady(qfull, kw, vw, slope, tq):
    cols = qfull.shape[0]
    nb = WINDOW // tq
    qq = lax.broadcasted_iota(I32, (1, cols), 1) % tq
    rio = lax.broadcasted_iota(I32, (tq, cols), 0)
    s = _nt(kw.astype(BF16), qfull).reshape(nb + 1, tq, cols) + (slope * rio.astype(F32))[None, :, :]
    s = s + (slope * (lax.broadcasted_iota(I32, (nb + 1, cols), 0) * tq).astype(F32))[:, None, :]
    ok_f, ok_l = rio >= qq, rio <= qq
    s_f, s_m, s_l = jnp.where(ok_f, s[0], NEG), s[1:nb], jnp.where(ok_l, s[nb], NEG)
    m = jnp.maximum(jnp.max(jnp.max(s_m, axis=0), axis=0, keepdims=True),
                    jnp.max(jnp.maximum(s_f, s_l), axis=0, keepdims=True))
    e_f = jnp.where(ok_f, jnp.exp(s_f - m), 0.0)
    e_l = jnp.where(ok_l, jnp.exp(s_l - m), 0.0)
    e_m = jnp.exp(s_m - m[None, :, :])
    l = jnp.sum(jnp.sum(e_m, axis=0), axis=0, keepdims=True) + jnp.sum(e_f + e_l, axis=0, keepdims=True)
    v = vw.astype(BF16)
    o = (_tn(e_m.astype(BF16).reshape((nb - 1) * tq, cols), v[tq:nb * tq]) + _tn(e_f.astype(BF16), v[0:tq])
         + _tn(e_l.astype(BF16), v[nb * tq:]))
    return o * _as_column(1.0 / jnp.maximum(l, 1e-30))


def _nsa_combine(gates, e_gate, outs, tq):
    cols = N_HEADS * tq
    row_g = (lax.broadcasted_iota(I32, (cols, KV_W), 0) // tq) % N_KV
    own = row_g == lax.broadcasted_iota(I32, (cols, KV_W), 1) // HEAD_DIM

    def fold(o):
        o = jnp.where(own, o, 0.0)
        return jnp.concatenate(
            [sum(o[(r * N_KV + g) * tq:(r * N_KV + g + 1) * tq, :] for g in range(N_KV)) for r in range(GROUP)], axis=1)

    out = jnp.zeros((tq, N_HEADS * HEAD_DIM), F32)
    for j, o in enumerate(outs):
        out = out + _split_dot(gates, e_gate[j]) * fold(o)
    return out


def _nsa_prompt_body(q_ref, gt_ref, kc_ref, vc_ref, ks_ref, kw_ref, sl_ref, fr_ref, eg_ref, o_ref, os_ref, ow_ref,
                     *, tq, ns, tw, bucket):
    q0 = pl.program_id(1) * tq
    slope = sl_ref[...]
    qfull, pq, o_c, sel_bias = _nsa_front(q_ref[...], kc_ref[0], vc_ref[0], q0, slope, fr_ref[...], tq, ns)
    blk_q = q0 // SEL_BLOCK
    for k in range(ns // bucket):
        @pl.when(blk_q // bucket == k)
        def _():
            os_ref[...] = _nsa_selected(qfull, pq, sel_bias, ks_ref.at[0], slope, (k + 1) * bucket, blk_q)
    w = kw_ref[0, pl.ds(pl.multiple_of(q0, 8), tw), :]

    @pl.when(q0 >= WINDOW)
    def _():
        ow_ref[...] = _nsa_window_steady(qfull, w[:, :KV_W], w[:, KV_W:], slope, tq)

    @pl.when(q0 < WINDOW)
    def _():
        ow_ref[...] = _nsa_window(qfull, pq, w[:, :KV_W], w[:, KV_W:], q0 - WINDOW, slope)

    o_ref[...] = _nsa_combine(gt_ref[...], eg_ref, (o_c, os_ref[...], ow_ref[...]), tq)


def nsa_prompt(q, gates, kc, vc, kvs, kvw_pad, slope, frac_t, e_gate, tq, bucket):
    b, t, _ = kvs.shape
    ns = t // SEL_BLOCK
    assert SEL_BLOCK % tq == 0 and ns % bucket == 0 and t % SEL_BLOCK == 0
    assert WINDOW % tq == 0 and tq % 8 == 0
    tw = WINDOW + tq
    nq = t // tq
    full = lambda a: pl.BlockSpec(a.shape, lambda i, j: (0,) * a.ndim)
    perb = lambda a: pl.BlockSpec((1,) + a.shape[1:], lambda i, j: (i, 0, 0))
    row = lambda n: pl.BlockSpec((tq, n), lambda i, j: (i * nq + j, 0))
    return pl.pallas_call(
        functools.partial(_nsa_prompt_body, tq=tq, ns=ns, tw=tw, bucket=bucket),
        out_shape=jax.ShapeDtypeStruct((b * t, N_HEADS * HEAD_DIM), F32),
        grid=(b, nq),
        in_specs=[row(q.shape[1]), row(gates.shape[1]), perb(kc), perb(vc), perb(kvs), perb(kvw_pad),
                  full(slope), full(frac_t), full(e_gate)],
        out_specs=row(N_HEADS * HEAD_DIM),
        scratch_shapes=[pltpu.VMEM((N_HEADS * tq, KV_W), F32), pltpu.VMEM((N_HEADS * tq, KV_W), F32)],
        compiler_params=_cp(("arbitrary", "arbitrary")),
        name="nsa_prompt",
    )(q, gates, kc, vc, kvs, kvw_pad, slope, frac_t, e_gate)


def _nsa_sample_body(pt_ref, q_ref, gt_ref, kc_ref, vc_ref, cache_ref, knew_ref, win_ref, wnew_ref, sl_ref, fr_ref,
                     eg_ref, o_ref, buf_ref, ks_ref, sem_ref, *, tq, ns, past):
    n_pages, page = pt_ref.shape[1], cache_ref.shape[2]
    tk = ns * SEL_BLOCK

    @pl.when(pl.program_id(0) == 0)
    def _():
        ks_ref[past:, :] = jnp.zeros((tk - past, ks_ref.shape[1]), F32)

    slot = _gather_pages(cache_ref, buf_ref, sem_ref, pt_ref)
    for j in range(n_pages):
        ks_ref[pl.ds(j * page, page), :] = buf_ref[slot, j].T
    ks_ref[past:past + tq, :] = knew_ref[0]
    wbuf = win_ref.shape[2]
    tw = wbuf + SEL_BLOCK
    w = jnp.concatenate([win_ref[0].T, wnew_ref[0], jnp.zeros((tw - wbuf - tq, 2 * KV_W), F32)], axis=0)
    slope = sl_ref[...]
    qfull, pq, o_c, sel_bias = _nsa_front(q_ref[0], kc_ref[0], vc_ref[0], past, slope, fr_ref[...], tq, ns)
    o_s = _nsa_selected(qfull, pq, sel_bias, ks_ref, slope, past // SEL_BLOCK, past // SEL_BLOCK)
    o_w = _nsa_window(qfull, pq, w[:, :KV_W], w[:, KV_W:], past - wbuf, slope)
    o_ref[0] = _nsa_combine(gt_ref[0], eg_ref, (o_c, o_s, o_w), tq)


def nsa_sample(page_table, q, gates, kc, vc, cache_t, kvs_new, win_t, kvw_new, slope, frac_t, e_gate):
    b, n_pages = page_table.shape
    _, width, page = cache_t.shape
    past = n_pages * page
    tq = q.shape[1]
    assert past % SEL_BLOCK + tq <= SEL_BLOCK
    ns = -(-(past + tq) // SEL_BLOCK)
    tk = ns * SEL_BLOCK
    full = lambda a: pl.BlockSpec(a.shape, lambda i, pt: (0,) * a.ndim)
    perb = lambda a: pl.BlockSpec((1,) + a.shape[1:], lambda i, pt: (i, 0, 0))
    return pl.pallas_call(
        functools.partial(_nsa_sample_body, tq=tq, ns=ns, past=past),
        out_shape=jax.ShapeDtypeStruct((b, tq, N_HEADS * HEAD_DIM), F32),
        grid_spec=pltpu.PrefetchScalarGridSpec(
            num_scalar_prefetch=1, grid=(b,),
            in_specs=[perb(q), perb(gates), perb(kc), perb(vc), pl.BlockSpec(memory_space=pl.ANY), perb(kvs_new),
                      perb(win_t), perb(kvw_new), full(slope), full(frac_t), full(e_gate)],
            out_specs=pl.BlockSpec((1, tq, N_HEADS * HEAD_DIM), lambda i, pt: (i, 0, 0)),
            scratch_shapes=[pltpu.VMEM((2, n_pages, width, page), F32), pltpu.VMEM((tk, width), F32),
                            pltpu.SemaphoreType.DMA((2,))]),
        compiler_params=_cp(("arbitrary",)),
        name="nsa_sample",
    )(page_table, q, gates, kc, vc, cache_t, kvs_new, win_t, kvw_new, slope, frac_t, e_gate)


def _pool_diff_body(ext_ref, d_ref, *, t, pos0):
    gw = ext_ref.shape[2] // len(POOL_SIZES)
    pos = pos0 + lax.broadcasted_iota(I32, (t, gw), 0)
    for s in range(ext_ref.shape[0]):
        for gi, w in enumerate(POOL_SIZES):
            lanes = pl.ds(gi * gw, gw)
            cur = ext_ref[s, pl.ds(POOL_BUF, t), lanes]
            acc = cur
            for j in range(1, w):
                acc = acc + ext_ref[s, pl.ds(POOL_BUF - j, t), lanes]
            cnt = jnp.minimum(w, pos + 1).astype(F32)
            d_ref[s, :, lanes] = acc / cnt - cur


def pool_diff(ext, pos0, bb=1):
    b, rows, width = ext.shape
    t = rows - POOL_BUF
    return pl.pallas_call(
        functools.partial(_pool_diff_body, t=t, pos0=pos0),
        out_shape=jax.ShapeDtypeStruct((b, t, width), F32),
        grid=(b // bb,),
        in_specs=[pl.BlockSpec((bb, rows, width), lambda i: (i, 0, 0))],
        out_specs=pl.BlockSpec((bb, t, width), lambda i: (i, 0, 0)),
        compiler_params=_cp(("arbitrary",)),
        name="pool_diff",
    )(ext)


def _outproj_body(o_ref, d_ref, x_ref, gate_ref, sh_ref, sc_ref, g2_ref, wn_ref, wp_ref, wpool_ref, ps_ref,
                  x1_ref, h2_ref):
    gw = wpool_ref.shape[1]
    d = d_ref[...]
    yp = jnp.concatenate([_dot(d[:, g * gw:(g + 1) * gw].astype(BF16), wpool_ref[g]) for g in range(len(POOL_SIZES))],
                         axis=1) * ps_ref[...]
    mix = _dot(o_ref[...].astype(BF16), wn_ref[...]) + _dot(yp.astype(BF16), wp_ref[...])
    x1 = x_ref[...] + gate_ref[0] * mix
    x1_ref[...] = x1
    h2 = _rms_rows(x1) * g2_ref[...]
    h2_ref[...] = (h2 * (1.0 + sc_ref[0]) + sh_ref[0]).astype(BF16)


def outproj(o, d, x, gate, shift, scale, g2, wn, wp, wpool, ps, tm, tiles_per_mod):
    t, dm = x.shape
    full = lambda a: pl.BlockSpec(a.shape, lambda i: (0,) * a.ndim)
    row = lambda n: pl.BlockSpec((tm, n), lambda i: (i, 0))
    ms = lambda a: _mod_spec(a, tm, tiles_per_mod)
    return pl.pallas_call(
        _outproj_body,
        out_shape=(jax.ShapeDtypeStruct((t, dm), F32), jax.ShapeDtypeStruct((t, dm), BF16)),
        grid=(t // tm,),
        in_specs=[row(o.shape[1]), row(d.shape[1]), row(dm), ms(gate), ms(shift), ms(scale), full(g2),
                  full(wn), full(wp), full(wpool), full(ps)],
        out_specs=(row(dm), row(dm)),
        compiler_params=_cp(("arbitrary",)),
        name="outproj",
    )(o, d, x, gate, shift, scale, g2, wn, wp, wpool, ps)


_CAND_PIECES = ((0, 16), (1, 8), (2, 8), (3, 8), (4, 8), (5, 8), (6, 8), (7, 8))


def _top16_rows(s, exact):
    kio = lax.broadcasted_iota(I32, s.shape, 0)
    nk = s.shape[0]
    s0 = s
    rank = jnp.full(s.shape, float(PEER_TOPK), F32)
    vals = []
    for it in range(PEER_TOPK):
        m = jnp.max(s, axis=0, keepdims=True)
        hit = s == m
        if exact:
            hit = kio == jnp.min(jnp.where(hit, kio, nk), axis=0, keepdims=True)
        rank = jnp.where(hit, float(it), rank)
        s = jnp.where(hit, -jnp.inf, s)
        vals.append(m)
    cnt = jnp.sum((s0 >= vals[-1]).astype(F32), axis=0, keepdims=True)
    return vals, rank, cnt


def _route_head(s1, s2, exact):
    v1, rank1, cnt1 = _top16_rows(s1, exact)
    v2, rank2, cnt2 = _top16_rows(s2, exact)
    n = s1.shape[1]
    v2a = jnp.concatenate(v2, axis=0)
    top = v1[0] + v2[0]

    bio = {nb: lax.broadcasted_iota(I32, (nb, n), 0) for nb in (8, PEER_TOPK)}
    v2p = {8: jnp.concatenate(v2[:8], axis=0), PEER_TOPK: v2a}
    cands, flats = [], []
    for a, nb in _CAND_PIECES:
        c = v1[a] + v2p[nb]
        ok = (bio[nb] + 1) * (a + 1) <= PEER_TOPK
        cands.append(jnp.where(ok, c, -jnp.inf))
        flats.append(a * PEER_TOPK + bio[nb])
    v1b = jnp.concatenate(v1[8:], axis=0)
    cands.append(v1b + v2[0])
    flats.append((bio[8] + 8) * PEER_TOPK)
    cand = jnp.concatenate(cands, axis=0)
    flat = jnp.concatenate(flats, axis=0)
    big = PEER_TOPK * PEER_TOPK
    taken = jnp.zeros(cand.shape, jnp.bool_)
    z = jnp.zeros((1, n), F32)
    for _ in range(PEER_TOPK):
        m = jnp.max(cand, axis=0, keepdims=True)
        hit = cand == m
        if exact:
            hit = flat == jnp.min(jnp.where(hit, flat, big), axis=0, keepdims=True)
        taken = taken | hit
        cand = jnp.where(hit, -jnp.inf, cand)
        z = z + jnp.exp(m - top)
    takenf = taken.astype(F32)
    cnt3 = jnp.sum(takenf, axis=0, keepdims=True)

    lim = jnp.zeros(s1.shape, F32)
    off = 0
    for a, nb in _CAND_PIECES:
        lim = jnp.where(rank1 == float(a), jnp.sum(takenf[off:off + nb], axis=0, keepdims=True), lim)
        off += nb
    for i in range(8):
        lim = jnp.where(rank1 == float(8 + i), takenf[off + i:off + i + 1], lim)

    k = float(PEER_TOPK)
    ok = ((cnt1 == k) & (cnt2 == k) & (cnt3 == k)).astype(F32)
    return rank2.astype(BF16), lim, jnp.exp(s1 - v1[0]) / z, jnp.exp(s2 - v2[0]).astype(BF16), ok


def _peer_route_body(h_ref, wq_ref, sk_ref, rk_ref, lim_ref, e1_ref, e2_ref, q_scr):
    step = pl.program_id(1)
    hps, qd = sk_ref.shape[1], sk_ref.shape[3]

    @pl.when(step == 0)
    def _():
        q = _dot(h_ref[...], wq_ref[...]).astype(BF16)
        for i in range(PEER_HEADS):
            q_scr[i] = q[:, i * 2 * qd:(i + 1) * 2 * qd]

    def emit(hh, s1, s2, exact):
        rk, lim, e1, e2, ok = _route_head(s1, s2, exact)
        rk_ref[hh], lim_ref[hh], e1_ref[hh], e2_ref[hh] = rk, lim, e1, e2
        return ok

    heads = []
    for hh in range(hps):
        qh = q_scr[step * hps + hh]
        s1 = _nt(sk_ref[0, hh], qh[:, :qd])
        s2 = _nt(sk_ref[1, hh], qh[:, qd:])
        heads.append((hh, s1, s2, emit(hh, s1, s2, False)))
    for hh, s1, s2, ok in heads:
        @pl.when(jnp.min(ok) < 0.5)
        def _(hh=hh, s1=s1, s2=s2):
            emit(hh, s1, s2, True)


def peer_route(h2, wq, sk, tt):
    t, d = h2.shape
    nk = sk.shape[2]
    out = jax.ShapeDtypeStruct((PEER_HEADS, nk, t), F32)
    outb = jax.ShapeDtypeStruct((PEER_HEADS, nk, t), BF16)
    hps = 2
    ospec = pl.BlockSpec((hps, nk, tt), lambda i, h: (h, 0, i))
    return pl.pallas_call(
        _peer_route_body,
        out_shape=(outb, out, out, outb),
        grid=(t // tt, PEER_HEADS // hps),
        in_specs=[pl.BlockSpec((tt, d), lambda i, h: (i, 0)),
                  pl.BlockSpec(wq.shape, lambda i, h: (0, 0)),
                  pl.BlockSpec((2, hps) + sk.shape[2:], lambda i, h: (0, h, 0, 0))],
        out_specs=(ospec, ospec, ospec, ospec),
        scratch_shapes=[pltpu.VMEM((PEER_HEADS, tt, wq.shape[1] // PEER_HEADS), BF16)],
        compiler_params=_cp(("arbitrary", "arbitrary")),
        name="peer_route",
    )(h2, wq, sk)


def _peer_dense_body(h_ref, u_ref, v_ref, rk_ref, lim_ref, e1_ref, e2_ref, x1_ref, gate_ref, y_ref, w_scr, lb_scr,
                     eb_scr):
    j = pl.program_id(1)
    nk = rk_ref.shape[1]
    et, tt = u_ref.shape[0], h_ref.shape[0]
    rows = lb_scr.shape[2]

    @pl.when(j == 0)
    def _():
        y_ref[...] = jnp.zeros_like(y_ref)

    zero = jnp.zeros((), BF16)
    for rr in range(et // nk):
        r = j * (et // nk) + rr
        for hd in range(PEER_HEADS):
            lb_scr[rr, hd] = jnp.broadcast_to(lim_ref[hd, pl.ds(r, 1), :], (rows, tt)).astype(BF16)
            eb_scr[rr, hd] = jnp.broadcast_to(0.5 * e1_ref[hd, pl.ds(r, 1), :], (rows, tt)).astype(BF16)
    for rr in range(et // nk):
        for ch in range(nk // rows):
            sl = pl.ds(ch * rows, rows)
            w = jnp.zeros((rows, tt), BF16)
            for hd in range(PEER_HEADS):
                w = w + jnp.where(rk_ref[hd, sl, :] < lb_scr[rr, hd], e2_ref[hd, sl, :], zero) * eb_scr[rr, hd]
            w_scr[pl.ds(rr * nk + ch * rows, rows), :] = w
    a = _nt(u_ref[...], h_ref[...])
    act2 = a * (lax.erf(a * np.float32(1.0 / np.sqrt(2.0))) + 1.0)
    wt = (act2 * w_scr[...].astype(F32)).astype(BF16)
    y_ref[...] += _tn(wt, v_ref[...])

    @pl.when(j == pl.num_programs(1) - 1)
    def _():
        y_ref[...] = x1_ref[...] + gate_ref[0] * y_ref[...]


def peer_dense(h2, u, v, rk, lim, e1, e2, x1, gate, tt, et, tiles_per_mod):
    t, d = h2.shape
    ne = u.shape[0]
    nk = rk.shape[1]
    once = pl.Buffered(1)
    tok = pl.BlockSpec((tt, d), lambda i, j: (i, 0), pipeline_mode=once)
    exp = pl.BlockSpec((et, d), lambda i, j: (j, 0))
    rt = pl.BlockSpec((PEER_HEADS, nk, tt), lambda i, j: (0, 0, i), pipeline_mode=once)
    if gate.shape[1] == 1:
        gspec = pl.BlockSpec((1, 1, d), lambda i, j: (i // tiles_per_mod, 0, 0), pipeline_mode=once)
    else:
        gspec = pl.BlockSpec((1, tt, d), lambda i, j: (i, 0, 0), pipeline_mode=once)
    return pl.pallas_call(
        _peer_dense_body,
        out_shape=jax.ShapeDtypeStruct((t, d), F32),
        grid=(t // tt, ne // et),
        in_specs=[tok, exp, exp, rt, rt, rt, rt, tok, gspec],
        out_specs=pl.BlockSpec((tt, d), lambda i, j: (i, 0)),
        scratch_shapes=[pltpu.VMEM((et, tt), BF16), pltpu.VMEM((et // nk, PEER_HEADS, BF16_ROWS, tt), BF16),
                        pltpu.VMEM((et // nk, PEER_HEADS, BF16_ROWS, tt), BF16)],
        compiler_params=_cp(("arbitrary", "arbitrary")),
        name="peer_dense",
    )(h2, u, v, rk, lim, e1, e2, x1, gate)


def _slope_lanes(tq):
    h = np.arange(1, N_HEADS + 1, dtype=np.float32)
    s = (2.0 ** (-8.0 * h / N_HEADS)).reshape(N_KV, GROUP)
    return jnp.asarray(np.repeat(s.T.reshape(-1), tq)[None, :], F32)


def _frac_t(nc, ns, nc_pad, ns_pad):
    pos = np.arange(nc)[:, None] * CMP_STRIDE + np.arange(CMP_BLOCK)[None, :]
    f = ((pos // SEL_BLOCK)[:, :, None] == np.arange(ns)[None, None, :]).mean(axis=1)
    out = np.zeros((ns_pad, nc_pad), np.float32)
    out[:ns, :nc] = f.T
    return jnp.asarray(out)


def _gate_expand(width):
    e = np.zeros((3, width, N_HEADS * HEAD_DIM), np.float32)
    for g in range(N_KV):
        for r in range(GROUP):
            for j in range(3):
                c0 = r * KV_W + g * HEAD_DIM
                e[j, (g * GROUP + r) * 3 + j, c0:c0 + HEAD_DIM] = 1.0
    return jnp.asarray(e, BF16)


def _rgd(a, axis):
    shp = a.shape
    a = a.reshape(shp[:axis] + (N_KV, GROUP, HEAD_DIM) + shp[axis + 1:])
    a = jnp.swapaxes(a, axis, axis + 1)
    return a.reshape(shp)


def kernel(x_prompt, x_sample, cache_kv_cmp, cache_kv_slc, cache_kv_win, state_pool, page_table, c_prompt, c_sample,
           w_ada, b_ada, g_norm1, g_norm2, w_in, w_out, g_qnorm, g_knorm, cmp_pe, w_cmp, w_pool, pool_scale,
           peer_wq, peer_subkeys, peer_u, peer_v):
    bp, tp, dm = x_prompt.shape
    bs, ts, _ = x_sample.shape
    n_pages = page_table.shape[1]
    page = cache_kv_cmp.shape[1]
    past = n_pages * page
    nsa_w = N_HEADS * HEAD_DIM
    kv3 = 3 * 2 * KV_W
    ngl = 3 * N_HEADS
    gpad = 128
    tm, tq, sel_bucket, tt, et = 256, 32, 4, 512, 1024

    wq = _rgd(w_in[:, :nsa_w], 1).astype(BF16)
    wkv = w_in[:, nsa_w:nsa_w + kv3].astype(BF16)
    wg = jnp.pad(w_in[:, nsa_w + kv3:nsa_w + kv3 + ngl], ((0, 0), (0, gpad - ngl))).astype(BF16)
    wu = w_in[:, nsa_w + kv3 + ngl:].astype(BF16)
    gq = jnp.tile(g_qnorm, N_HEADS)[None, :]
    gk = jnp.tile(g_knorm, (1, N_KV))
    g1 = g_norm1[None, :]
    g2 = g_norm2[None, :]
    wn = _rgd(w_out[:nsa_w], 0).astype(BF16)
    wp = w_out[nsa_w:].astype(BF16)
    wpool = w_pool.astype(BF16)
    ps = pool_scale[None, :]
    wc = jnp.einsum('cglde,gh->clgdhe', w_cmp, jnp.eye(N_KV, dtype=F32)).reshape(2, CMP_BLOCK, KV_W, KV_W).astype(BF16)
    pe = jnp.transpose(cmp_pe, (1, 0, 2, 3)).reshape(CMP_BLOCK, 2 * KV_W)
    e_gate = _gate_expand(gpad)
    pwq = peer_wq.astype(BF16)
    psk = peer_subkeys.astype(BF16)
    pu = peer_u.astype(BF16)
    pv = peer_v.astype(BF16)

    mod = adaln(jnp.concatenate([c_prompt, c_sample], axis=0), w_ada, b_ada).reshape(bp + bs, 6, dm)
    mod_p = [mod[:bp, k][:, None, :] for k in range(6)]
    mod_s = [jnp.repeat(mod[bp:, k], ts, axis=0).reshape(bs * ts // tm, tm, dm) for k in range(6)]

    xp = x_prompt.reshape(bp * tp, dm)
    q, kvc, kvs, kvw, gates, u, kvc_t, kvs_t, kvw_t = inproj(xp, mod_p[0], mod_p[1], g1, wq, wkv, wg, wu, gq, gk, tm,
                                                             tp // tm, seq_len=tp)
    kc, vc = compress_prompt(kvc.reshape(bp, tp // CMP_STRIDE, CMP_STRIDE * 2 * KV_W), pe, wc, gk)
    nc = (tp - CMP_BLOCK) // CMP_STRIDE + 1
    kvw3 = kvw.reshape(bp, tp, 2 * KV_W)
    o_nsa = nsa_prompt(q, gates, kc, vc, kvs.reshape(bp, tp, 2 * KV_W), jnp.pad(kvw3, ((0, 0), (WINDOW, 0), (0, 0))),
                       _slope_lanes(tq), _frac_t(nc, tp // SEL_BLOCK, tp // CMP_STRIDE, 40), e_gate, tq, sel_bucket)
    u3 = u.reshape(bp, tp, -1)
    d_pool = pool_diff(jnp.pad(u3, ((0, 0), (POOL_BUF, 0), (0, 0))), 0).reshape(bp * tp, -1)
    x1, h2 = outproj(o_nsa, d_pool, xp, mod_p[2], mod_p[3], mod_p[4], g2, wn, wp, wpool, ps, tm, tp // tm)
    routes = peer_route(h2, pwq, psk, 256)
    y_prompt = peer_dense(h2, pu, pv, *routes, x1, mod_p[5], tt, et, tp // tt).reshape(bp, tp, dm)
    rows_major = lambda a: jnp.transpose(a.reshape(bp, 2, N_KV, HEAD_DIM, -1), (0, 4, 1, 2, 3))
    win_p = rows_major(kvw_t[:, :, -min(WINDOW, tp):])
    pool_p = u3[:, -POOL_BUF:]

    xs = x_sample.reshape(bs * ts, dm)
    q, kvc_s, kvs_s, kvw_s, gates, u = inproj(xs, mod_s[0], mod_s[1], g1, wq, wkv, wg, wu, gq, gk, tm, 1)
    feat_major = lambda c: jnp.transpose(c, (0, 2, 3, 4, 1)).reshape(c.shape[0], 2 * KV_W, c.shape[1])
    kc, vc = compress_sample(page_table, feat_major(cache_kv_cmp), pe, wc, gk)
    nc = (past + ts - CMP_BLOCK) // CMP_STRIDE + 1
    ns = -(-(past + ts) // SEL_BLOCK)
    kvw_s3 = kvw_s.reshape(bs, ts, 2 * KV_W)
    o_nsa = nsa_sample(page_table, q.reshape(bs, ts, -1), gates.reshape(bs, ts, -1), kc, vc,
                       feat_major(cache_kv_slc), kvs_s.reshape(bs, ts, 2 * KV_W), feat_major(cache_kv_win), kvw_s3,
                       _slope_lanes(ts), _frac_t(nc, ns, past // CMP_STRIDE, 40), e_gate).reshape(bs * ts, -1)
    ext = jnp.concatenate([state_pool, u.reshape(bs, ts, -1)], axis=1)
    d_pool = pool_diff(ext, past, bb=8).reshape(bs * ts, -1)
    x1, h2 = outproj(o_nsa, d_pool, xs, mod_s[2], mod_s[3], mod_s[4], g2, wn, wp, wpool, ps, tm, 1)
    routes = peer_route(h2, pwq, psk, 256)
    gate5 = mod_s[5].reshape(bs * ts // tt, tt, dm)
    y_sample = peer_dense(h2, pu, pv, *routes, x1, gate5, tt, et, 1).reshape(bs, ts, dm)
    shp_s = (bs, ts, 2, N_KV, HEAD_DIM)
    wbuf = cache_kv_win.shape[1]
    win_s = jnp.concatenate([cache_kv_win, kvw_s.reshape(shp_s)], axis=1)[:, -wbuf:]
    pool_s = ext[:, -POOL_BUF:]

    return (y_prompt, y_sample, rows_major(kvc_t), rows_major(kvs_t), win_p, pool_p,
            kvc_s.reshape(shp_s), kvs_s.reshape(shp_s), win_s, pool_s)
```

```python
import functools

import numpy as np
import jax
import jax.numpy as jnp
from jax import lax
from jax.experimental import pallas as pl
from jax.experimental.pallas import tpu as pltpu

F32, BF16, I32 = jnp.float32, jnp.bfloat16, jnp.int32

HEAD_DIM = 64
N_KV = 4
GROUP = 4
N_HEADS = N_KV * GROUP
CMP_BLOCK = 32
CMP_STRIDE = 16
SEL_BLOCK = 64
SEL_TOP = 16
WINDOW = 512
POOL_SIZES = (2, 4, 8, 16)
POOL_BUF = max(POOL_SIZES) - 1
PEER_HEADS = 8
N_KEYS = 128
PEER_TOPK = 16
EPS = 1e-6
NEG = -1e30
FORCE_BONUS = 1e4
KV_W = N_KV * HEAD_DIM
VMEM_LIMIT = 56 * 1024 * 1024


def _cp(sem, vmem=VMEM_LIMIT):
    return pltpu.CompilerParams(dimension_semantics=sem, vmem_limit_bytes=vmem)


def _dot(a, b):
    return jnp.dot(a, b, preferred_element_type=F32)


def _nt(a, b):
    return lax.dot_general(a, b, (((1,), (1,)), ((), ())), preferred_element_type=F32)


def _tn(a, b):
    return lax.dot_general(a, b, (((0,), (0,)), ((), ())), preferred_element_type=F32)


def _split_dot(x, w):
    hi = x.astype(BF16)
    r1 = x - hi.astype(F32)
    mid = r1.astype(BF16)
    lo = (r1 - mid.astype(F32)).astype(BF16)
    return _dot(hi, w) + _dot(mid, w) + _dot(lo, w)


def _ones64():
    r = lax.broadcasted_iota(I32, (KV_W, KV_W), 0) // HEAD_DIM
    c = lax.broadcasted_iota(I32, (KV_W, KV_W), 1) // HEAD_DIM
    return (r == c).astype(BF16)


def _rms64(x, ones):
    ms = _split_dot(x * x, ones) * (1.0 / HEAD_DIM)
    return x * lax.rsqrt(ms + EPS)


def _rms_rows(x):
    return x * lax.rsqrt(jnp.mean(x * x, axis=-1, keepdims=True) + EPS)


def _adaln_body(c_ref, w_ref, b_ref, o_ref):
    c = c_ref[...]
    s = (c * jax.nn.sigmoid(c)).astype(BF16)
    o_ref[...] = _dot(s, w_ref[...].astype(BF16)) + b_ref[...]


def adaln(c_all, w_ada, b_ada, tn=1024):
    m, d = c_all.shape
    n = w_ada.shape[1]
    return pl.pallas_call(
        _adaln_body,
        out_shape=jax.ShapeDtypeStruct((m, n), F32),
        grid=(n // tn,),
        in_specs=[pl.BlockSpec((m, d), lambda j: (0, 0)),
                  pl.BlockSpec((d, tn), lambda j: (0, j)),
                  pl.BlockSpec((1, tn), lambda j: (0, j))],
        out_specs=pl.BlockSpec((m, tn), lambda j: (0, j)),
        compiler_params=_cp(("arbitrary",)),
        name="adaln",
    )(c_all, w_ada, b_ada.reshape(1, n))


def _inproj_body(x_ref, sh_ref, sc_ref, g1_ref, wq_ref, wkv_ref, wg_ref, wu_ref, gq_ref, gk_ref,
                 q_ref, kvc_ref, kvs_ref, kvw_ref, gt_ref, u_ref, *t_refs):
    h = _rms_rows(x_ref[...]) * g1_ref[...]
    h = h * (1.0 + sc_ref[0]) + sh_ref[0]
    hb = h.astype(BF16)
    ones = _ones64()
    zq = _dot(hb, wq_ref[...])
    for r in range(GROUP):
        sl = slice(r * KV_W, (r + 1) * KV_W)
        q_ref[:, sl] = _rms64(zq[:, sl], ones) * gq_ref[:, sl] * (HEAD_DIM ** -0.5)
    zkv = _dot(hb, wkv_ref[...])
    kvc_ref[...] = zkv[:, 0:2 * KV_W]
    kvs_ref[:, 0:KV_W] = _rms64(zkv[:, 2 * KV_W:3 * KV_W], ones) * gk_ref[1:2, :]
    kvs_ref[:, KV_W:] = zkv[:, 3 * KV_W:4 * KV_W]
    kvw_ref[:, 0:KV_W] = _rms64(zkv[:, 4 * KV_W:5 * KV_W], ones) * gk_ref[2:3, :]
    kvw_ref[:, KV_W:] = zkv[:, 5 * KV_W:6 * KV_W]
    gt_ref[...] = jax.nn.sigmoid(_dot(hb, wg_ref[...]))
    u_ref[...] = _dot(hb, wu_ref[...])
    for src, dst in zip((kvc_ref, kvs_ref, kvw_ref), t_refs):
        dst[0] = src[...].T


def _mod_spec(arr, tm, rows_per_mod):
    d = arr.shape[-1]
    if arr.shape[1] == 1:
        return pl.BlockSpec((1, 1, d), lambda i: (i // rows_per_mod, 0, 0))
    return pl.BlockSpec((1, tm, d), lambda i: (i, 0, 0))


def inproj(x, shift, scale, g1, wq, wkv, wg, wu, gq, gk, tm, tiles_per_mod, seq_len=None):
    t, d = x.shape
    nq, nkv, ng, nu = wq.shape[1], wkv.shape[1], wg.shape[1], wu.shape[1]
    full = lambda a: pl.BlockSpec(a.shape, lambda i: (0,) * a.ndim)
    row = lambda n: pl.BlockSpec((tm, n), lambda i: (i, 0))
    kvrow = jax.ShapeDtypeStruct((t, 2 * KV_W), F32)
    out_shape = [jax.ShapeDtypeStruct((t, nq), F32), kvrow, kvrow, kvrow,
                 jax.ShapeDtypeStruct((t, ng), F32), jax.ShapeDtypeStruct((t, nu), F32)]
    out_specs = [row(nq), row(2 * KV_W), row(2 * KV_W), row(2 * KV_W), row(ng), row(nu)]
    if seq_len is not None:
        tiles = seq_len // tm
        out_shape += [jax.ShapeDtypeStruct((t // seq_len, 2 * KV_W, seq_len), F32)] * 3
        out_specs += [pl.BlockSpec((1, 2 * KV_W, tm), lambda i: (i // tiles, 0, i % tiles))] * 3
    return pl.pallas_call(
        _inproj_body,
        out_shape=tuple(out_shape),
        grid=(t // tm,),
        in_specs=[row(d), _mod_spec(shift, tm, tiles_per_mod), _mod_spec(scale, tm, tiles_per_mod), full(g1),
                  full(wq), full(wkv), full(wg), full(wu), full(gq), full(gk)],
        out_specs=tuple(out_specs),
        compiler_params=_cp(("arbitrary",)),
        name="inproj",
    )(x, shift, scale, g1, wq, wkv, wg, wu, gq, gk)


def _compress(get_x, nchunk, pe_ref, w_ref, gk_ref, kc_ref, vc_ref):
    half = CMP_BLOCK // 2
    acc = [jnp.zeros((nchunk, KV_W), F32) for _ in range(4)]
    for l in range(half):
        xl = get_x(l)
        a = (xl + pe_ref[l:l + 1, :]).astype(BF16)
        b = (xl + pe_ref[half + l:half + l + 1, :]).astype(BF16)
        acc[0] += _dot(a[:, :KV_W], w_ref[0, l])
        acc[1] += _dot(a[:, KV_W:], w_ref[1, l])
        acc[2] += _dot(b[:, :KV_W], w_ref[0, half + l])
        acc[3] += _dot(b[:, KV_W:], w_ref[1, half + l])
    rio = lax.broadcasted_iota(I32, (nchunk, KV_W), 0)
    nxt = lambda v: jnp.where(rio < nchunk - 1, pltpu.roll(v, nchunk - 1, 0), 0.0)
    ck = acc[0] + nxt(acc[2])
    cv = acc[1] + nxt(acc[3])
    kc_ref[...] = _rms64(ck, _ones64()) * gk_ref[0:1, :]
    vc_ref[...] = cv


def _compress_prompt_body(x_ref, pe_ref, w_ref, gk_ref, kc_ref, vc_ref):
    row_w = 2 * KV_W
    _compress(lambda l: x_ref[0, :, l * row_w:(l + 1) * row_w], x_ref.shape[1], pe_ref, w_ref, gk_ref,
              kc_ref.at[0], vc_ref.at[0])


def compress_prompt(kv_chunks, pe, wc, gk):
    b, nchunk, width = kv_chunks.shape
    full = lambda a: pl.BlockSpec(a.shape, lambda i: (0,) * a.ndim)
    out = jax.ShapeDtypeStruct((b, nchunk, KV_W), F32)
    ospec = pl.BlockSpec((1, nchunk, KV_W), lambda i: (i, 0, 0))
    return pl.pallas_call(
        _compress_prompt_body,
        out_shape=(out, out),
        grid=(b,),
        in_specs=[pl.BlockSpec((1, nchunk, width), lambda i: (i, 0, 0)), full(pe), full(wc), full(gk)],
        out_specs=(ospec, ospec),
        compiler_params=_cp(("arbitrary",)),
        name="compress_prompt",
    )(kv_chunks, pe, wc, gk)


def _page_copy(cache_ref, buf_ref, sem_ref, pt_ref, seq, slot, j):
    return pltpu.make_async_copy(cache_ref.at[pt_ref[seq, j]], buf_ref.at[slot, j], sem_ref.at[slot])


def _gather_pages(cache_ref, buf_ref, sem_ref, pt_ref):
    b = pl.program_id(0)
    nb = pl.num_programs(0)
    n_pages = pt_ref.shape[1]
    slot = b % 2

    def start(seq, s):
        for j in range(n_pages):
            _page_copy(cache_ref, buf_ref, sem_ref, pt_ref, seq, s, j).start()

    @pl.when(b == 0)
    def _():
        start(b, slot)

    @pl.when(b + 1 < nb)
    def _():
        start(b + 1, 1 - slot)

    for j in range(n_pages):
        _page_copy(cache_ref, buf_ref, sem_ref, pt_ref, b, slot, j).wait()
    return slot


LANES = 128
BF16_ROWS = 16


def _compress_sample_body(pt_ref, cache_ref, pe_ref, w_ref, gk_ref, kc_ref, vc_ref, buf_ref, xs_ref, sem_ref):
    n_pages, page = pt_ref.shape[1], cache_ref.shape[2]
    slot = _gather_pages(cache_ref, buf_ref, sem_ref, pt_ref)
    nlb = xs_ref.shape[0]
    for j in range(n_pages):
        for k in range(nlb):
            xs_ref[k, pl.ds(j * page, page), :] = buf_ref[slot, j, pl.ds(k * LANES, LANES), :].T
    nchunk = n_pages * page // CMP_STRIDE
    get_x = lambda l: jnp.concatenate([xs_ref[k, pl.ds(l, nchunk, stride=CMP_STRIDE), :] for k in range(nlb)], axis=1)
    _compress(get_x, nchunk, pe_ref, w_ref, gk_ref, kc_ref.at[0], vc_ref.at[0])


def compress_sample(page_table, cache_t, pe, wc, gk):
    b, n_pages = page_table.shape
    _, width, page = cache_t.shape
    nchunk = n_pages * page // CMP_STRIDE
    full = lambda a: pl.BlockSpec(a.shape, lambda i, pt: (0,) * a.ndim)
    out = jax.ShapeDtypeStruct((b, nchunk, KV_W), F32)
    ospec = pl.BlockSpec((1, nchunk, KV_W), lambda i, pt: (i, 0, 0))
    return pl.pallas_call(
        _compress_sample_body,
        out_shape=(out, out),
        grid_spec=pltpu.PrefetchScalarGridSpec(
            num_scalar_prefetch=1, grid=(b,),
            in_specs=[pl.BlockSpec(memory_space=pl.ANY), full(pe), full(wc), full(gk)],
            out_specs=(ospec, ospec),
            scratch_shapes=[pltpu.VMEM((2, n_pages, width, page), F32),
                            pltpu.VMEM((width // LANES, n_pages * page, LANES), F32),
                            pltpu.SemaphoreType.DMA((2,))]),
        compiler_params=_cp(("arbitrary",)),
        name="compress_sample",
    )(page_table, cache_t, pe, wc, gk)


def _softmax_cols(s, valid):
    s = jnp.where(valid, s, NEG)
    m = jnp.max(s, axis=0, keepdims=True)
    e = jnp.where(valid, jnp.exp(s - m), 0.0)
    return e / jnp.maximum(jnp.sum(e, axis=0, keepdims=True), 1e-30)


def _nsa_front(q, kc, vc, pq0, slope, frac_t, tq, ns):
    cols = N_HEADS * tq
    lane = lax.broadcasted_iota(I32, (1, cols), 1)
    pq = (pq0 + lane % tq).astype(F32)
    lg = lax.broadcasted_iota(I32, (tq, KV_W), 1) // HEAD_DIM
    qfull = jnp.concatenate([jnp.where(lg == g, q[:, r * KV_W:(r + 1) * KV_W], 0.0)
                             for r in range(GROUP) for g in range(N_KV)], axis=0).astype(BF16)

    nc = kc.shape[0]
    pos_c = (lax.broadcasted_iota(I32, (nc, cols), 0) * CMP_STRIDE + (CMP_BLOCK - 1)).astype(F32)
    dist_c = pq - pos_c
    p_c = _softmax_cols(_nt(kc.astype(BF16), qfull) - slope * dist_c, dist_c >= 0).astype(BF16)
    o_c = _tn(p_c, vc.astype(BF16))

    imp = _dot(frac_t.astype(BF16), p_c)
    imp = imp + pltpu.roll(imp, 4 * tq, 1) + pltpu.roll(imp, 8 * tq, 1) + pltpu.roll(imp, 12 * tq, 1)
    gq = N_KV * tq
    compact = gq % LANES == 0
    wsel = gq if compact else cols
    imp = imp[:, :wsel]
    nsp = imp.shape[0]
    jio = lax.broadcasted_iota(I32, (nsp, wsel), 0)
    pos_q = pq0 + lax.broadcasted_iota(I32, (1, wsel), 1) % tq
    blk_q = pos_q // SEL_BLOCK
    forced = (jio == 0) | (jio == blk_q) | (jio == blk_q - 1)
    valid_blk = jio * SEL_BLOCK <= pos_q
    imp = jnp.where(valid_blk, imp + FORCE_BONUS * forced.astype(F32), NEG)
    rank = jnp.zeros((nsp, wsel), I32)
    for i in range(ns):
        row = imp[i:i + 1, :]
        rank = rank + ((row > imp) | ((row == imp) & (i < jio))).astype(I32)
    sel_bias = jnp.where(rank < min(SEL_TOP, ns), 0.0, NEG)
    if compact:
        sel_bias = jnp.concatenate([sel_bias] * GROUP, axis=1)
    return qfull, pq, o_c, sel_bias


def _as_column(row):
    return jnp.transpose(jnp.broadcast_to(row, (LANES, row.shape[1])))[:, 0:1]


def _nsa_selected(qfull, pq, sel_bias, ks_ref, slope, nblk, blk_q):
    cols = qfull.shape[0]
    nsp = sel_bias.shape[0]
    tk = nblk * SEL_BLOCK
    jio = lax.broadcasted_iota(I32, (nsp, cols), 0)
    bias = jnp.where(jio < blk_q, sel_bias, NEG) + slope * (jio * SEL_BLOCK).astype(F32)
    inblk = slope * lax.broadcasted_iota(I32, (SEL_BLOCK, cols), 0).astype(F32)
    s_m = _nt(ks_ref[0:tk, 0:KV_W].astype(BF16), qfull).reshape(nblk, SEL_BLOCK, cols)
    s_m = s_m + inblk[None, :, :] + bias[:nblk][:, None, :]

    d0 = blk_q * SEL_BLOCK
    kd = ks_ref[pl.ds(d0 if isinstance(d0, int) else pl.multiple_of(d0, SEL_BLOCK), SEL_BLOCK), :]
    kpos_d = (blk_q * SEL_BLOCK + lax.broadcasted_iota(I32, (SEL_BLOCK, cols), 0)).astype(F32)
    valid_d = kpos_d <= pq
    s_d = jnp.where(valid_d, _nt(kd[:, :KV_W].astype(BF16), qfull) + slope * kpos_d, NEG)

    m = jnp.maximum(jnp.max(jnp.max(s_m, axis=0), axis=0, keepdims=True), jnp.max(s_d, axis=0, keepdims=True))
    e_m = jnp.exp(s_m - m[None, :, :])
    e_d = jnp.where(valid_d, jnp.exp(s_d - m), 0.0)
    l = jnp.sum(jnp.sum(e_m, axis=0), axis=0, keepdims=True) + jnp.sum(e_d, axis=0, keepdims=True)
    o = _tn(e_m.astype(BF16).reshape(tk, cols), ks_ref[0:tk, KV_W:].astype(BF16)) + _tn(e_d.astype(BF16),
                                                                                       kd[:, KV_W:].astype(BF16))
    return o * _as_column(1.0 / jnp.maximum(l, 1e-30))


def _nsa_window(qfull, pq, kw, vw, pw0, slope):
    tw, cols = kw.shape[0], qfull.shape[0]
    pos_w = (pw0 + lax.broadcasted_iota(I32, (tw, cols), 0)).astype(F32)
    dist_w = pq - pos_w
    valid_w = (dist_w >= 0) & (dist_w <= WINDOW) & (pos_w >= 0)
    p_w = _softmax_cols(_nt(kw.astype(BF16), qfull) - slope * dist_w, valid_w).astype(BF16)
    return _tn(p_w, vw.astype(BF16))


def _nsa_window_steady(qfull, kw, vw, slope, tq):
    cols = qfull.shape[0]
    nb = WINDOW // tq
    qq = lax.broadcasted_iota(I32, (1, cols), 1) % tq
    rio = lax.broadcasted_iota(I32, (tq, cols), 0)
    s = _nt(kw.astype(BF16), qfull).reshape(nb + 1, tq, cols) + (slope * rio.astype(F32))[None, :, :]
    s = s + (slope * (lax.broadcasted_iota(I32, (nb + 1, cols), 0) * tq).astype(F32))[:, None, :]
    ok_f, ok_l = rio >= qq, rio <= qq
    s_f, s_m, s_l = jnp.where(ok_f, s[0], NEG), s[1:nb], jnp.where(ok_l, s[nb], NEG)
    m = jnp.maximum(jnp.max(jnp.max(s_m, axis=0), axis=0, keepdims=True),
                    jnp.max(jnp.maximum(s_f, s_l), axis=0, keepdims=True))
    e_f = jnp.where(ok_f, jnp.exp(s_f - m), 0.0)
    e_l = jnp.where(ok_l, jnp.exp(s_l - m), 0.0)
    e_m = jnp.exp(s_m - m[None, :, :])
    l = jnp.sum(jnp.sum(e_m, axis=0), axis=0, keepdims=True) + jnp.sum(e_f + e_l, axis=0, keepdims=True)
    v = vw.astype(BF16)
    o = (_tn(e_m.astype(BF16).reshape((nb - 1) * tq, cols), v[tq:nb * tq]) + _tn(e_f.astype(BF16), v[0:tq])
         + _tn(e_l.astype(BF16), v[nb * tq:]))
    return o * _as_column(1.0 / jnp.maximum(l, 1e-30))


def _nsa_combine(gates, e_gate, outs, tq):
    cols = N_HEADS * tq
    row_g = (lax.broadcasted_iota(I32, (cols, KV_W), 0) // tq) % N_KV
    own = row_g == lax.broadcasted_iota(I32, (cols, KV_W), 1) // HEAD_DIM

    def fold(o):
        o = jnp.where(own, o, 0.0)
        return jnp.concatenate(
            [sum(o[(r * N_KV + g) * tq:(r * N_KV + g + 1) * tq, :] for g in range(N_KV)) for r in range(GROUP)], axis=1)

    out = jnp.zeros((tq, N_HEADS * HEAD_DIM), F32)
    for j, o in enumerate(outs):
        out = out + _split_dot(gates, e_gate[j]) * fold(o)
    return out


def _nsa_prompt_body(q_ref, gt_ref, kc_ref, vc_ref, ks_ref, kw_ref, sl_ref, fr_ref, eg_ref, o_ref, os_ref, ow_ref,
                     *, tq, ns, tw, bucket):
    q0 = pl.program_id(1) * tq
    slope = sl_ref[...]
    qfull, pq, o_c, sel_bias = _nsa_front(q_ref[...], kc_ref[0], vc_ref[0], q0, slope, fr_ref[...], tq, ns)
    blk_q = q0 // SEL_BLOCK
    for k in range(ns // bucket):
        @pl.when(blk_q // bucket == k)
        def _():
            os_ref[...] = _nsa_selected(qfull, pq, sel_bias, ks_ref.at[0], slope, (k + 1) * bucket, blk_q)
    w = kw_ref[0, pl.ds(pl.multiple_of(q0, 8), tw), :]

    @pl.when(q0 >= WINDOW)
    def _():
        ow_ref[...] = _nsa_window_steady(qfull, w[:, :KV_W], w[:, KV_W:], slope, tq)

    @pl.when(q0 < WINDOW)
    def _():
        ow_ref[...] = _nsa_window(qfull, pq, w[:, :KV_W], w[:, KV_W:], q0 - WINDOW, slope)

    o_ref[...] = _nsa_combine(gt_ref[...], eg_ref, (o_c, os_ref[...], ow_ref[...]), tq)


def nsa_prompt(q, gates, kc, vc, kvs, kvw_pad, slope, frac_t, e_gate, tq, bucket):
    b, t, _ = kvs.shape
    ns = t // SEL_BLOCK
    assert SEL_BLOCK % tq == 0 and ns % bucket == 0 and t % SEL_BLOCK == 0
    assert WINDOW % tq == 0 and tq % 8 == 0
    tw = WINDOW + tq
    nq = t // tq
    full = lambda a: pl.BlockSpec(a.shape, lambda i, j: (0,) * a.ndim)
    perb = lambda a: pl.BlockSpec((1,) + a.shape[1:], lambda i, j: (i, 0, 0))
    row = lambda n: pl.BlockSpec((tq, n), lambda i, j: (i * nq + j, 0))
    return pl.pallas_call(
        functools.partial(_nsa_prompt_body, tq=tq, ns=ns, tw=tw, bucket=bucket),
        out_shape=jax.ShapeDtypeStruct((b * t, N_HEADS * HEAD_DIM), F32),
        grid=(b, nq),
        in_specs=[row(q.shape[1]), row(gates.shape[1]), perb(kc), perb(vc), perb(kvs), perb(kvw_pad),
                  full(slope), full(frac_t), full(e_gate)],
        out_specs=row(N_HEADS * HEAD_DIM),
        scratch_shapes=[pltpu.VMEM((N_HEADS * tq, KV_W), F32), pltpu.VMEM((N_HEADS * tq, KV_W), F32)],
        compiler_params=_cp(("arbitrary", "arbitrary")),
        name="nsa_prompt",
    )(q, gates, kc, vc, kvs, kvw_pad, slope, frac_t, e_gate)


def _nsa_sample_body(pt_ref, q_ref, gt_ref, kc_ref, vc_ref, cache_ref, knew_ref, win_ref, wnew_ref, sl_ref, fr_ref,
                     eg_ref, o_ref, buf_ref, ks_ref, sem_ref, *, tq, ns, past):
    n_pages, page = pt_ref.shape[1], cache_ref.shape[2]
    tk = ns * SEL_BLOCK

    @pl.when(pl.program_id(0) == 0)
    def _():
        ks_ref[past:, :] = jnp.zeros((tk - past, ks_ref.shape[1]), F32)

    slot = _gather_pages(cache_ref, buf_ref, sem_ref, pt_ref)
    for j in range(n_pages):
        ks_ref[pl.ds(j * page, page), :] = buf_ref[slot, j].T
    ks_ref[past:past + tq, :] = knew_ref[0]
    wbuf = win_ref.shape[2]
    tw = wbuf + SEL_BLOCK
    w = jnp.concatenate([win_ref[0].T, wnew_ref[0], jnp.zeros((tw - wbuf - tq, 2 * KV_W), F32)], axis=0)
    slope = sl_ref[...]
    qfull, pq, o_c, sel_bias = _nsa_front(q_ref[0], kc_ref[0], vc_ref[0], past, slope, fr_ref[...], tq, ns)
    o_s = _nsa_selected(qfull, pq, sel_bias, ks_ref, slope, past // SEL_BLOCK, past // SEL_BLOCK)
    o_w = _nsa_window(qfull, pq, w[:, :KV_W], w[:, KV_W:], past - wbuf, slope)
    o_ref[0] = _nsa_combine(gt_ref[0], eg_ref, (o_c, o_s, o_w), tq)


def nsa_sample(page_table, q, gates, kc, vc, cache_t, kvs_new, win_t, kvw_new, slope, frac_t, e_gate):
    b, n_pages = page_table.shape
    _, width, page = cache_t.shape
    past = n_pages * page
    tq = q.shape[1]
    assert past % SEL_BLOCK + tq <= SEL_BLOCK
    ns = -(-(past + tq) // SEL_BLOCK)
    tk = ns * SEL_BLOCK
    full = lambda a: pl.BlockSpec(a.shape, lambda i, pt: (0,) * a.ndim)
    perb = lambda a: pl.BlockSpec((1,) + a.shape[1:], lambda i, pt: (i, 0, 0))
    return pl.pallas_call(
        functools.partial(_nsa_sample_body, tq=tq, ns=ns, past=past),
        out_shape=jax.ShapeDtypeStruct((b, tq, N_HEADS * HEAD_DIM), F32),
        grid_spec=pltpu.PrefetchScalarGridSpec(
            num_scalar_prefetch=1, grid=(b,),
            in_specs=[perb(q), perb(gates), perb(kc), perb(vc), pl.BlockSpec(memory_space=pl.ANY), perb(kvs_new),
                      perb(win_t), perb(kvw_new), full(slope), full(frac_t), full(e_gate)],
            out_specs=pl.BlockSpec((1, tq, N_HEADS * HEAD_DIM), lambda i, pt: (i, 0, 0)),
            scratch_shapes=[pltpu.VMEM((2, n_pages, width, page), F32), pltpu.VMEM((tk, width), F32),
                            pltpu.SemaphoreType.DMA((2,))]),
        compiler_params=_cp(("arbitrary",)),
        name="nsa_sample",
    )(page_table, q, gates, kc, vc, cache_t, kvs_new, win_t, kvw_new, slope, frac_t, e_gate)


def _pool_diff_body(ext_ref, d_ref, *, t, pos0):
    gw = ext_ref.shape[2] // len(POOL_SIZES)
    pos = pos0 + lax.broadcasted_iota(I32, (t, gw), 0)
    for s in range(ext_ref.shape[0]):
        for gi, w in enumerate(POOL_SIZES):
            lanes = pl.ds(gi * gw, gw)
            cur = ext_ref[s, pl.ds(POOL_BUF, t), lanes]
            acc = cur
            for j in range(1, w):
                acc = acc + ext_ref[s, pl.ds(POOL_BUF - j, t), lanes]
            cnt = jnp.minimum(w, pos + 1).astype(F32)
            d_ref[s, :, lanes] = acc / cnt - cur


def pool_diff(ext, pos0, bb=1):
    b, rows, width = ext.shape
    t = rows - POOL_BUF
    return pl.pallas_call(
        functools.partial(_pool_diff_body, t=t, pos0=pos0),
        out_shape=jax.ShapeDtypeStruct((b, t, width), F32),
        grid=(b // bb,),
        in_specs=[pl.BlockSpec((bb, rows, width), lambda i: (i, 0, 0))],
        out_specs=pl.BlockSpec((bb, t, width), lambda i: (i, 0, 0)),
        compiler_params=_cp(("arbitrary",)),
        name="pool_diff",
    )(ext)


def _outproj_body(o_ref, d_ref, x_ref, gate_ref, sh_ref, sc_ref, g2_ref, wn_ref, wp_ref, wpool_ref, ps_ref,
                  x1_ref, h2_ref):
    gw = wpool_ref.shape[1]
    d = d_ref[...]
    yp = jnp.concatenate([_dot(d[:, g * gw:(g + 1) * gw].astype(BF16), wpool_ref[g]) for g in range(len(POOL_SIZES))],
                         axis=1) * ps_ref[...]
    mix = _dot(o_ref[...].astype(BF16), wn_ref[...]) + _dot(yp.astype(BF16), wp_ref[...])
    x1 = x_ref[...] + gate_ref[0] * mix
    x1_ref[...] = x1
    h2 = _rms_rows(x1) * g2_ref[...]
    h2_ref[...] = (h2 * (1.0 + sc_ref[0]) + sh_ref[0]).astype(BF16)


def outproj(o, d, x, gate, shift, scale, g2, wn, wp, wpool, ps, tm, tiles_per_mod):
    t, dm = x.shape
    full = lambda a: pl.BlockSpec(a.shape, lambda i: (0,) * a.ndim)
    row = lambda n: pl.BlockSpec((tm, n), lambda i: (i, 0))
    ms = lambda a: _mod_spec(a, tm, tiles_per_mod)
    return pl.pallas_call(
        _outproj_body,
        out_shape=(jax.ShapeDtypeStruct((t, dm), F32), jax.ShapeDtypeStruct((t, dm), BF16)),
        grid=(t // tm,),
        in_specs=[row(o.shape[1]), row(d.shape[1]), row(dm), ms(gate), ms(shift), ms(scale), full(g2),
                  full(wn), full(wp), full(wpool), full(ps)],
        out_specs=(row(dm), row(dm)),
        compiler_params=_cp(("arbitrary",)),
        name="outproj",
    )(o, d, x, gate, shift, scale, g2, wn, wp, wpool, ps)


_CAND_PIECES = ((0, 16), (1, 8), (2, 8), (3, 8), (4, 8), (5, 8), (6, 8), (7, 8))


def _top16_rows(s, exact):
    kio = lax.broadcasted_iota(I32, s.shape, 0)
    nk = s.shape[0]
    s0 = s
    rank = jnp.full(s.shape, float(PEER_TOPK), F32)
    vals = []
    for it in range(PEER_TOPK):
        m = jnp.max(s, axis=0, keepdims=True)
        hit = s == m
        if exact:
            hit = kio == jnp.min(jnp.where(hit, kio, nk), axis=0, keepdims=True)
        rank = jnp.where(hit, float(it), rank)
        s = jnp.where(hit, -jnp.inf, s)
        vals.append(m)
    cnt = jnp.sum((s0 >= vals[-1]).astype(F32), axis=0, keepdims=True)
    return vals, rank, cnt


def _route_head(s1, s2, exact):
    v1, rank1, cnt1 = _top16_rows(s1, exact)
    v2, rank2, cnt2 = _top16_rows(s2, exact)
    n = s1.shape[1]
    v2a = jnp.concatenate(v2, axis=0)
    top = v1[0] + v2[0]

    bio = {nb: lax.broadcasted_iota(I32, (nb, n), 0) for nb in (8, PEER_TOPK)}
    v2p = {8: jnp.concatenate(v2[:8], axis=0), PEER_TOPK: v2a}
    cands, flats = [], []
    for a, nb in _CAND_PIECES:
        c = v1[a] + v2p[nb]
        ok = (bio[nb] + 1) * (a + 1) <= PEER_TOPK
        cands.append(jnp.where(ok, c, -jnp.inf))
        flats.append(a * PEER_TOPK + bio[nb])
    v1b = jnp.concatenate(v1[8:], axis=0)
    cands.append(v1b + v2[0])
    flats.append((bio[8] + 8) * PEER_TOPK)
    cand = jnp.concatenate(cands, axis=0)
    flat = jnp.concatenate(flats, axis=0)
    big = PEER_TOPK * PEER_TOPK
    taken = jnp.zeros(cand.shape, jnp.bool_)
    z = jnp.zeros((1, n), F32)
    for _ in range(PEER_TOPK):
        m = jnp.max(cand, axis=0, keepdims=True)
        hit = cand == m
        if exact:
            hit = flat == jnp.min(jnp.where(hit, flat, big), axis=0, keepdims=True)
        taken = taken | hit
        cand = jnp.where(hit, -jnp.inf, cand)
        z = z + jnp.exp(m - top)
    takenf = taken.astype(F32)
    cnt3 = jnp.sum(takenf, axis=0, keepdims=True)

    lim = jnp.zeros(s1.shape, F32)
    off = 0
    for a, nb in _CAND_PIECES:
        lim = jnp.where(rank1 == float(a), jnp.sum(takenf[off:off + nb], axis=0, keepdims=True), lim)
        off += nb
    for i in range(8):
        lim = jnp.where(rank1 == float(8 + i), takenf[off + i:off + i + 1], lim)

    k = float(PEER_TOPK)
    ok = ((cnt1 == k) & (cnt2 == k) & (cnt3 == k)).astype(F32)
    return rank2.astype(BF16), lim, jnp.exp(s1 - v1[0]) / z, jnp.exp(s2 - v2[0]).astype(BF16), ok


def _peer_route_body(h_ref, wq_ref, sk_ref, rk_ref, lim_ref, e1_ref, e2_ref, q_scr):
    step = pl.program_id(1)
    hps, qd = sk_ref.shape[1], sk_ref.shape[3]

    @pl.when(step == 0)
    def _():
        q = _dot(h_ref[...], wq_ref[...]).astype(BF16)
        for i in range(PEER_HEADS):
            q_scr[i] = q[:, i * 2 * qd:(i + 1) * 2 * qd]

    def emit(hh, s1, s2, exact):
        rk, lim, e1, e2, ok = _route_head(s1, s2, exact)
        rk_ref[hh], lim_ref[hh], e1_ref[hh], e2_ref[hh] = rk, lim, e1, e2
        return ok

    heads = []
    for hh in range(hps):
        qh = q_scr[step * hps + hh]
        s1 = _nt(sk_ref[0, hh], qh[:, :qd])
        s2 = _nt(sk_ref[1, hh], qh[:, qd:])
        heads.append((hh, s1, s2, emit(hh, s1, s2, False)))
    for hh, s1, s2, ok in heads:
        @pl.when(jnp.min(ok) < 0.5)
        def _(hh=hh, s1=s1, s2=s2):
            emit(hh, s1, s2, True)


def peer_route(h2, wq, sk, tt):
    t, d = h2.shape
    nk = sk.shape[2]
    out = jax.ShapeDtypeStruct((PEER_HEADS, nk, t), F32)
    outb = jax.ShapeDtypeStruct((PEER_HEADS, nk, t), BF16)
    hps = 2
    ospec = pl.BlockSpec((hps, nk, tt), lambda i, h: (h, 0, i))
    return pl.pallas_call(
        _peer_route_body,
        out_shape=(outb, out, out, outb),
        grid=(t // tt, PEER_HEADS // hps),
        in_specs=[pl.BlockSpec((tt, d), lambda i, h: (i, 0)),
                  pl.BlockSpec(wq.shape, lambda i, h: (0, 0)),
                  pl.BlockSpec((2, hps) + sk.shape[2:], lambda i, h: (0, h, 0, 0))],
        out_specs=(ospec, ospec, ospec, ospec),
        scratch_shapes=[pltpu.VMEM((PEER_HEADS, tt, wq.shape[1] // PEER_HEADS), BF16)],
        compiler_params=_cp(("arbitrary", "arbitrary")),
        name="peer_route",
    )(h2, wq, sk)


def _peer_dense_body(h_ref, u_ref, v_ref, rk_ref, lim_ref, e1_ref, e2_ref, x1_ref, gate_ref, y_ref, w_scr, lb_scr,
                     eb_scr):
    j = pl.program_id(1)
    nk = rk_ref.shape[1]
    et, tt = u_ref.shape[0], h_ref.shape[0]
    rows = lb_scr.shape[2]

    @pl.when(j == 0)
    def _():
        y_ref[...] = jnp.zeros_like(y_ref)

    zero = jnp.zeros((), BF16)
    for rr in range(et // nk):
        r = j * (et // nk) + rr
        for hd in range(PEER_HEADS):
            lb_scr[rr, hd] = jnp.broadcast_to(lim_ref[hd, pl.ds(r, 1), :], (rows, tt)).astype(BF16)
            eb_scr[rr, hd] = jnp.broadcast_to(0.5 * e1_ref[hd, pl.ds(r, 1), :], (rows, tt)).astype(BF16)
    nr = et // nk
    for ch in range(nk // rows):
        sl = pl.ds(ch * rows, rows)
        ws = [jnp.zeros((rows, tt), BF16) for _ in range(nr)]
        for hd in range(PEER_HEADS):
            rk, e2 = rk_ref[hd, sl, :], e2_ref[hd, sl, :]
            for rr in range(nr):
                ws[rr] = ws[rr] + jnp.where(rk < lb_scr[rr, hd], e2, zero) * eb_scr[rr, hd]
        for rr in range(nr):
            w_scr[pl.ds(rr * nk + ch * rows, rows), :] = ws[rr]
    a = _nt(u_ref[...], h_ref[...])
    act2 = a * (lax.erf(a * np.float32(1.0 / np.sqrt(2.0))) + 1.0)
    wt = (act2 * w_scr[...].astype(F32)).astype(BF16)
    y_ref[...] += _tn(wt, v_ref[...])

    @pl.when(j == pl.num_programs(1) - 1)
    def _():
        y_ref[...] = x1_ref[...] + gate_ref[0] * y_ref[...]


def peer_dense(h2, u, v, rk, lim, e1, e2, x1, gate, tt, et, tiles_per_mod):
    t, d = h2.shape
    ne = u.shape[0]
    nk = rk.shape[1]
    once = pl.Buffered(1)
    tok = pl.BlockSpec((tt, d), lambda i, j: (i, 0), pipeline_mode=once)
    exp = pl.BlockSpec((et, d), lambda i, j: (j, 0))
    rt = pl.BlockSpec((PEER_HEADS, nk, tt), lambda i, j: (0, 0, i), pipeline_mode=once)
    if gate.shape[1] == 1:
        gspec = pl.BlockSpec((1, 1, d), lambda i, j: (i // tiles_per_mod, 0, 0), pipeline_mode=once)
    else:
        gspec = pl.BlockSpec((1, tt, d), lambda i, j: (i, 0, 0), pipeline_mode=once)
    return pl.pallas_call(
        _peer_dense_body,
        out_shape=jax.ShapeDtypeStruct((t, d), F32),
        grid=(t // tt, ne // et),
        in_specs=[tok, exp, exp, rt, rt, rt, rt, tok, gspec],
        out_specs=pl.BlockSpec((tt, d), lambda i, j: (i, 0)),
        scratch_shapes=[pltpu.VMEM((et, tt), BF16), pltpu.VMEM((et // nk, PEER_HEADS, BF16_ROWS, tt), BF16),
                        pltpu.VMEM((et // nk, PEER_HEADS, BF16_ROWS, tt), BF16)],
        compiler_params=_cp(("arbitrary", "arbitrary")),
        name="peer_dense",
    )(h2, u, v, rk, lim, e1, e2, x1, gate)


def _slope_lanes(tq):
    h = np.arange(1, N_HEADS + 1, dtype=np.float32)
    s = (2.0 ** (-8.0 * h / N_HEADS)).reshape(N_KV, GROUP)
    return jnp.asarray(np.repeat(s.T.reshape(-1), tq)[None, :], F32)


def _frac_t(nc, ns, nc_pad, ns_pad):
    pos = np.arange(nc)[:, None] * CMP_STRIDE + np.arange(CMP_BLOCK)[None, :]
    f = ((pos // SEL_BLOCK)[:, :, None] == np.arange(ns)[None, None, :]).mean(axis=1)
    out = np.zeros((ns_pad, nc_pad), np.float32)
    out[:ns, :nc] = f.T
    return jnp.asarray(out)


def _gate_expand(width):
    e = np.zeros((3, width, N_HEADS * HEAD_DIM), np.float32)
    for g in range(N_KV):
        for r in range(GROUP):
            for j in range(3):
                c0 = r * KV_W + g * HEAD_DIM
                e[j, (g * GROUP + r) * 3 + j, c0:c0 + HEAD_DIM] = 1.0
    return jnp.asarray(e, BF16)


def _rgd(a, axis):
    shp = a.shape
    a = a.reshape(shp[:axis] + (N_KV, GROUP, HEAD_DIM) + shp[axis + 1:])
    a = jnp.swapaxes(a, axis, axis + 1)
    return a.reshape(shp)


def kernel(x_prompt, x_sample, cache_kv_cmp, cache_kv_slc, cache_kv_win, state_pool, page_table, c_prompt, c_sample,
           w_ada, b_ada, g_norm1, g_norm2, w_in, w_out, g_qnorm, g_knorm, cmp_pe, w_cmp, w_pool, pool_scale,
           peer_wq, peer_subkeys, peer_u, peer_v):
    bp, tp, dm = x_prompt.shape
    bs, ts, _ = x_sample.shape
    n_pages = page_table.shape[1]
    page = cache_kv_cmp.shape[1]
    past = n_pages * page
    nsa_w = N_HEADS * HEAD_DIM
    kv3 = 3 * 2 * KV_W
    ngl = 3 * N_HEADS
    gpad = 128
    tm, tq, sel_bucket, tt, et = 256, 64, 4, 512, 1024

    wq = _rgd(w_in[:, :nsa_w], 1).astype(BF16)
    wkv = w_in[:, nsa_w:nsa_w + kv3].astype(BF16)
    wg = jnp.pad(w_in[:, nsa_w + kv3:nsa_w + kv3 + ngl], ((0, 0), (0, gpad - ngl))).astype(BF16)
    wu = w_in[:, nsa_w + kv3 + ngl:].astype(BF16)
    gq = jnp.tile(g_qnorm, N_HEADS)[None, :]
    gk = jnp.tile(g_knorm, (1, N_KV))
    g1 = g_norm1[None, :]
    g2 = g_norm2[None, :]
    wn = _rgd(w_out[:nsa_w], 0).astype(BF16)
    wp = w_out[nsa_w:].astype(BF16)
    wpool = w_pool.astype(BF16)
    ps = pool_scale[None, :]
    wc = jnp.einsum('cglde,gh->clgdhe', w_cmp, jnp.eye(N_KV, dtype=F32)).reshape(2, CMP_BLOCK, KV_W, KV_W).astype(BF16)
    pe = jnp.transpose(cmp_pe, (1, 0, 2, 3)).reshape(CMP_BLOCK, 2 * KV_W)
    e_gate = _gate_expand(gpad)
    pwq = peer_wq.astype(BF16)
    psk = peer_subkeys.astype(BF16)
    pu = peer_u.astype(BF16)
    pv = peer_v.astype(BF16)

    mod = adaln(jnp.concatenate([c_prompt, c_sample], axis=0), w_ada, b_ada).reshape(bp + bs, 6, dm)
    mod_p = [mod[:bp, k][:, None, :] for k in range(6)]
    mod_s = [jnp.repeat(mod[bp:, k], ts, axis=0).reshape(bs * ts // tm, tm, dm) for k in range(6)]

    xp = x_prompt.reshape(bp * tp, dm)
    q, kvc, kvs, kvw, gates, u, kvc_t, kvs_t, kvw_t = inproj(xp, mod_p[0], mod_p[1], g1, wq, wkv, wg, wu, gq, gk, tm,
                                                             tp // tm, seq_len=tp)
    kc, vc = compress_prompt(kvc.reshape(bp, tp // CMP_STRIDE, CMP_STRIDE * 2 * KV_W), pe, wc, gk)
    nc = (tp - CMP_BLOCK) // CMP_STRIDE + 1
    kvw3 = kvw.reshape(bp, tp, 2 * KV_W)
    o_nsa = nsa_prompt(q, gates, kc, vc, kvs.reshape(bp, tp, 2 * KV_W), jnp.pad(kvw3, ((0, 0), (WINDOW, 0), (0, 0))),
                       _slope_lanes(tq), _frac_t(nc, tp // SEL_BLOCK, tp // CMP_STRIDE, 40), e_gate, tq, sel_bucket)
    u3 = u.reshape(bp, tp, -1)
    d_pool = pool_diff(jnp.pad(u3, ((0, 0), (POOL_BUF, 0), (0, 0))), 0).reshape(bp * tp, -1)
    x1, h2 = outproj(o_nsa, d_pool, xp, mod_p[2], mod_p[3], mod_p[4], g2, wn, wp, wpool, ps, tm, tp // tm)
    routes = peer_route(h2, pwq, psk, tt)
    y_prompt = peer_dense(h2, pu, pv, *routes, x1, mod_p[5], tt, et, tp // tt).reshape(bp, tp, dm)
    rows_major = lambda a: jnp.transpose(a.reshape(bp, 2, N_KV, HEAD_DIM, -1), (0, 4, 1, 2, 3))
    win_p = rows_major(kvw_t[:, :, -min(WINDOW, tp):])
    pool_p = u3[:, -POOL_BUF:]

    xs = x_sample.reshape(bs * ts, dm)
    q, kvc_s, kvs_s, kvw_s, gates, u = inproj(xs, mod_s[0], mod_s[1], g1, wq, wkv, wg, wu, gq, gk, tm, 1)
    feat_major = lambda c: jnp.transpose(c, (0, 2, 3, 4, 1)).reshape(c.shape[0], 2 * KV_W, c.shape[1])
    kc, vc = compress_sample(page_table, feat_major(cache_kv_cmp), pe, wc, gk)
    nc = (past + ts - CMP_BLOCK) // CMP_STRIDE + 1
    ns = -(-(past + ts) // SEL_BLOCK)
    kvw_s3 = kvw_s.reshape(bs, ts, 2 * KV_W)
    o_nsa = nsa_sample(page_table, q.reshape(bs, ts, -1), gates.reshape(bs, ts, -1), kc, vc,
                       feat_major(cache_kv_slc), kvs_s.reshape(bs, ts, 2 * KV_W), feat_major(cache_kv_win), kvw_s3,
                       _slope_lanes(ts), _frac_t(nc, ns, past // CMP_STRIDE, 40), e_gate).reshape(bs * ts, -1)
    ext = jnp.concatenate([state_pool, u.reshape(bs, ts, -1)], axis=1)
    d_pool = pool_diff(ext, past, bb=8).reshape(bs * ts, -1)
    x1, h2 = outproj(o_nsa, d_pool, xs, mod_s[2], mod_s[3], mod_s[4], g2, wn, wp, wpool, ps, tm, 1)
    routes = peer_route(h2, pwq, psk, tt)
    gate5 = mod_s[5].reshape(bs * ts // tt, tt, dm)
    y_sample = peer_dense(h2, pu, pv, *routes, x1, gate5, tt, et, 1).reshape(bs, ts, dm)
    shp_s = (bs, ts, 2, N_KV, HEAD_DIM)
    wbuf = cache_kv_win.shape[1]
    win_s = jnp.concatenate([cache_kv_win, kvw_s.reshape(shp_s)], axis=1)[:, -wbuf:]
    pool_s = ext[:, -POOL_BUF:]

    return (y_prompt, y_sample, rows_major(kvc_t), rows_major(kvs_t), win_p, pool_p,
            kvc_s.reshape(shp_s), kvs_s.reshape(shp_s), win_s, pool_s)
```

```python
import functools

import numpy as np
import jax
import jax.numpy as jnp
from jax import lax
from jax.experimental import pallas as pl
from jax.experimental.pallas import tpu as pltpu

F32, BF16, I32 = jnp.float32, jnp.bfloat16, jnp.int32

HEAD_DIM = 64
N_KV = 4
GROUP = 4
N_HEADS = N_KV * GROUP
CMP_BLOCK = 32
CMP_STRIDE = 16
SEL_BLOCK = 64
SEL_TOP = 16
WINDOW = 512
POOL_SIZES = (2, 4, 8, 16)
POOL_BUF = max(POOL_SIZES) - 1
PEER_HEADS = 8
N_KEYS = 128
PEER_TOPK = 16
EPS = 1e-6
NEG = -1e30
FORCE_BONUS = 1e4
KV_W = N_KV * HEAD_DIM
VMEM_LIMIT = 56 * 1024 * 1024


def _cp(sem, vmem=VMEM_LIMIT):
    return pltpu.CompilerParams(dimension_semantics=sem, vmem_limit_bytes=vmem)


def _dot(a, b):
    return jnp.dot(a, b, preferred_element_type=F32)


def _nt(a, b):
    return lax.dot_general(a, b, (((1,), (1,)), ((), ())), preferred_element_type=F32)


def _tn(a, b):
    return lax.dot_general(a, b, (((0,), (0,)), ((), ())), preferred_element_type=F32)


def _split_dot(x, w, pieces=3):
    hi = x.astype(BF16)
    r1 = x - hi.astype(F32)
    mid = r1.astype(BF16)
    if pieces == 2:
        return _dot(hi, w) + _dot(mid, w)
    lo = (r1 - mid.astype(F32)).astype(BF16)
    return _dot(hi, w) + _dot(mid, w) + _dot(lo, w)


def _ones64():
    r = lax.broadcasted_iota(I32, (KV_W, KV_W), 0) // HEAD_DIM
    c = lax.broadcasted_iota(I32, (KV_W, KV_W), 1) // HEAD_DIM
    return (r == c).astype(BF16)


def _rms64(x, ones):
    ms = _split_dot(x * x, ones, pieces=2) * (1.0 / HEAD_DIM)
    return x * lax.rsqrt(ms + EPS)


def _rms_rows(x):
    return x * lax.rsqrt(jnp.mean(x * x, axis=-1, keepdims=True) + EPS)


def _adaln_body(c_ref, w_ref, b_ref, o_ref):
    c = c_ref[...]
    s = (c * jax.nn.sigmoid(c)).astype(BF16)
    o_ref[...] = _dot(s, w_ref[...].astype(BF16)) + b_ref[...]


def adaln(c_all, w_ada, b_ada, tn=1024):
    m, d = c_all.shape
    n = w_ada.shape[1]
    return pl.pallas_call(
        _adaln_body,
        out_shape=jax.ShapeDtypeStruct((m, n), F32),
        grid=(n // tn,),
        in_specs=[pl.BlockSpec((m, d), lambda j: (0, 0)),
                  pl.BlockSpec((d, tn), lambda j: (0, j)),
                  pl.BlockSpec((1, tn), lambda j: (0, j))],
        out_specs=pl.BlockSpec((m, tn), lambda j: (0, j)),
        compiler_params=_cp(("arbitrary",)),
        name="adaln",
    )(c_all, w_ada, b_ada.reshape(1, n))


def _inproj_body(x_ref, sh_ref, sc_ref, g1_ref, wq_ref, wkv_ref, wg_ref, wu_ref, gq_ref, gk_ref,
                 q_ref, kvc_ref, kvs_ref, kvw_ref, gt_ref, u_ref, *t_refs):
    h = _rms_rows(x_ref[...]) * g1_ref[...]
    h = h * (1.0 + sc_ref[0]) + sh_ref[0]
    hb = h.astype(BF16)
    ones = _ones64()
    zq = _dot(hb, wq_ref[...])
    for r in range(GROUP):
        sl = slice(r * KV_W, (r + 1) * KV_W)
        q_ref[:, sl] = _rms64(zq[:, sl], ones) * gq_ref[:, sl] * (HEAD_DIM ** -0.5)
    zkv = _dot(hb, wkv_ref[...])
    kvc_ref[...] = zkv[:, 0:2 * KV_W]
    kvs_ref[:, 0:KV_W] = _rms64(zkv[:, 2 * KV_W:3 * KV_W], ones) * gk_ref[1:2, :]
    kvs_ref[:, KV_W:] = zkv[:, 3 * KV_W:4 * KV_W]
    kvw_ref[:, 0:KV_W] = _rms64(zkv[:, 4 * KV_W:5 * KV_W], ones) * gk_ref[2:3, :]
    kvw_ref[:, KV_W:] = zkv[:, 5 * KV_W:6 * KV_W]
    gt_ref[...] = jax.nn.sigmoid(_dot(hb, wg_ref[...]))
    u_ref[...] = _dot(hb, wu_ref[...])
    for src, dst in zip((kvc_ref, kvs_ref, kvw_ref), t_refs):
        dst[0] = src[...].T


def _mod_spec(arr, tm, rows_per_mod):
    d = arr.shape[-1]
    if arr.shape[1] == 1:
        return pl.BlockSpec((1, 1, d), lambda i: (i // rows_per_mod, 0, 0))
    return pl.BlockSpec((1, tm, d), lambda i: (i, 0, 0))


def inproj(x, shift, scale, g1, wq, wkv, wg, wu, gq, gk, tm, tiles_per_mod, seq_len=None):
    t, d = x.shape
    nq, nkv, ng, nu = wq.shape[1], wkv.shape[1], wg.shape[1], wu.shape[1]
    full = lambda a: pl.BlockSpec(a.shape, lambda i: (0,) * a.ndim)
    row = lambda n: pl.BlockSpec((tm, n), lambda i: (i, 0))
    kvrow = jax.ShapeDtypeStruct((t, 2 * KV_W), F32)
    out_shape = [jax.ShapeDtypeStruct((t, nq), F32), kvrow, kvrow, kvrow,
                 jax.ShapeDtypeStruct((t, ng), F32), jax.ShapeDtypeStruct((t, nu), F32)]
    out_specs = [row(nq), row(2 * KV_W), row(2 * KV_W), row(2 * KV_W), row(ng), row(nu)]
    if seq_len is not None:
        tiles = seq_len // tm
        out_shape += [jax.ShapeDtypeStruct((t // seq_len, 2 * KV_W, seq_len), F32)] * 3
        out_specs += [pl.BlockSpec((1, 2 * KV_W, tm), lambda i: (i // tiles, 0, i % tiles))] * 3
    return pl.pallas_call(
        _inproj_body,
        out_shape=tuple(out_shape),
        grid=(t // tm,),
        in_specs=[row(d), _mod_spec(shift, tm, tiles_per_mod), _mod_spec(scale, tm, tiles_per_mod), full(g1),
                  full(wq), full(wkv), full(wg), full(wu), full(gq), full(gk)],
        out_specs=tuple(out_specs),
        compiler_params=_cp(("arbitrary",)),
        name="inproj",
    )(x, shift, scale, g1, wq, wkv, wg, wu, gq, gk)


def _compress(get_x, nchunk, pe_ref, w_ref, gk_ref, kc_ref, vc_ref):
    half = CMP_BLOCK // 2
    acc = [jnp.zeros((nchunk, KV_W), F32) for _ in range(4)]
    for l in range(half):
        xl = get_x(l)
        a = (xl + pe_ref[l:l + 1, :]).astype(BF16)
        b = (xl + pe_ref[half + l:half + l + 1, :]).astype(BF16)
        acc[0] += _dot(a[:, :KV_W], w_ref[0, l])
        acc[1] += _dot(a[:, KV_W:], w_ref[1, l])
        acc[2] += _dot(b[:, :KV_W], w_ref[0, half + l])
        acc[3] += _dot(b[:, KV_W:], w_ref[1, half + l])
    rio = lax.broadcasted_iota(I32, (nchunk, KV_W), 0)
    nxt = lambda v: jnp.where(rio < nchunk - 1, pltpu.roll(v, nchunk - 1, 0), 0.0)
    ck = acc[0] + nxt(acc[2])
    cv = acc[1] + nxt(acc[3])
    kc_ref[...] = _rms64(ck, _ones64()) * gk_ref[0:1, :]
    vc_ref[...] = cv


def _compress_prompt_body(x_ref, pe_ref, w_ref, gk_ref, kc_ref, vc_ref):
    row_w = 2 * KV_W
    _compress(lambda l: x_ref[0, :, l * row_w:(l + 1) * row_w], x_ref.shape[1], pe_ref, w_ref, gk_ref,
              kc_ref.at[0], vc_ref.at[0])


def compress_prompt(kv_chunks, pe, wc, gk):
    b, nchunk, width = kv_chunks.shape
    full = lambda a: pl.BlockSpec(a.shape, lambda i: (0,) * a.ndim)
    out = jax.ShapeDtypeStruct((b, nchunk, KV_W), F32)
    ospec = pl.BlockSpec((1, nchunk, KV_W), lambda i: (i, 0, 0))
    return pl.pallas_call(
        _compress_prompt_body,
        out_shape=(out, out),
        grid=(b,),
        in_specs=[pl.BlockSpec((1, nchunk, width), lambda i: (i, 0, 0)), full(pe), full(wc), full(gk)],
        out_specs=(ospec, ospec),
        compiler_params=_cp(("arbitrary",)),
        name="compress_prompt",
    )(kv_chunks, pe, wc, gk)


def _page_copy(cache_ref, buf_ref, sem_ref, pt_ref, seq, slot, j):
    return pltpu.make_async_copy(cache_ref.at[pt_ref[seq, j]], buf_ref.at[slot, j], sem_ref.at[slot])


def _gather_pages(cache_ref, buf_ref, sem_ref, pt_ref):
    b = pl.program_id(0)
    nb = pl.num_programs(0)
    n_pages = pt_ref.shape[1]
    slot = b % 2

    def start(seq, s):
        for j in range(n_pages):
            _page_copy(cache_ref, buf_ref, sem_ref, pt_ref, seq, s, j).start()

    @pl.when(b == 0)
    def _():
        start(b, slot)

    @pl.when(b + 1 < nb)
    def _():
        start(b + 1, 1 - slot)

    for j in range(n_pages):
        _page_copy(cache_ref, buf_ref, sem_ref, pt_ref, b, slot, j).wait()
    return slot


LANES = 128
BF16_ROWS = 16


def _compress_sample_body(pt_ref, cache_ref, pe_ref, w_ref, gk_ref, kc_ref, vc_ref, buf_ref, xs_ref, sem_ref):
    n_pages, page = pt_ref.shape[1], cache_ref.shape[2]
    slot = _gather_pages(cache_ref, buf_ref, sem_ref, pt_ref)
    nlb = xs_ref.shape[0]
    for j in range(n_pages):
        for k in range(nlb):
            xs_ref[k, pl.ds(j * page, page), :] = buf_ref[slot, j, pl.ds(k * LANES, LANES), :].T
    nchunk = n_pages * page // CMP_STRIDE
    get_x = lambda l: jnp.concatenate([xs_ref[k, pl.ds(l, nchunk, stride=CMP_STRIDE), :] for k in range(nlb)], axis=1)
    _compress(get_x, nchunk, pe_ref, w_ref, gk_ref, kc_ref.at[0], vc_ref.at[0])


def compress_sample(page_table, cache_t, pe, wc, gk):
    b, n_pages = page_table.shape
    _, width, page = cache_t.shape
    nchunk = n_pages * page // CMP_STRIDE
    full = lambda a: pl.BlockSpec(a.shape, lambda i, pt: (0,) * a.ndim)
    out = jax.ShapeDtypeStruct((b, nchunk, KV_W), F32)
    ospec = pl.BlockSpec((1, nchunk, KV_W), lambda i, pt: (i, 0, 0))
    return pl.pallas_call(
        _compress_sample_body,
        out_shape=(out, out),
        grid_spec=pltpu.PrefetchScalarGridSpec(
            num_scalar_prefetch=1, grid=(b,),
            in_specs=[pl.BlockSpec(memory_space=pl.ANY), full(pe), full(wc), full(gk)],
            out_specs=(ospec, ospec),
            scratch_shapes=[pltpu.VMEM((2, n_pages, width, page), F32),
                            pltpu.VMEM((width // LANES, n_pages * page, LANES), F32),
                            pltpu.SemaphoreType.DMA((2,))]),
        compiler_params=_cp(("arbitrary",)),
        name="compress_sample",
    )(page_table, cache_t, pe, wc, gk)


def _softmax_cols(s, valid):
    s = jnp.where(valid, s, NEG)
    m = jnp.max(s, axis=0, keepdims=True)
    e = jnp.where(valid, jnp.exp(s - m), 0.0)
    return e / jnp.maximum(jnp.sum(e, axis=0, keepdims=True), 1e-30)


def _nsa_front(q, kc, vc, pq0, slope, frac_t, tq, ns):
    cols = N_HEADS * tq
    lane = lax.broadcasted_iota(I32, (1, cols), 1)
    pq = (pq0 + lane % tq).astype(F32)
    lg = lax.broadcasted_iota(I32, (tq, KV_W), 1) // HEAD_DIM
    qfull = jnp.concatenate([jnp.where(lg == g, q[:, r * KV_W:(r + 1) * KV_W], 0.0)
                             for r in range(GROUP) for g in range(N_KV)], axis=0).astype(BF16)

    nc = kc.shape[0]
    pos_c = (lax.broadcasted_iota(I32, (nc, cols), 0) * CMP_STRIDE + (CMP_BLOCK - 1)).astype(F32)
    dist_c = pq - pos_c
    p_c = _softmax_cols(_nt(kc.astype(BF16), qfull) - slope * dist_c, dist_c >= 0).astype(BF16)
    o_c = _tn(p_c, vc.astype(BF16))

    imp = _dot(frac_t.astype(BF16), p_c)
    imp = imp + pltpu.roll(imp, 4 * tq, 1) + pltpu.roll(imp, 8 * tq, 1) + pltpu.roll(imp, 12 * tq, 1)
    gq = N_KV * tq
    compact = gq % LANES == 0
    wsel = gq if compact else cols
    imp = imp[:, :wsel]
    nsp = imp.shape[0]
    jio = lax.broadcasted_iota(I32, (nsp, wsel), 0)
    pos_q = pq0 + lax.broadcasted_iota(I32, (1, wsel), 1) % tq
    blk_q = pos_q // SEL_BLOCK
    forced = (jio == 0) | (jio == blk_q) | (jio == blk_q - 1)
    valid_blk = jio * SEL_BLOCK <= pos_q
    imp = jnp.where(valid_blk, imp + FORCE_BONUS * forced.astype(F32), NEG)
    rank = jnp.zeros((nsp, wsel), I32)
    for i in range(ns):
        row = imp[i:i + 1, :]
        rank = rank + ((row > imp) | ((row == imp) & (i < jio))).astype(I32)
    sel_bias = jnp.where(rank < min(SEL_TOP, ns), 0.0, NEG)
    if compact:
        sel_bias = jnp.concatenate([sel_bias] * GROUP, axis=1)
    return qfull, pq, o_c, sel_bias


def _as_column(row):
    return jnp.transpose(jnp.broadcast_to(row, (LANES, row.shape[1])))[:, 0:1]


def _nsa_selected(qfull, pq, sel_bias, ks_ref, slope, nblk, blk_q):
    cols = qfull.shape[0]
    nsp = sel_bias.shape[0]
    tk = nblk * SEL_BLOCK
    jio = lax.broadcasted_iota(I32, (nsp, cols), 0)
    bias = jnp.where(jio < blk_q, sel_bias, NEG) + slope * (jio * SEL_BLOCK).astype(F32)
    inblk = slope * lax.broadcasted_iota(I32, (SEL_BLOCK, cols), 0).astype(F32)
    s_m = _nt(ks_ref[0:tk, 0:KV_W].astype(BF16), qfull).reshape(nblk, SEL_BLOCK, cols)
    s_m = s_m + inblk[None, :, :] + bias[:nblk][:, None, :]

    d0 = blk_q * SEL_BLOCK
    kd = ks_ref[pl.ds(d0 if isinstance(d0, int) else pl.multiple_of(d0, SEL_BLOCK), SEL_BLOCK), :]
    kpos_d = (blk_q * SEL_BLOCK + lax.broadcasted_iota(I32, (SEL_BLOCK, cols), 0)).astype(F32)
    valid_d = kpos_d <= pq
    s_d = jnp.where(valid_d, _nt(kd[:, :KV_W].astype(BF16), qfull) + slope * kpos_d, NEG)

    m = jnp.maximum(jnp.max(jnp.max(s_m, axis=0), axis=0, keepdims=True), jnp.max(s_d, axis=0, keepdims=True))
    e_m = jnp.exp(s_m - m[None, :, :])
    e_d = jnp.where(valid_d, jnp.exp(s_d - m), 0.0)
    l = jnp.sum(jnp.sum(e_m, axis=0), axis=0, keepdims=True) + jnp.sum(e_d, axis=0, keepdims=True)
    o = _tn(e_m.astype(BF16).reshape(tk, cols), ks_ref[0:tk, KV_W:].astype(BF16)) + _tn(e_d.astype(BF16),
                                                                                       kd[:, KV_W:].astype(BF16))
    return o * _as_column(1.0 / jnp.maximum(l, 1e-30))


def _nsa_window(qfull, pq, kw, vw, pw0, slope):
    tw, cols = kw.shape[0], qfull.shape[0]
    pos_w = (pw0 + lax.broadcasted_iota(I32, (tw, cols), 0)).astype(F32)
    dist_w = pq - pos_w
    valid_w = (dist_w >= 0) & (dist_w <= WINDOW) & (pos_w >= 0)
    p_w = _softmax_cols(_nt(kw.astype(BF16), qfull) - slope * dist_w, valid_w).astype(BF16)
    return _tn(p_w, vw.astype(BF16))


def _nsa_window_steady(qfull, kw, vw, slope, tq):
    cols = qfull.shape[0]
    nb = WINDOW // tq
    qq = lax.broadcasted_iota(I32, (1, cols), 1) % tq
    rio = lax.broadcasted_iota(I32, (tq, cols), 0)
    s = _nt(kw.astype(BF16), qfull).reshape(nb + 1, tq, cols) + (slope * rio.astype(F32))[None, :, :]
    s = s + (slope * (lax.broadcasted_iota(I32, (nb + 1, cols), 0) * tq).astype(F32))[:, None, :]
    ok_f, ok_l = rio >= qq, rio <= qq
    s_f, s_m, s_l = jnp.where(ok_f, s[0], NEG), s[1:nb], jnp.where(ok_l, s[nb], NEG)
    m = jnp.maximum(jnp.max(jnp.max(s_m, axis=0), axis=0, keepdims=True),
                    jnp.max(jnp.maximum(s_f, s_l), axis=0, keepdims=True))
    e_f = jnp.where(ok_f, jnp.exp(s_f - m), 0.0)
    e_l = jnp.where(ok_l, jnp.exp(s_l - m), 0.0)
    e_m = jnp.exp(s_m - m[None, :, :])
    l = jnp.sum(jnp.sum(e_m, axis=0), axis=0, keepdims=True) + jnp.sum(e_f + e_l, axis=0, keepdims=True)
    v = vw.astype(BF16)
    o = (_tn(e_m.astype(BF16).reshape((nb - 1) * tq, cols), v[tq:nb * tq]) + _tn(e_f.astype(BF16), v[0:tq])
         + _tn(e_l.astype(BF16), v[nb * tq:]))
    return o * _as_column(1.0 / jnp.maximum(l, 1e-30))


def _nsa_combine(gates, e_gate, outs, tq):
    cols = N_HEADS * tq
    row_g = (lax.broadcasted_iota(I32, (cols, KV_W), 0) // tq) % N_KV
    own = row_g == lax.broadcasted_iota(I32, (cols, KV_W), 1) // HEAD_DIM

    def fold(o):
        o = jnp.where(own, o, 0.0)
        return jnp.concatenate(
            [sum(o[(r * N_KV + g) * tq:(r * N_KV + g + 1) * tq, :] for g in range(N_KV)) for r in range(GROUP)], axis=1)

    out = jnp.zeros((tq, N_HEADS * HEAD_DIM), F32)
    for j, o in enumerate(outs):
        out = out + _split_dot(gates, e_gate[j]) * fold(o)
    return out


def _nsa_prompt_body(q_ref, gt_ref, kc_ref, vc_ref, ks_ref, kw_ref, sl_ref, fr_ref, eg_ref, o_ref, os_ref, ow_ref,
                     *, tq, ns, tw, bucket):
    q0 = pl.program_id(1) * tq
    slope = sl_ref[...]
    qfull, pq, o_c, sel_bias = _nsa_front(q_ref[...], kc_ref[0], vc_ref[0], q0, slope, fr_ref[...], tq, ns)
    blk_q = q0 // SEL_BLOCK
    for k in range(ns // bucket):
        @pl.when(blk_q // bucket == k)
        def _():
            os_ref[...] = _nsa_selected(qfull, pq, sel_bias, ks_ref.at[0], slope, (k + 1) * bucket, blk_q)
    w = kw_ref[0, pl.ds(pl.multiple_of(q0, 8), tw), :]

    @pl.when(q0 >= WINDOW)
    def _():
        ow_ref[...] = _nsa_window_steady(qfull, w[:, :KV_W], w[:, KV_W:], slope, tq)

    @pl.when(q0 < WINDOW)
    def _():
        ow_ref[...] = _nsa_window(qfull, pq, w[:, :KV_W], w[:, KV_W:], q0 - WINDOW, slope)

    o_ref[...] = _nsa_combine(gt_ref[...], eg_ref, (o_c, os_ref[...], ow_ref[...]), tq)


def nsa_prompt(q, gates, kc, vc, kvs, kvw_pad, slope, frac_t, e_gate, tq, bucket):
    b, t, _ = kvs.shape
    ns = t // SEL_BLOCK
    assert SEL_BLOCK % tq == 0 and ns % bucket == 0 and t % SEL_BLOCK == 0
    assert WINDOW % tq == 0 and tq % 8 == 0
    tw = WINDOW + tq
    nq = t // tq
    full = lambda a: pl.BlockSpec(a.shape, lambda i, j: (0,) * a.ndim)
    perb = lambda a: pl.BlockSpec((1,) + a.shape[1:], lambda i, j: (i, 0, 0))
    row = lambda n: pl.BlockSpec((tq, n), lambda i, j: (i * nq + j, 0))
    return pl.pallas_call(
        functools.partial(_nsa_prompt_body, tq=tq, ns=ns, tw=tw, bucket=bucket),
        out_shape=jax.ShapeDtypeStruct((b * t, N_HEADS * HEAD_DIM), F32),
        grid=(b, nq),
        in_specs=[row(q.shape[1]), row(gates.shape[1]), perb(kc), perb(vc), perb(kvs), perb(kvw_pad),
                  full(slope), full(frac_t), full(e_gate)],
        out_specs=row(N_HEADS * HEAD_DIM),
        scratch_shapes=[pltpu.VMEM((N_HEADS * tq, KV_W), F32), pltpu.VMEM((N_HEADS * tq, KV_W), F32)],
        compiler_params=_cp(("arbitrary", "arbitrary")),
        name="nsa_prompt",
    )(q, gates, kc, vc, kvs, kvw_pad, slope, frac_t, e_gate)


def _nsa_sample_body(pt_ref, q_ref, gt_ref, kc_ref, vc_ref, cache_ref, knew_ref, win_ref, wnew_ref, sl_ref, fr_ref,
                     eg_ref, o_ref, buf_ref, ks_ref, sem_ref, *, tq, ns, past):
    n_pages, page = pt_ref.shape[1], cache_ref.shape[2]
    tk = ns * SEL_BLOCK

    @pl.when(pl.program_id(0) == 0)
    def _():
        ks_ref[past:, :] = jnp.zeros((tk - past, ks_ref.shape[1]), F32)

    slot = _gather_pages(cache_ref, buf_ref, sem_ref, pt_ref)
    for j in range(n_pages):
        ks_ref[pl.ds(j * page, page), :] = buf_ref[slot, j].T
    ks_ref[past:past + tq, :] = knew_ref[0]
    wbuf = win_ref.shape[2]
    tw = wbuf + SEL_BLOCK
    w = jnp.concatenate([win_ref[0].T, wnew_ref[0], jnp.zeros((tw - wbuf - tq, 2 * KV_W), F32)], axis=0)
    slope = sl_ref[...]
    qfull, pq, o_c, sel_bias = _nsa_front(q_ref[0], kc_ref[0], vc_ref[0], past, slope, fr_ref[...], tq, ns)
    o_s = _nsa_selected(qfull, pq, sel_bias, ks_ref, slope, past // SEL_BLOCK, past // SEL_BLOCK)
    o_w = _nsa_window(qfull, pq, w[:, :KV_W], w[:, KV_W:], past - wbuf, slope)
    o_ref[0] = _nsa_combine(gt_ref[0], eg_ref, (o_c, o_s, o_w), tq)


def nsa_sample(page_table, q, gates, kc, vc, cache_t, kvs_new, win_t, kvw_new, slope, frac_t, e_gate):
    b, n_pages = page_table.shape
    _, width, page = cache_t.shape
    past = n_pages * page
    tq = q.shape[1]
    assert past % SEL_BLOCK + tq <= SEL_BLOCK
    ns = -(-(past + tq) // SEL_BLOCK)
    tk = ns * SEL_BLOCK
    full = lambda a: pl.BlockSpec(a.shape, lambda i, pt: (0,) * a.ndim)
    perb = lambda a: pl.BlockSpec((1,) + a.shape[1:], lambda i, pt: (i, 0, 0))
    return pl.pallas_call(
        functools.partial(_nsa_sample_body, tq=tq, ns=ns, past=past),
        out_shape=jax.ShapeDtypeStruct((b, tq, N_HEADS * HEAD_DIM), F32),
        grid_spec=pltpu.PrefetchScalarGridSpec(
            num_scalar_prefetch=1, grid=(b,),
            in_specs=[perb(q), perb(gates), perb(kc), perb(vc), pl.BlockSpec(memory_space=pl.ANY), perb(kvs_new),
                      perb(win_t), perb(kvw_new), full(slope), full(frac_t), full(e_gate)],
            out_specs=pl.BlockSpec((1, tq, N_HEADS * HEAD_DIM), lambda i, pt: (i, 0, 0)),
            scratch_shapes=[pltpu.VMEM((2, n_pages, width, page), F32), pltpu.VMEM((tk, width), F32),
                            pltpu.SemaphoreType.DMA((2,))]),
        compiler_params=_cp(("arbitrary",)),
        name="nsa_sample",
    )(page_table, q, gates, kc, vc, cache_t, kvs_new, win_t, kvw_new, slope, frac_t, e_gate)


def _pool_diff_body(ext_ref, d_ref, *, t, pos0):
    gw = ext_ref.shape[2] // len(POOL_SIZES)
    pos = pos0 + lax.broadcasted_iota(I32, (t, gw), 0)
    for s in range(ext_ref.shape[0]):
        for gi, w in enumerate(POOL_SIZES):
            lanes = pl.ds(gi * gw, gw)
            cur = ext_ref[s, pl.ds(POOL_BUF, t), lanes]
            acc = cur
            for j in range(1, w):
                acc = acc + ext_ref[s, pl.ds(POOL_BUF - j, t), lanes]
            cnt = jnp.minimum(w, pos + 1).astype(F32)
            d_ref[s, :, lanes] = acc / cnt - cur


def pool_diff(ext, pos0, bb=1):
    b, rows, width = ext.shape
    t = rows - POOL_BUF
    return pl.pallas_call(
        functools.partial(_pool_diff_body, t=t, pos0=pos0),
        out_shape=jax.ShapeDtypeStruct((b, t, width), F32),
        grid=(b // bb,),
        in_specs=[pl.BlockSpec((bb, rows, width), lambda i: (i, 0, 0))],
        out_specs=pl.BlockSpec((bb, t, width), lambda i: (i, 0, 0)),
        compiler_params=_cp(("arbitrary",)),
        name="pool_diff",
    )(ext)


def _outproj_body(o_ref, d_ref, x_ref, gate_ref, sh_ref, sc_ref, g2_ref, wn_ref, wp_ref, wpool_ref, ps_ref,
                  x1_ref, h2_ref):
    gw = wpool_ref.shape[1]
    d = d_ref[...]
    yp = jnp.concatenate([_dot(d[:, g * gw:(g + 1) * gw].astype(BF16), wpool_ref[g]) for g in range(len(POOL_SIZES))],
                         axis=1) * ps_ref[...]
    mix = _dot(o_ref[...].astype(BF16), wn_ref[...]) + _dot(yp.astype(BF16), wp_ref[...])
    x1 = x_ref[...] + gate_ref[0] * mix
    x1_ref[...] = x1
    h2 = _rms_rows(x1) * g2_ref[...]
    h2_ref[...] = (h2 * (1.0 + sc_ref[0]) + sh_ref[0]).astype(BF16)


def outproj(o, d, x, gate, shift, scale, g2, wn, wp, wpool, ps, tm, tiles_per_mod):
    t, dm = x.shape
    full = lambda a: pl.BlockSpec(a.shape, lambda i: (0,) * a.ndim)
    row = lambda n: pl.BlockSpec((tm, n), lambda i: (i, 0))
    ms = lambda a: _mod_spec(a, tm, tiles_per_mod)
    return pl.pallas_call(
        _outproj_body,
        out_shape=(jax.ShapeDtypeStruct((t, dm), F32), jax.ShapeDtypeStruct((t, dm), BF16)),
        grid=(t // tm,),
        in_specs=[row(o.shape[1]), row(d.shape[1]), row(dm), ms(gate), ms(shift), ms(scale), full(g2),
                  full(wn), full(wp), full(wpool), full(ps)],
        out_specs=(row(dm), row(dm)),
        compiler_params=_cp(("arbitrary",)),
        name="outproj",
    )(o, d, x, gate, shift, scale, g2, wn, wp, wpool, ps)


_CAND_PIECES = ((0, 16), (1, 8), (2, 8), (3, 8), (4, 8), (5, 8), (6, 8), (7, 8))


def _top16_rows(s, exact):
    kio = lax.broadcasted_iota(I32, s.shape, 0)
    nk = s.shape[0]
    s0 = s
    rank = jnp.full(s.shape, float(PEER_TOPK), F32)
    vals = []
    for it in range(PEER_TOPK):
        m = jnp.max(s, axis=0, keepdims=True)
        hit = s == m
        if exact:
            hit = kio == jnp.min(jnp.where(hit, kio, nk), axis=0, keepdims=True)
        rank = jnp.where(hit, float(it), rank)
        s = jnp.where(hit, -jnp.inf, s)
        vals.append(m)
    cnt = jnp.sum((s0 >= vals[-1]).astype(F32), axis=0, keepdims=True)
    return vals, rank, cnt


def _route_head(s1, s2, exact):
    v1, rank1, cnt1 = _top16_rows(s1, exact)
    v2, rank2, cnt2 = _top16_rows(s2, exact)
    n = s1.shape[1]
    v2a = jnp.concatenate(v2, axis=0)
    top = v1[0] + v2[0]

    bio = {nb: lax.broadcasted_iota(I32, (nb, n), 0) for nb in (8, PEER_TOPK)}
    v2p = {8: jnp.concatenate(v2[:8], axis=0), PEER_TOPK: v2a}
    cands, flats = [], []
    for a, nb in _CAND_PIECES:
        c = v1[a] + v2p[nb]
        ok = (bio[nb] + 1) * (a + 1) <= PEER_TOPK
        cands.append(jnp.where(ok, c, -jnp.inf))
        flats.append(a * PEER_TOPK + bio[nb])
    v1b = jnp.concatenate(v1[8:], axis=0)
    cands.append(v1b + v2[0])
    flats.append((bio[8] + 8) * PEER_TOPK)
    cand = jnp.concatenate(cands, axis=0)
    flat = jnp.concatenate(flats, axis=0)
    big = PEER_TOPK * PEER_TOPK
    taken = jnp.zeros(cand.shape, jnp.bool_)
    z = jnp.zeros((1, n), F32)
    for _ in range(PEER_TOPK):
        m = jnp.max(cand, axis=0, keepdims=True)
        hit = cand == m
        if exact:
            hit = flat == jnp.min(jnp.where(hit, flat, big), axis=0, keepdims=True)
        taken = taken | hit
        cand = jnp.where(hit, -jnp.inf, cand)
        z = z + jnp.exp(m - top)
    takenf = taken.astype(F32)
    cnt3 = jnp.sum(takenf, axis=0, keepdims=True)

    lim = jnp.zeros(s1.shape, F32)
    off = 0
    for a, nb in _CAND_PIECES:
        lim = jnp.where(rank1 == float(a), jnp.sum(takenf[off:off + nb], axis=0, keepdims=True), lim)
        off += nb
    for i in range(8):
        lim = jnp.where(rank1 == float(8 + i), takenf[off + i:off + i + 1], lim)

    k = float(PEER_TOPK)
    ok = ((cnt1 == k) & (cnt2 == k) & (cnt3 == k)).astype(F32)
    return rank2.astype(BF16), lim, jnp.exp(s1 - v1[0]) / z, jnp.exp(s2 - v2[0]).astype(BF16), ok


def _peer_route_body(h_ref, wq_ref, sk_ref, rk_ref, lim_ref, e1_ref, e2_ref, q_scr):
    step = pl.program_id(1)
    hps, qd = sk_ref.shape[1], sk_ref.shape[3]

    @pl.when(step == 0)
    def _():
        q = _dot(h_ref[...], wq_ref[...]).astype(BF16)
        for i in range(PEER_HEADS):
            q_scr[i] = q[:, i * 2 * qd:(i + 1) * 2 * qd]

    def emit(hh, s1, s2, exact):
        rk, lim, e1, e2, ok = _route_head(s1, s2, exact)
        rk_ref[hh], lim_ref[hh], e1_ref[hh], e2_ref[hh] = rk, lim, e1, e2
        return ok

    heads = []
    for hh in range(hps):
        qh = q_scr[step * hps + hh]
        s1 = _nt(sk_ref[0, hh], qh[:, :qd])
        s2 = _nt(sk_ref[1, hh], qh[:, qd:])
        heads.append((hh, s1, s2, emit(hh, s1, s2, False)))
    for hh, s1, s2, ok in heads:
        @pl.when(jnp.min(ok) < 0.5)
        def _(hh=hh, s1=s1, s2=s2):
            emit(hh, s1, s2, True)


def peer_route(h2, wq, sk, tt):
    t, d = h2.shape
    nk = sk.shape[2]
    out = jax.ShapeDtypeStruct((PEER_HEADS, nk, t), F32)
    outb = jax.ShapeDtypeStruct((PEER_HEADS, nk, t), BF16)
    hps = 4
    ospec = pl.BlockSpec((hps, nk, tt), lambda i, h: (h, 0, i))
    return pl.pallas_call(
        _peer_route_body,
        out_shape=(outb, out, out, outb),
        grid=(t // tt, PEER_HEADS // hps),
        in_specs=[pl.BlockSpec((tt, d), lambda i, h: (i, 0)),
                  pl.BlockSpec(wq.shape, lambda i, h: (0, 0)),
                  pl.BlockSpec((2, hps) + sk.shape[2:], lambda i, h: (0, h, 0, 0))],
        out_specs=(ospec, ospec, ospec, ospec),
        scratch_shapes=[pltpu.VMEM((PEER_HEADS, tt, wq.shape[1] // PEER_HEADS), BF16)],
        compiler_params=_cp(("arbitrary", "arbitrary")),
        name="peer_route",
    )(h2, wq, sk)


def _peer_dense_body(h_ref, u_ref, v_ref, rk_ref, lim_ref, e1_ref, e2_ref, x1_ref, gate_ref, y_ref, w_scr, lb_scr,
                     eb_scr):
    j = pl.program_id(1)
    nk = rk_ref.shape[1]
    et, tt = u_ref.shape[0], h_ref.shape[0]
    rows = lb_scr.shape[2]

    @pl.when(j == 0)
    def _():
        y_ref[...] = jnp.zeros_like(y_ref)

    zero = jnp.zeros((), BF16)
    for rr in range(et // nk):
        r = j * (et // nk) + rr
        for hd in range(PEER_HEADS):
            lb_scr[rr, hd] = jnp.broadcast_to(lim_ref[hd, pl.ds(r, 1), :], (rows, tt)).astype(BF16)
            eb_scr[rr, hd] = jnp.broadcast_to(0.5 * e1_ref[hd, pl.ds(r, 1), :], (rows, tt)).astype(BF16)
    nr = et // nk
    for ch in range(nk // rows):
        sl = pl.ds(ch * rows, rows)
        ws = [jnp.zeros((rows, tt), BF16) for _ in range(nr)]
        for hd in range(PEER_HEADS):
            rk, e2 = rk_ref[hd, sl, :], e2_ref[hd, sl, :]
            for rr in range(nr):
                ws[rr] = ws[rr] + jnp.where(rk < lb_scr[rr, hd], e2, zero) * eb_scr[rr, hd]
        for rr in range(nr):
            w_scr[pl.ds(rr * nk + ch * rows, rows), :] = ws[rr]
    a = _nt(u_ref[...], h_ref[...])
    act2 = a * (lax.erf(a * np.float32(1.0 / np.sqrt(2.0))) + 1.0)
    wt = (act2 * w_scr[...].astype(F32)).astype(BF16)
    y_ref[...] += _tn(wt, v_ref[...])

    @pl.when(j == pl.num_programs(1) - 1)
    def _():
        y_ref[...] = x1_ref[...] + gate_ref[0] * y_ref[...]


def peer_dense(h2, u, v, rk, lim, e1, e2, x1, gate, tt, et, tiles_per_mod):
    t, d = h2.shape
    ne = u.shape[0]
    nk = rk.shape[1]
    once = pl.Buffered(1)
    tok = pl.BlockSpec((tt, d), lambda i, j: (i, 0), pipeline_mode=once)
    exp = pl.BlockSpec((et, d), lambda i, j: (j, 0))
    rt = pl.BlockSpec((PEER_HEADS, nk, tt), lambda i, j: (0, 0, i), pipeline_mode=once)
    if gate.shape[1] == 1:
        gspec = pl.BlockSpec((1, 1, d), lambda i, j: (i // tiles_per_mod, 0, 0), pipeline_mode=once)
    else:
        gspec = pl.BlockSpec((1, tt, d), lambda i, j: (i, 0, 0), pipeline_mode=once)
    return pl.pallas_call(
        _peer_dense_body,
        out_shape=jax.ShapeDtypeStruct((t, d), F32),
        grid=(t // tt, ne // et),
        in_specs=[tok, exp, exp, rt, rt, rt, rt, tok, gspec],
        out_specs=pl.BlockSpec((tt, d), lambda i, j: (i, 0)),
        scratch_shapes=[pltpu.VMEM((et, tt), BF16), pltpu.VMEM((et // nk, PEER_HEADS, BF16_ROWS, tt), BF16),
                        pltpu.VMEM((et // nk, PEER_HEADS, BF16_ROWS, tt), BF16)],
        compiler_params=_cp(("arbitrary", "arbitrary")),
        name="peer_dense",
    )(h2, u, v, rk, lim, e1, e2, x1, gate)


def _slope_lanes(tq):
    h = np.arange(1, N_HEADS + 1, dtype=np.float32)
    s = (2.0 ** (-8.0 * h / N_HEADS)).reshape(N_KV, GROUP)
    return jnp.asarray(np.repeat(s.T.reshape(-1), tq)[None, :], F32)


def _frac_t(nc, ns, nc_pad, ns_pad):
    pos = np.arange(nc)[:, None] * CMP_STRIDE + np.arange(CMP_BLOCK)[None, :]
    f = ((pos // SEL_BLOCK)[:, :, None] == np.arange(ns)[None, None, :]).mean(axis=1)
    out = np.zeros((ns_pad, nc_pad), np.float32)
    out[:ns, :nc] = f.T
    return jnp.asarray(out)


def _gate_expand(width):
    e = np.zeros((3, width, N_HEADS * HEAD_DIM), np.float32)
    for g in range(N_KV):
        for r in range(GROUP):
            for j in range(3):
                c0 = r * KV_W + g * HEAD_DIM
                e[j, (g * GROUP + r) * 3 + j, c0:c0 + HEAD_DIM] = 1.0
    return jnp.asarray(e, BF16)


def _rgd(a, axis):
    shp = a.shape
    a = a.reshape(shp[:axis] + (N_KV, GROUP, HEAD_DIM) + shp[axis + 1:])
    a = jnp.swapaxes(a, axis, axis + 1)
    return a.reshape(shp)


def kernel(x_prompt, x_sample, cache_kv_cmp, cache_kv_slc, cache_kv_win, state_pool, page_table, c_prompt, c_sample,
           w_ada, b_ada, g_norm1, g_norm2, w_in, w_out, g_qnorm, g_knorm, cmp_pe, w_cmp, w_pool, pool_scale,
           peer_wq, peer_subkeys, peer_u, peer_v):
    bp, tp, dm = x_prompt.shape
    bs, ts, _ = x_sample.shape
    n_pages = page_table.shape[1]
    page = cache_kv_cmp.shape[1]
    past = n_pages * page
    nsa_w = N_HEADS * HEAD_DIM
    kv3 = 3 * 2 * KV_W
    ngl = 3 * N_HEADS
    gpad = 128
    tm, tq, sel_bucket, tt, et = 256, 64, 2, 512, 1024

    wq = _rgd(w_in[:, :nsa_w], 1).astype(BF16)
    wkv = w_in[:, nsa_w:nsa_w + kv3].astype(BF16)
    wg = jnp.pad(w_in[:, nsa_w + kv3:nsa_w + kv3 + ngl], ((0, 0), (0, gpad - ngl))).astype(BF16)
    wu = w_in[:, nsa_w + kv3 + ngl:].astype(BF16)
    gq = jnp.tile(g_qnorm, N_HEADS)[None, :]
    gk = jnp.tile(g_knorm, (1, N_KV))
    g1 = g_norm1[None, :]
    g2 = g_norm2[None, :]
    wn = _rgd(w_out[:nsa_w], 0).astype(BF16)
    wp = w_out[nsa_w:].astype(BF16)
    wpool = w_pool.astype(BF16)
    ps = pool_scale[None, :]
    wc = jnp.einsum('cglde,gh->clgdhe', w_cmp, jnp.eye(N_KV, dtype=F32)).reshape(2, CMP_BLOCK, KV_W, KV_W).astype(BF16)
    pe = jnp.transpose(cmp_pe, (1, 0, 2, 3)).reshape(CMP_BLOCK, 2 * KV_W)
    e_gate = _gate_expand(gpad)
    pwq = peer_wq.astype(BF16)
    psk = peer_subkeys.astype(BF16)
    pu = peer_u.astype(BF16)
    pv = peer_v.astype(BF16)

    mod = adaln(jnp.concatenate([c_prompt, c_sample], axis=0), w_ada, b_ada).reshape(bp + bs, 6, dm)
    mod_p = [mod[:bp, k][:, None, :] for k in range(6)]
    mod_s = [jnp.repeat(mod[bp:, k], ts, axis=0).reshape(bs * ts // tm, tm, dm) for k in range(6)]

    xp = x_prompt.reshape(bp * tp, dm)
    q, kvc, kvs, kvw, gates, u, kvc_t, kvs_t, kvw_t = inproj(xp, mod_p[0], mod_p[1], g1, wq, wkv, wg, wu, gq, gk, tm,
                                                             tp // tm, seq_len=tp)
    kc, vc = compress_prompt(kvc.reshape(bp, tp // CMP_STRIDE, CMP_STRIDE * 2 * KV_W), pe, wc, gk)
    nc = (tp - CMP_BLOCK) // CMP_STRIDE + 1
    kvw3 = kvw.reshape(bp, tp, 2 * KV_W)
    o_nsa = nsa_prompt(q, gates, kc, vc, kvs.reshape(bp, tp, 2 * KV_W), jnp.pad(kvw3, ((0, 0), (WINDOW, 0), (0, 0))),
                       _slope_lanes(tq), _frac_t(nc, tp // SEL_BLOCK, tp // CMP_STRIDE, 40), e_gate, tq, sel_bucket)
    u3 = u.reshape(bp, tp, -1)
    d_pool = pool_diff(jnp.pad(u3, ((0, 0), (POOL_BUF, 0), (0, 0))), 0).reshape(bp * tp, -1)
    x1, h2 = outproj(o_nsa, d_pool, xp, mod_p[2], mod_p[3], mod_p[4], g2, wn, wp, wpool, ps, tm, tp // tm)
    routes = peer_route(h2, pwq, psk, tt)
    y_prompt = peer_dense(h2, pu, pv, *routes, x1, mod_p[5], tt, et, tp // tt).reshape(bp, tp, dm)
    rows_major = lambda a: jnp.transpose(a.reshape(bp, 2, N_KV, HEAD_DIM, -1), (0, 4, 1, 2, 3))
    win_p = rows_major(kvw_t[:, :, -min(WINDOW, tp):])
    pool_p = u3[:, -POOL_BUF:]

    xs = x_sample.reshape(bs * ts, dm)
    q, kvc_s, kvs_s, kvw_s, gates, u = inproj(xs, mod_s[0], mod_s[1], g1, wq, wkv, wg, wu, gq, gk, tm, 1)
    feat_major = lambda c: jnp.transpose(c, (0, 2, 3, 4, 1)).reshape(c.shape[0], 2 * KV_W, c.shape[1])
    kc, vc = compress_sample(page_table, feat_major(cache_kv_cmp), pe, wc, gk)
    nc = (past + ts - CMP_BLOCK) // CMP_STRIDE + 1
    ns = -(-(past + ts) // SEL_BLOCK)
    kvw_s3 = kvw_s.reshape(bs, ts, 2 * KV_W)
    o_nsa = nsa_sample(page_table, q.reshape(bs, ts, -1), gates.reshape(bs, ts, -1), kc, vc,
                       feat_major(cache_kv_slc), kvs_s.reshape(bs, ts, 2 * KV_W), feat_major(cache_kv_win), kvw_s3,
                       _slope_lanes(ts), _frac_t(nc, ns, past // CMP_STRIDE, 40), e_gate).reshape(bs * ts, -1)
    ext = jnp.concatenate([state_pool, u.reshape(bs, ts, -1)], axis=1)
    d_pool = pool_diff(ext, past, bb=8).reshape(bs * ts, -1)
    x1, h2 = outproj(o_nsa, d_pool, xs, mod_s[2], mod_s[3], mod_s[4], g2, wn, wp, wpool, ps, tm, 1)
    routes = peer_route(h2, pwq, psk, tt)
    gate5 = mod_s[5].reshape(bs * ts // tt, tt, dm)
    y_sample = peer_dense(h2, pu, pv, *routes, x1, gate5, tt, et, 1).reshape(bs, ts, dm)
    shp_s = (bs, ts, 2, N_KV, HEAD_DIM)
    wbuf = cache_kv_win.shape[1]
    win_s = jnp.concatenate([cache_kv_win, kvw_s.reshape(shp_s)], axis=1)[:, -wbuf:]
    pool_s = ext[:, -POOL_BUF:]

    return (y_prompt, y_sample, rows_major(kvc_t), rows_major(kvs_t), win_p, pool_p,
            kvc_s.reshape(shp_s), kvs_s.reshape(shp_s), win_s, pool_s)
```

```python
import functools

import numpy as np
import jax
import jax.numpy as jnp
from jax import lax
from jax.experimental import pallas as pl
from jax.experimental.pallas import tpu as pltpu

F32, BF16, I32 = jnp.float32, jnp.bfloat16, jnp.int32

HEAD_DIM = 64
N_KV = 4
GROUP = 4
N_HEADS = N_KV * GROUP
CMP_BLOCK = 32
CMP_STRIDE = 16
SEL_BLOCK = 64
SEL_TOP = 16
WINDOW = 512
POOL_SIZES = (2, 4, 8, 16)
POOL_BUF = max(POOL_SIZES) - 1
PEER_HEADS = 8
N_KEYS = 128
PEER_TOPK = 16
EPS = 1e-6
NEG = -1e30
FORCE_BONUS = 1e4
KV_W = N_KV * HEAD_DIM
VMEM_LIMIT = 56 * 1024 * 1024


def _cp(sem, vmem=VMEM_LIMIT):
    return pltpu.CompilerParams(dimension_semantics=sem, vmem_limit_bytes=vmem)


def _dot(a, b):
    return jnp.dot(a, b, preferred_element_type=F32)


def _nt(a, b):
    return lax.dot_general(a, b, (((1,), (1,)), ((), ())), preferred_element_type=F32)


def _tn(a, b):
    return lax.dot_general(a, b, (((0,), (0,)), ((), ())), preferred_element_type=F32)


def _split_dot(x, w):
    hi = x.astype(BF16)
    r1 = x - hi.astype(F32)
    mid = r1.astype(BF16)
    lo = (r1 - mid.astype(F32)).astype(BF16)
    return _dot(hi, w) + _dot(mid, w) + _dot(lo, w)


def _ones64():
    r = lax.broadcasted_iota(I32, (KV_W, KV_W), 0) // HEAD_DIM
    c = lax.broadcasted_iota(I32, (KV_W, KV_W), 1) // HEAD_DIM
    return (r == c).astype(BF16)


def _rms64(x, ones):
    ms = _split_dot(x * x, ones) * (1.0 / HEAD_DIM)
    return x * lax.rsqrt(ms + EPS)


def _rms_rows(x):
    return x * lax.rsqrt(jnp.mean(x * x, axis=-1, keepdims=True) + EPS)


def _adaln_body(c_ref, w_ref, b_ref, o_ref):
    c = c_ref[...]
    s = (c * jax.nn.sigmoid(c)).astype(BF16)
    o_ref[...] = _dot(s, w_ref[...].astype(BF16)) + b_ref[...]


def adaln(c_all, w_ada, b_ada, tn=1024):
    m, d = c_all.shape
    n = w_ada.shape[1]
    return pl.pallas_call(
        _adaln_body,
        out_shape=jax.ShapeDtypeStruct((m, n), F32),
        grid=(n // tn,),
        in_specs=[pl.BlockSpec((m, d), lambda j: (0, 0)),
                  pl.BlockSpec((d, tn), lambda j: (0, j)),
                  pl.BlockSpec((1, tn), lambda j: (0, j))],
        out_specs=pl.BlockSpec((m, tn), lambda j: (0, j)),
        compiler_params=_cp(("arbitrary",)),
        name="adaln",
    )(c_all, w_ada, b_ada.reshape(1, n))


def _inproj_body(x_ref, sh_ref, sc_ref, g1_ref, wq_ref, wkv_ref, wg_ref, wu_ref, gq_ref, gk_ref,
                 q_ref, kvc_ref, kvs_ref, kvw_ref, gt_ref, u_ref, *t_refs):
    h = _rms_rows(x_ref[...]) * g1_ref[...]
    h = h * (1.0 + sc_ref[0]) + sh_ref[0]
    hb = h.astype(BF16)
    ones = _ones64()
    zq = _dot(hb, wq_ref[...])
    for r in range(GROUP):
        sl = slice(r * KV_W, (r + 1) * KV_W)
        q_ref[:, sl] = _rms64(zq[:, sl], ones) * gq_ref[:, sl] * (HEAD_DIM ** -0.5)
    zkv = _dot(hb, wkv_ref[...])
    kvc_ref[...] = zkv[:, 0:2 * KV_W]
    kvs_ref[:, 0:KV_W] = _rms64(zkv[:, 2 * KV_W:3 * KV_W], ones) * gk_ref[1:2, :]
    kvs_ref[:, KV_W:] = zkv[:, 3 * KV_W:4 * KV_W]
    kvw_ref[:, 0:KV_W] = _rms64(zkv[:, 4 * KV_W:5 * KV_W], ones) * gk_ref[2:3, :]
    kvw_ref[:, KV_W:] = zkv[:, 5 * KV_W:6 * KV_W]
    gt_ref[...] = jax.nn.sigmoid(_dot(hb, wg_ref[...]))
    u_ref[...] = _dot(hb, wu_ref[...])
    for src, dst in zip((kvc_ref, kvs_ref, kvw_ref), t_refs[:3]):
        dst[0] = src[...].T
    for src, dst in zip((kvs_ref, kvw_ref), t_refs[3:]):
        dst[...] = src[...].astype(BF16)


def _mod_spec(arr, tm, rows_per_mod):
    d = arr.shape[-1]
    if arr.shape[1] == 1:
        return pl.BlockSpec((1, 1, d), lambda i: (i // rows_per_mod, 0, 0))
    return pl.BlockSpec((1, tm, d), lambda i: (i, 0, 0))


def inproj(x, shift, scale, g1, wq, wkv, wg, wu, gq, gk, tm, tiles_per_mod, seq_len=None):
    t, d = x.shape
    nq, nkv, ng, nu = wq.shape[1], wkv.shape[1], wg.shape[1], wu.shape[1]
    full = lambda a: pl.BlockSpec(a.shape, lambda i: (0,) * a.ndim)
    row = lambda n: pl.BlockSpec((tm, n), lambda i: (i, 0))
    kvrow = jax.ShapeDtypeStruct((t, 2 * KV_W), F32)
    out_shape = [jax.ShapeDtypeStruct((t, nq), F32), kvrow, kvrow, kvrow,
                 jax.ShapeDtypeStruct((t, ng), F32), jax.ShapeDtypeStruct((t, nu), F32)]
    out_specs = [row(nq), row(2 * KV_W), row(2 * KV_W), row(2 * KV_W), row(ng), row(nu)]
    if seq_len is not None:
        tiles = seq_len // tm
        out_shape += [jax.ShapeDtypeStruct((t // seq_len, 2 * KV_W, seq_len), F32)] * 3
        out_specs += [pl.BlockSpec((1, 2 * KV_W, tm), lambda i: (i // tiles, 0, i % tiles))] * 3
        out_shape += [jax.ShapeDtypeStruct((t, 2 * KV_W), BF16)] * 2
        out_specs += [row(2 * KV_W)] * 2
    return pl.pallas_call(
        _inproj_body,
        out_shape=tuple(out_shape),
        grid=(t // tm,),
        in_specs=[row(d), _mod_spec(shift, tm, tiles_per_mod), _mod_spec(scale, tm, tiles_per_mod), full(g1),
                  full(wq), full(wkv), full(wg), full(wu), full(gq), full(gk)],
        out_specs=tuple(out_specs),
        compiler_params=_cp(("arbitrary",)),
        name="inproj",
    )(x, shift, scale, g1, wq, wkv, wg, wu, gq, gk)


def _compress(get_x, nchunk, pe_ref, w_ref, gk_ref, kc_ref, vc_ref):
    half = CMP_BLOCK // 2
    acc = [jnp.zeros((nchunk, KV_W), F32) for _ in range(4)]
    for l in range(half):
        xl = get_x(l)
        a = (xl + pe_ref[l:l + 1, :]).astype(BF16)
        b = (xl + pe_ref[half + l:half + l + 1, :]).astype(BF16)
        acc[0] += _dot(a[:, :KV_W], w_ref[0, l])
        acc[1] += _dot(a[:, KV_W:], w_ref[1, l])
        acc[2] += _dot(b[:, :KV_W], w_ref[0, half + l])
        acc[3] += _dot(b[:, KV_W:], w_ref[1, half + l])
    rio = lax.broadcasted_iota(I32, (nchunk, KV_W), 0)
    nxt = lambda v: jnp.where(rio < nchunk - 1, pltpu.roll(v, nchunk - 1, 0), 0.0)
    ck = acc[0] + nxt(acc[2])
    cv = acc[1] + nxt(acc[3])
    kc_ref[...] = _rms64(ck, _ones64()) * gk_ref[0:1, :]
    vc_ref[...] = cv


def _compress_prompt_body(x_ref, pe_ref, w_ref, gk_ref, kc_ref, vc_ref):
    row_w = 2 * KV_W
    _compress(lambda l: x_ref[0, :, l * row_w:(l + 1) * row_w], x_ref.shape[1], pe_ref, w_ref, gk_ref,
              kc_ref.at[0], vc_ref.at[0])


def compress_prompt(kv_chunks, pe, wc, gk):
    b, nchunk, width = kv_chunks.shape
    full = lambda a: pl.BlockSpec(a.shape, lambda i: (0,) * a.ndim)
    out = jax.ShapeDtypeStruct((b, nchunk, KV_W), F32)
    ospec = pl.BlockSpec((1, nchunk, KV_W), lambda i: (i, 0, 0))
    return pl.pallas_call(
        _compress_prompt_body,
        out_shape=(out, out),
        grid=(b,),
        in_specs=[pl.BlockSpec((1, nchunk, width), lambda i: (i, 0, 0)), full(pe), full(wc), full(gk)],
        out_specs=(ospec, ospec),
        compiler_params=_cp(("arbitrary",)),
        name="compress_prompt",
    )(kv_chunks, pe, wc, gk)


def _page_copy(cache_ref, buf_ref, sem_ref, pt_ref, seq, slot, j):
    return pltpu.make_async_copy(cache_ref.at[pt_ref[seq, j]], buf_ref.at[slot, j], sem_ref.at[slot])


def _gather_pages(cache_ref, buf_ref, sem_ref, pt_ref):
    b = pl.program_id(0)
    nb = pl.num_programs(0)
    n_pages = pt_ref.shape[1]
    slot = b % 2

    def start(seq, s):
        for j in range(n_pages):
            _page_copy(cache_ref, buf_ref, sem_ref, pt_ref, seq, s, j).start()

    @pl.when(b == 0)
    def _():
        start(b, slot)

    @pl.when(b + 1 < nb)
    def _():
        start(b + 1, 1 - slot)

    for j in range(n_pages):
        _page_copy(cache_ref, buf_ref, sem_ref, pt_ref, b, slot, j).wait()
    return slot


LANES = 128
BF16_ROWS = 16


def _compress_sample_body(pt_ref, cache_ref, pe_ref, w_ref, gk_ref, kc_ref, vc_ref, buf_ref, xs_ref, sem_ref):
    n_pages, page = pt_ref.shape[1], cache_ref.shape[2]
    slot = _gather_pages(cache_ref, buf_ref, sem_ref, pt_ref)
    nlb = xs_ref.shape[0]
    for j in range(n_pages):
        for k in range(nlb):
            xs_ref[k, pl.ds(j * page, page), :] = buf_ref[slot, j, pl.ds(k * LANES, LANES), :].T
    nchunk = n_pages * page // CMP_STRIDE
    get_x = lambda l: jnp.concatenate([xs_ref[k, pl.ds(l, nchunk, stride=CMP_STRIDE), :] for k in range(nlb)], axis=1)
    _compress(get_x, nchunk, pe_ref, w_ref, gk_ref, kc_ref.at[0], vc_ref.at[0])


def compress_sample(page_table, cache_t, pe, wc, gk):
    b, n_pages = page_table.shape
    _, width, page = cache_t.shape
    nchunk = n_pages * page // CMP_STRIDE
    full = lambda a: pl.BlockSpec(a.shape, lambda i, pt: (0,) * a.ndim)
    out = jax.ShapeDtypeStruct((b, nchunk, KV_W), F32)
    ospec = pl.BlockSpec((1, nchunk, KV_W), lambda i, pt: (i, 0, 0))
    return pl.pallas_call(
        _compress_sample_body,
        out_shape=(out, out),
        grid_spec=pltpu.PrefetchScalarGridSpec(
            num_scalar_prefetch=1, grid=(b,),
            in_specs=[pl.BlockSpec(memory_space=pl.ANY), full(pe), full(wc), full(gk)],
            out_specs=(ospec, ospec),
            scratch_shapes=[pltpu.VMEM((2, n_pages, width, page), F32),
                            pltpu.VMEM((width // LANES, n_pages * page, LANES), F32),
                            pltpu.SemaphoreType.DMA((2,))]),
        compiler_params=_cp(("arbitrary",)),
        name="compress_sample",
    )(page_table, cache_t, pe, wc, gk)


def _softmax_cols(s, valid):
    s = jnp.where(valid, s, NEG)
    m = jnp.max(s, axis=0, keepdims=True)
    e = jnp.where(valid, jnp.exp(s - m), 0.0)
    return e / jnp.maximum(jnp.sum(e, axis=0, keepdims=True), 1e-30)


def _nsa_front(q, kc, vc, pq0, slope, frac_t, tq, ns):
    cols = N_HEADS * tq
    lane = lax.broadcasted_iota(I32, (1, cols), 1)
    pq = (pq0 + lane % tq).astype(F32)
    lg = lax.broadcasted_iota(I32, (tq, KV_W), 1) // HEAD_DIM
    qfull = jnp.concatenate([jnp.where(lg == g, q[:, r * KV_W:(r + 1) * KV_W], 0.0)
                             for r in range(GROUP) for g in range(N_KV)], axis=0).astype(BF16)

    nc = kc.shape[0]
    pos_c = (lax.broadcasted_iota(I32, (nc, cols), 0) * CMP_STRIDE + (CMP_BLOCK - 1)).astype(F32)
    dist_c = pq - pos_c
    p_c = _softmax_cols(_nt(kc.astype(BF16), qfull) - slope * dist_c, dist_c >= 0).astype(BF16)
    o_c = _tn(p_c, vc.astype(BF16))

    imp = _dot(frac_t.astype(BF16), p_c)
    imp = imp + pltpu.roll(imp, 4 * tq, 1) + pltpu.roll(imp, 8 * tq, 1) + pltpu.roll(imp, 12 * tq, 1)
    gq = N_KV * tq
    compact = gq % LANES == 0
    wsel = gq if compact else cols
    imp = imp[:, :wsel]
    nsp = imp.shape[0]
    jio = lax.broadcasted_iota(I32, (nsp, wsel), 0)
    pos_q = pq0 + lax.broadcasted_iota(I32, (1, wsel), 1) % tq
    blk_q = pos_q // SEL_BLOCK
    forced = (jio == 0) | (jio == blk_q) | (jio == blk_q - 1)
    valid_blk = jio * SEL_BLOCK <= pos_q
    imp = jnp.where(valid_blk, imp + FORCE_BONUS * forced.astype(F32), NEG)
    rank = jnp.zeros((nsp, wsel), I32)
    for i in range(ns):
        row = imp[i:i + 1, :]
        rank = rank + ((row > imp) | ((row == imp) & (i < jio))).astype(I32)
    sel_bias = jnp.where(rank < min(SEL_TOP, ns), 0.0, NEG)
    if compact:
        sel_bias = jnp.concatenate([sel_bias] * GROUP, axis=1)
    return qfull, pq, o_c, sel_bias


def _as_column(row):
    return jnp.transpose(jnp.broadcast_to(row, (LANES, row.shape[1])))[:, 0:1]


def _nsa_selected(qfull, pq, sel_bias, ks_ref, slope, nblk, blk_q):
    cols = qfull.shape[0]
    nsp = sel_bias.shape[0]
    tk = nblk * SEL_BLOCK
    jio = lax.broadcasted_iota(I32, (nsp, cols), 0)
    bias = jnp.where(jio < blk_q, sel_bias, NEG) + slope * (jio * SEL_BLOCK).astype(F32)
    inblk = slope * lax.broadcasted_iota(I32, (SEL_BLOCK, cols), 0).astype(F32)
    s_m = _nt(ks_ref[0:tk, 0:KV_W].astype(BF16), qfull).reshape(nblk, SEL_BLOCK, cols)
    s_m = s_m + inblk[None, :, :] + bias[:nblk][:, None, :]

    d0 = blk_q * SEL_BLOCK
    kd = ks_ref[pl.ds(d0 if isinstance(d0, int) else pl.multiple_of(d0, SEL_BLOCK), SEL_BLOCK), :]
    kpos_d = (blk_q * SEL_BLOCK + lax.broadcasted_iota(I32, (SEL_BLOCK, cols), 0)).astype(F32)
    valid_d = kpos_d <= pq
    s_d = jnp.where(valid_d, _nt(kd[:, :KV_W].astype(BF16), qfull) + slope * kpos_d, NEG)

    m = jnp.maximum(jnp.max(jnp.max(s_m, axis=0), axis=0, keepdims=True), jnp.max(s_d, axis=0, keepdims=True))
    e_m = jnp.exp(s_m - m[None, :, :])
    e_d = jnp.where(valid_d, jnp.exp(s_d - m), 0.0)
    l = jnp.sum(jnp.sum(e_m, axis=0), axis=0, keepdims=True) + jnp.sum(e_d, axis=0, keepdims=True)
    o = _tn(e_m.astype(BF16).reshape(tk, cols), ks_ref[0:tk, KV_W:].astype(BF16)) + _tn(e_d.astype(BF16),
                                                                                       kd[:, KV_W:].astype(BF16))
    return o * _as_column(1.0 / jnp.maximum(l, 1e-30))


def _nsa_window(qfull, pq, kw, vw, pw0, slope):
    tw, cols = kw.shape[0], qfull.shape[0]
    pos_w = (pw0 + lax.broadcasted_iota(I32, (tw, cols), 0)).astype(F32)
    dist_w = pq - pos_w
    valid_w = (dist_w >= 0) & (dist_w <= WINDOW) & (pos_w >= 0)
    p_w = _softmax_cols(_nt(kw.astype(BF16), qfull) - slope * dist_w, valid_w).astype(BF16)
    return _tn(p_w, vw.astype(BF16))


def _nsa_window_steady(qfull, kw, vw, slope, tq):
    cols = qfull.shape[0]
    nb = WINDOW // tq
    qq = lax.broadcasted_iota(I32, (1, cols), 1) % tq
    rio = lax.broadcasted_iota(I32, (tq, cols), 0)
    s = _nt(kw.astype(BF16), qfull).reshape(nb + 1, tq, cols) + (slope * rio.astype(F32))[None, :, :]
    s = s + (slope * (lax.broadcasted_iota(I32, (nb + 1, cols), 0) * tq).astype(F32))[:, None, :]
    ok_f, ok_l = rio >= qq, rio <= qq
    s_f, s_m, s_l = jnp.where(ok_f, s[0], NEG), s[1:nb], jnp.where(ok_l, s[nb], NEG)
    m = jnp.maximum(jnp.max(jnp.max(s_m, axis=0), axis=0, keepdims=True),
                    jnp.max(jnp.maximum(s_f, s_l), axis=0, keepdims=True))
    e_f = jnp.where(ok_f, jnp.exp(s_f - m), 0.0)
    e_l = jnp.where(ok_l, jnp.exp(s_l - m), 0.0)
    e_m = jnp.exp(s_m - m[None, :, :])
    l = jnp.sum(jnp.sum(e_m, axis=0), axis=0, keepdims=True) + jnp.sum(e_f + e_l, axis=0, keepdims=True)
    v = vw.astype(BF16)
    o = (_tn(e_m.astype(BF16).reshape((nb - 1) * tq, cols), v[tq:nb * tq]) + _tn(e_f.astype(BF16), v[0:tq])
         + _tn(e_l.astype(BF16), v[nb * tq:]))
    return o * _as_column(1.0 / jnp.maximum(l, 1e-30))


def _nsa_combine(gates, e_gate, outs, tq):
    cols = N_HEADS * tq
    row_g = (lax.broadcasted_iota(I32, (cols, KV_W), 0) // tq) % N_KV
    own = row_g == lax.broadcasted_iota(I32, (cols, KV_W), 1) // HEAD_DIM

    def fold(o):
        o = jnp.where(own, o, 0.0)
        return jnp.concatenate(
            [sum(o[(r * N_KV + g) * tq:(r * N_KV + g + 1) * tq, :] for g in range(N_KV)) for r in range(GROUP)], axis=1)

    out = jnp.zeros((tq, N_HEADS * HEAD_DIM), F32)
    for j, o in enumerate(outs):
        out = out + _split_dot(gates, e_gate[j]) * fold(o)
    return out


def _nsa_prompt_body(q_ref, gt_ref, kc_ref, vc_ref, ks_ref, kw_ref, sl_ref, fr_ref, eg_ref, o_ref, os_ref, ow_ref,
                     *, tq, ns, tw, bucket):
    q0 = pl.program_id(1) * tq
    slope = sl_ref[...]
    qfull, pq, o_c, sel_bias = _nsa_front(q_ref[...], kc_ref[0], vc_ref[0], q0, slope, fr_ref[...], tq, ns)
    blk_q = q0 // SEL_BLOCK
    for k in range(ns // bucket):
        @pl.when(blk_q // bucket == k)
        def _():
            os_ref[...] = _nsa_selected(qfull, pq, sel_bias, ks_ref.at[0], slope, (k + 1) * bucket, blk_q)
    w = kw_ref[0, pl.ds(pl.multiple_of(q0, tq), tw), :]

    @pl.when(q0 >= WINDOW)
    def _():
        ow_ref[...] = _nsa_window_steady(qfull, w[:, :KV_W], w[:, KV_W:], slope, tq)

    @pl.when(q0 < WINDOW)
    def _():
        ow_ref[...] = _nsa_window(qfull, pq, w[:, :KV_W], w[:, KV_W:], q0 - WINDOW, slope)

    o_ref[...] = _nsa_combine(gt_ref[...], eg_ref, (o_c, os_ref[...], ow_ref[...]), tq)


def nsa_prompt(q, gates, kc, vc, kvs, kvw_pad, slope, frac_t, e_gate, tq, bucket):
    b, t, _ = kvs.shape
    ns = t // SEL_BLOCK
    assert SEL_BLOCK % tq == 0 and ns % bucket == 0 and t % SEL_BLOCK == 0
    assert WINDOW % tq == 0 and tq % 8 == 0
    tw = WINDOW + tq
    nq = t // tq
    full = lambda a: pl.BlockSpec(a.shape, lambda i, j: (0,) * a.ndim)
    perb = lambda a: pl.BlockSpec((1,) + a.shape[1:], lambda i, j: (i, 0, 0))
    row = lambda n: pl.BlockSpec((tq, n), lambda i, j: (i * nq + j, 0))
    return pl.pallas_call(
        functools.partial(_nsa_prompt_body, tq=tq, ns=ns, tw=tw, bucket=bucket),
        out_shape=jax.ShapeDtypeStruct((b * t, N_HEADS * HEAD_DIM), F32),
        grid=(b, nq),
        in_specs=[row(q.shape[1]), row(gates.shape[1]), perb(kc), perb(vc), perb(kvs), perb(kvw_pad),
                  full(slope), full(frac_t), full(e_gate)],
        out_specs=row(N_HEADS * HEAD_DIM),
        scratch_shapes=[pltpu.VMEM((N_HEADS * tq, KV_W), F32), pltpu.VMEM((N_HEADS * tq, KV_W), F32)],
        compiler_params=_cp(("arbitrary", "arbitrary")),
        name="nsa_prompt",
    )(q, gates, kc, vc, kvs, kvw_pad, slope, frac_t, e_gate)


def _nsa_sample_body(pt_ref, q_ref, gt_ref, kc_ref, vc_ref, cache_ref, knew_ref, win_ref, wnew_ref, sl_ref, fr_ref,
                     eg_ref, o_ref, buf_ref, ks_ref, sem_ref, *, tq, ns, past):
    n_pages, page = pt_ref.shape[1], cache_ref.shape[2]
    tk = ns * SEL_BLOCK

    @pl.when(pl.program_id(0) == 0)
    def _():
        ks_ref[past:, :] = jnp.zeros((tk - past, ks_ref.shape[1]), F32)

    slot = _gather_pages(cache_ref, buf_ref, sem_ref, pt_ref)
    for j in range(n_pages):
        ks_ref[pl.ds(j * page, page), :] = buf_ref[slot, j].T
    ks_ref[past:past + tq, :] = knew_ref[0]
    wbuf = win_ref.shape[2]
    tw = wbuf + SEL_BLOCK
    w = jnp.concatenate([win_ref[0].T, wnew_ref[0], jnp.zeros((tw - wbuf - tq, 2 * KV_W), F32)], axis=0)
    slope = sl_ref[...]
    qfull, pq, o_c, sel_bias = _nsa_front(q_ref[0], kc_ref[0], vc_ref[0], past, slope, fr_ref[...], tq, ns)
    o_s = _nsa_selected(qfull, pq, sel_bias, ks_ref, slope, past // SEL_BLOCK, past // SEL_BLOCK)
    o_w = _nsa_window(qfull, pq, w[:, :KV_W], w[:, KV_W:], past - wbuf, slope)
    o_ref[0] = _nsa_combine(gt_ref[0], eg_ref, (o_c, o_s, o_w), tq)


def nsa_sample(page_table, q, gates, kc, vc, cache_t, kvs_new, win_t, kvw_new, slope, frac_t, e_gate):
    b, n_pages = page_table.shape
    _, width, page = cache_t.shape
    past = n_pages * page
    tq = q.shape[1]
    assert past % SEL_BLOCK + tq <= SEL_BLOCK
    ns = -(-(past + tq) // SEL_BLOCK)
    tk = ns * SEL_BLOCK
    full = lambda a: pl.BlockSpec(a.shape, lambda i, pt: (0,) * a.ndim)
    perb = lambda a: pl.BlockSpec((1,) + a.shape[1:], lambda i, pt: (i, 0, 0))
    return pl.pallas_call(
        functools.partial(_nsa_sample_body, tq=tq, ns=ns, past=past),
        out_shape=jax.ShapeDtypeStruct((b, tq, N_HEADS * HEAD_DIM), F32),
        grid_spec=pltpu.PrefetchScalarGridSpec(
            num_scalar_prefetch=1, grid=(b,),
            in_specs=[perb(q), perb(gates), perb(kc), perb(vc), pl.BlockSpec(memory_space=pl.ANY), perb(kvs_new),
                      perb(win_t), perb(kvw_new), full(slope), full(frac_t), full(e_gate)],
            out_specs=pl.BlockSpec((1, tq, N_HEADS * HEAD_DIM), lambda i, pt: (i, 0, 0)),
            scratch_shapes=[pltpu.VMEM((2, n_pages, width, page), F32), pltpu.VMEM((tk, width), F32),
                            pltpu.SemaphoreType.DMA((2,))]),
        compiler_params=_cp(("arbitrary",)),
        name="nsa_sample",
    )(page_table, q, gates, kc, vc, cache_t, kvs_new, win_t, kvw_new, slope, frac_t, e_gate)


def _pool_diff_body(ext_ref, d_ref, *, t, pos0):
    gw = ext_ref.shape[2] // len(POOL_SIZES)
    pos = pos0 + lax.broadcasted_iota(I32, (t, gw), 0)
    for s in range(ext_ref.shape[0]):
        for gi, w in enumerate(POOL_SIZES):
            lanes = pl.ds(gi * gw, gw)
            cur = ext_ref[s, pl.ds(POOL_BUF, t), lanes]
            acc = cur
            for j in range(1, w):
                acc = acc + ext_ref[s, pl.ds(POOL_BUF - j, t), lanes]
            cnt = jnp.minimum(w, pos + 1).astype(F32)
            d_ref[s, :, lanes] = acc / cnt - cur


def pool_diff(ext, pos0, bb=1):
    b, rows, width = ext.shape
    t = rows - POOL_BUF
    return pl.pallas_call(
        functools.partial(_pool_diff_body, t=t, pos0=pos0),
        out_shape=jax.ShapeDtypeStruct((b, t, width), F32),
        grid=(b // bb,),
        in_specs=[pl.BlockSpec((bb, rows, width), lambda i: (i, 0, 0))],
        out_specs=pl.BlockSpec((bb, t, width), lambda i: (i, 0, 0)),
        compiler_params=_cp(("arbitrary",)),
        name="pool_diff",
    )(ext)


def _outproj_body(o_ref, d_ref, x_ref, gate_ref, sh_ref, sc_ref, g2_ref, wn_ref, wp_ref, wpool_ref, ps_ref,
                  x1_ref, h2_ref):
    gw = wpool_ref.shape[1]
    d = d_ref[...]
    yp = jnp.concatenate([_dot(d[:, g * gw:(g + 1) * gw].astype(BF16), wpool_ref[g]) for g in range(len(POOL_SIZES))],
                         axis=1) * ps_ref[...]
    mix = _dot(o_ref[...].astype(BF16), wn_ref[...]) + _dot(yp.astype(BF16), wp_ref[...])
    x1 = x_ref[...] + gate_ref[0] * mix
    x1_ref[...] = x1
    h2 = _rms_rows(x1) * g2_ref[...]
    h2_ref[...] = (h2 * (1.0 + sc_ref[0]) + sh_ref[0]).astype(BF16)


def outproj(o, d, x, gate, shift, scale, g2, wn, wp, wpool, ps, tm, tiles_per_mod):
    t, dm = x.shape
    full = lambda a: pl.BlockSpec(a.shape, lambda i: (0,) * a.ndim)
    row = lambda n: pl.BlockSpec((tm, n), lambda i: (i, 0))
    ms = lambda a: _mod_spec(a, tm, tiles_per_mod)
    return pl.pallas_call(
        _outproj_body,
        out_shape=(jax.ShapeDtypeStruct((t, dm), F32), jax.ShapeDtypeStruct((t, dm), BF16)),
        grid=(t // tm,),
        in_specs=[row(o.shape[1]), row(d.shape[1]), row(dm), ms(gate), ms(shift), ms(scale), full(g2),
                  full(wn), full(wp), full(wpool), full(ps)],
        out_specs=(row(dm), row(dm)),
        compiler_params=_cp(("arbitrary",)),
        name="outproj",
    )(o, d, x, gate, shift, scale, g2, wn, wp, wpool, ps)


_CAND_PIECES = ((0, 16), (1, 8), (2, 8), (3, 8), (4, 8), (5, 8), (6, 8), (7, 8))


def _top16_rows(s, exact):
    kio = lax.broadcasted_iota(I32, s.shape, 0)
    nk = s.shape[0]
    s0 = s
    rank = jnp.full(s.shape, float(PEER_TOPK), F32)
    vals = []
    for it in range(PEER_TOPK):
        m = jnp.max(s, axis=0, keepdims=True)
        hit = s == m
        if exact:
            hit = kio == jnp.min(jnp.where(hit, kio, nk), axis=0, keepdims=True)
        rank = jnp.where(hit, float(it), rank)
        s = jnp.where(hit, -jnp.inf, s)
        vals.append(m)
    cnt = jnp.sum((s0 >= vals[-1]).astype(F32), axis=0, keepdims=True)
    return vals, rank, cnt


def _route_head(s1, s2, exact):
    v1, rank1, cnt1 = _top16_rows(s1, exact)
    v2, rank2, cnt2 = _top16_rows(s2, exact)
    n = s1.shape[1]
    v2a = jnp.concatenate(v2, axis=0)
    top = v1[0] + v2[0]

    bio = {nb: lax.broadcasted_iota(I32, (nb, n), 0) for nb in (8, PEER_TOPK)}
    v2p = {8: jnp.concatenate(v2[:8], axis=0), PEER_TOPK: v2a}
    cands, flats = [], []
    for a, nb in _CAND_PIECES:
        c = v1[a] + v2p[nb]
        ok = (bio[nb] + 1) * (a + 1) <= PEER_TOPK
        cands.append(jnp.where(ok, c, -jnp.inf))
        flats.append(a * PEER_TOPK + bio[nb])
    v1b = jnp.concatenate(v1[8:], axis=0)
    cands.append(v1b + v2[0])
    flats.append((bio[8] + 8) * PEER_TOPK)
    cand = jnp.concatenate(cands, axis=0)
    flat = jnp.concatenate(flats, axis=0)
    big = PEER_TOPK * PEER_TOPK
    taken = jnp.zeros(cand.shape, jnp.bool_)
    z = jnp.zeros((1, n), F32)
    for _ in range(PEER_TOPK):
        m = jnp.max(cand, axis=0, keepdims=True)
        hit = cand == m
        if exact:
            hit = flat == jnp.min(jnp.where(hit, flat, big), axis=0, keepdims=True)
        taken = taken | hit
        cand = jnp.where(hit, -jnp.inf, cand)
        z = z + jnp.exp(m - top)
    takenf = taken.astype(F32)
    cnt3 = jnp.sum(takenf, axis=0, keepdims=True)

    lim = jnp.zeros(s1.shape, F32)
    off = 0
    for a, nb in _CAND_PIECES:
        lim = jnp.where(rank1 == float(a), jnp.sum(takenf[off:off + nb], axis=0, keepdims=True), lim)
        off += nb
    for i in range(8):
        lim = jnp.where(rank1 == float(8 + i), takenf[off + i:off + i + 1], lim)

    k = float(PEER_TOPK)
    ok = ((cnt1 == k) & (cnt2 == k) & (cnt3 == k)).astype(F32)
    return rank2.astype(BF16), lim, jnp.exp(s1 - v1[0]) / z, jnp.exp(s2 - v2[0]).astype(BF16), ok


def _peer_route_body(h_ref, wq_ref, sk_ref, rk_ref, lim_ref, e1_ref, e2_ref, q_scr):
    step = pl.program_id(1)
    hps, qd = sk_ref.shape[1], sk_ref.shape[3]

    @pl.when(step == 0)
    def _():
        q = _dot(h_ref[...], wq_ref[...]).astype(BF16)
        for i in range(PEER_HEADS):
            q_scr[i] = q[:, i * 2 * qd:(i + 1) * 2 * qd]

    def emit(hh, s1, s2, exact):
        rk, lim, e1, e2, ok = _route_head(s1, s2, exact)
        rk_ref[hh], lim_ref[hh], e1_ref[hh], e2_ref[hh] = rk, lim, e1, e2
        return ok

    heads = []
    for hh in range(hps):
        qh = q_scr[step * hps + hh]
        s1 = _nt(sk_ref[0, hh], qh[:, :qd])
        s2 = _nt(sk_ref[1, hh], qh[:, qd:])
        heads.append((hh, s1, s2, emit(hh, s1, s2, False)))
    for hh, s1, s2, ok in heads:
        @pl.when(jnp.min(ok) < 0.5)
        def _(hh=hh, s1=s1, s2=s2):
            emit(hh, s1, s2, True)


def peer_route(h2, wq, sk, tt):
    t, d = h2.shape
    nk = sk.shape[2]
    out = jax.ShapeDtypeStruct((PEER_HEADS, nk, t), F32)
    outb = jax.ShapeDtypeStruct((PEER_HEADS, nk, t), BF16)
    hps = 2
    ospec = pl.BlockSpec((hps, nk, tt), lambda i, h: (h, 0, i))
    return pl.pallas_call(
        _peer_route_body,
        out_shape=(outb, out, out, outb),
        grid=(t // tt, PEER_HEADS // hps),
        in_specs=[pl.BlockSpec((tt, d), lambda i, h: (i, 0)),
                  pl.BlockSpec(wq.shape, lambda i, h: (0, 0)),
                  pl.BlockSpec((2, hps) + sk.shape[2:], lambda i, h: (0, h, 0, 0))],
        out_specs=(ospec, ospec, ospec, ospec),
        scratch_shapes=[pltpu.VMEM((PEER_HEADS, tt, wq.shape[1] // PEER_HEADS), BF16)],
        compiler_params=_cp(("arbitrary", "arbitrary")),
        name="peer_route",
    )(h2, wq, sk)


def _peer_dense_body(h_ref, u_ref, v_ref, rk_ref, lim_ref, e1_ref, e2_ref, x1_ref, gate_ref, y_ref, w_scr, lb_scr,
                     eb_scr):
    j = pl.program_id(1)
    nk = rk_ref.shape[1]
    et, tt = u_ref.shape[0], h_ref.shape[0]
    rows = lb_scr.shape[2]

    @pl.when(j == 0)
    def _():
        y_ref[...] = jnp.zeros_like(y_ref)

    zero = jnp.zeros((), BF16)
    for rr in range(et // nk):
        r = j * (et // nk) + rr
        for hd in range(PEER_HEADS):
            lb_scr[rr, hd] = jnp.broadcast_to(lim_ref[hd, pl.ds(r, 1), :], (rows, tt)).astype(BF16)
            eb_scr[rr, hd] = jnp.broadcast_to(0.5 * e1_ref[hd, pl.ds(r, 1), :], (rows, tt)).astype(BF16)
    nr = et // nk
    for ch in range(nk // rows):
        sl = pl.ds(ch * rows, rows)
        ws = [jnp.zeros((rows, tt), BF16) for _ in range(nr)]
        for hd in range(PEER_HEADS):
            rk, e2 = rk_ref[hd, sl, :], e2_ref[hd, sl, :]
            for rr in range(nr):
                ws[rr] = ws[rr] + jnp.where(rk < lb_scr[rr, hd], e2, zero) * eb_scr[rr, hd]
        for rr in range(nr):
            w_scr[pl.ds(rr * nk + ch * rows, rows), :] = ws[rr]
    a = _nt(u_ref[...], h_ref[...])
    act2 = a * (lax.erf(a * np.float32(1.0 / np.sqrt(2.0))) + 1.0)
    wt = (act2 * w_scr[...].astype(F32)).astype(BF16)
    y_ref[...] += _tn(wt, v_ref[...])

    @pl.when(j == pl.num_programs(1) - 1)
    def _():
        y_ref[...] = x1_ref[...] + gate_ref[0] * y_ref[...]


def peer_dense(h2, u, v, rk, lim, e1, e2, x1, gate, tt, et, tiles_per_mod):
    t, d = h2.shape
    ne = u.shape[0]
    nk = rk.shape[1]
    once = pl.Buffered(1)
    tok = pl.BlockSpec((tt, d), lambda i, j: (i, 0), pipeline_mode=once)
    exp = pl.BlockSpec((et, d), lambda i, j: (j, 0))
    rt = pl.BlockSpec((PEER_HEADS, nk, tt), lambda i, j: (0, 0, i), pipeline_mode=once)
    if gate.shape[1] == 1:
        gspec = pl.BlockSpec((1, 1, d), lambda i, j: (i // tiles_per_mod, 0, 0), pipeline_mode=once)
    else:
        gspec = pl.BlockSpec((1, tt, d), lambda i, j: (i, 0, 0), pipeline_mode=once)
    return pl.pallas_call(
        _peer_dense_body,
        out_shape=jax.ShapeDtypeStruct((t, d), F32),
        grid=(t // tt, ne // et),
        in_specs=[tok, exp, exp, rt, rt, rt, rt, tok, gspec],
        out_specs=pl.BlockSpec((tt, d), lambda i, j: (i, 0)),
        scratch_shapes=[pltpu.VMEM((et, tt), BF16), pltpu.VMEM((et // nk, PEER_HEADS, BF16_ROWS, tt), BF16),
                        pltpu.VMEM((et // nk, PEER_HEADS, BF16_ROWS, tt), BF16)],
        compiler_params=_cp(("arbitrary", "arbitrary")),
        name="peer_dense",
    )(h2, u, v, rk, lim, e1, e2, x1, gate)


def _slope_lanes(tq):
    h = np.arange(1, N_HEADS + 1, dtype=np.float32)
    s = (2.0 ** (-8.0 * h / N_HEADS)).reshape(N_KV, GROUP)
    return jnp.asarray(np.repeat(s.T.reshape(-1), tq)[None, :], F32)


def _frac_t(nc, ns, nc_pad, ns_pad):
    pos = np.arange(nc)[:, None] * CMP_STRIDE + np.arange(CMP_BLOCK)[None, :]
    f = ((pos // SEL_BLOCK)[:, :, None] == np.arange(ns)[None, None, :]).mean(axis=1)
    out = np.zeros((ns_pad, nc_pad), np.float32)
    out[:ns, :nc] = f.T
    return jnp.asarray(out)


def _gate_expand(width):
    e = np.zeros((3, width, N_HEADS * HEAD_DIM), np.float32)
    for g in range(N_KV):
        for r in range(GROUP):
            for j in range(3):
                c0 = r * KV_W + g * HEAD_DIM
                e[j, (g * GROUP + r) * 3 + j, c0:c0 + HEAD_DIM] = 1.0
    return jnp.asarray(e, BF16)


def _rgd(a, axis):
    shp = a.shape
    a = a.reshape(shp[:axis] + (N_KV, GROUP, HEAD_DIM) + shp[axis + 1:])
    a = jnp.swapaxes(a, axis, axis + 1)
    return a.reshape(shp)


def kernel(x_prompt, x_sample, cache_kv_cmp, cache_kv_slc, cache_kv_win, state_pool, page_table, c_prompt, c_sample,
           w_ada, b_ada, g_norm1, g_norm2, w_in, w_out, g_qnorm, g_knorm, cmp_pe, w_cmp, w_pool, pool_scale,
           peer_wq, peer_subkeys, peer_u, peer_v):
    bp, tp, dm = x_prompt.shape
    bs, ts, _ = x_sample.shape
    n_pages = page_table.shape[1]
    page = cache_kv_cmp.shape[1]
    past = n_pages * page
    nsa_w = N_HEADS * HEAD_DIM
    kv3 = 3 * 2 * KV_W
    ngl = 3 * N_HEADS
    gpad = 128
    tm, tq, sel_bucket, tt, et = 256, 64, 4, 512, 1024

    wq = _rgd(w_in[:, :nsa_w], 1).astype(BF16)
    wkv = w_in[:, nsa_w:nsa_w + kv3].astype(BF16)
    wg = jnp.pad(w_in[:, nsa_w + kv3:nsa_w + kv3 + ngl], ((0, 0), (0, gpad - ngl))).astype(BF16)
    wu = w_in[:, nsa_w + kv3 + ngl:].astype(BF16)
    gq = jnp.tile(g_qnorm, N_HEADS)[None, :]
    gk = jnp.tile(g_knorm, (1, N_KV))
    g1 = g_norm1[None, :]
    g2 = g_norm2[None, :]
    wn = _rgd(w_out[:nsa_w], 0).astype(BF16)
    wp = w_out[nsa_w:].astype(BF16)
    wpool = w_pool.astype(BF16)
    ps = pool_scale[None, :]
    wc = jnp.einsum('cglde,gh->clgdhe', w_cmp, jnp.eye(N_KV, dtype=F32)).reshape(2, CMP_BLOCK, KV_W, KV_W).astype(BF16)
    pe = jnp.transpose(cmp_pe, (1, 0, 2, 3)).reshape(CMP_BLOCK, 2 * KV_W)
    e_gate = _gate_expand(gpad)
    pwq = peer_wq.astype(BF16)
    psk = peer_subkeys.astype(BF16)
    pu = peer_u.astype(BF16)
    pv = peer_v.astype(BF16)

    mod = adaln(jnp.concatenate([c_prompt, c_sample], axis=0), w_ada, b_ada).reshape(bp + bs, 6, dm)
    mod_p = [mod[:bp, k][:, None, :] for k in range(6)]
    mod_s = [jnp.repeat(mod[bp:, k], ts, axis=0).reshape(bs * ts // tm, tm, dm) for k in range(6)]

    xp = x_prompt.reshape(bp * tp, dm)
    q, kvc, kvs, kvw, gates, u, kvc_t, kvs_t, kvw_t, kvs_b, kvw_b = inproj(
        xp, mod_p[0], mod_p[1], g1, wq, wkv, wg, wu, gq, gk, tm, tp // tm, seq_len=tp)
    kc, vc = compress_prompt(kvc.reshape(bp, tp // CMP_STRIDE, CMP_STRIDE * 2 * KV_W), pe, wc, gk)
    nc = (tp - CMP_BLOCK) // CMP_STRIDE + 1
    kvw_pad = jnp.pad(kvw_b.reshape(bp, tp, 2 * KV_W), ((0, 0), (WINDOW, 0), (0, 0)))
    o_nsa = nsa_prompt(q, gates, kc, vc, kvs_b.reshape(bp, tp, 2 * KV_W), kvw_pad,
                       _slope_lanes(tq), _frac_t(nc, tp // SEL_BLOCK, tp // CMP_STRIDE, 40), e_gate, tq, sel_bucket)
    u3 = u.reshape(bp, tp, -1)
    d_pool = pool_diff(jnp.pad(u3, ((0, 0), (POOL_BUF, 0), (0, 0))), 0).reshape(bp * tp, -1)
    x1, h2 = outproj(o_nsa, d_pool, xp, mod_p[2], mod_p[3], mod_p[4], g2, wn, wp, wpool, ps, tm, tp // tm)
    routes = peer_route(h2, pwq, psk, tt)
    y_prompt = peer_dense(h2, pu, pv, *routes, x1, mod_p[5], tt, et, tp // tt).reshape(bp, tp, dm)
    rows_major = lambda a: jnp.transpose(a.reshape(bp, 2, N_KV, HEAD_DIM, -1), (0, 4, 1, 2, 3))
    win_p = rows_major(kvw_t[:, :, -min(WINDOW, tp):])
    pool_p = u3[:, -POOL_BUF:]

    xs = x_sample.reshape(bs * ts, dm)
    q, kvc_s, kvs_s, kvw_s, gates, u = inproj(xs, mod_s[0], mod_s[1], g1, wq, wkv, wg, wu, gq, gk, tm, 1)
    feat_major = lambda c: jnp.transpose(c, (0, 2, 3, 4, 1)).reshape(c.shape[0], 2 * KV_W, c.shape[1])
    kc, vc = compress_sample(page_table, feat_major(cache_kv_cmp), pe, wc, gk)
    nc = (past + ts - CMP_BLOCK) // CMP_STRIDE + 1
    ns = -(-(past + ts) // SEL_BLOCK)
    kvw_s3 = kvw_s.reshape(bs, ts, 2 * KV_W)
    o_nsa = nsa_sample(page_table, q.reshape(bs, ts, -1), gates.reshape(bs, ts, -1), kc, vc,
                       feat_major(cache_kv_slc), kvs_s.reshape(bs, ts, 2 * KV_W), feat_major(cache_kv_win), kvw_s3,
                       _slope_lanes(ts), _frac_t(nc, ns, past // CMP_STRIDE, 40), e_gate).reshape(bs * ts, -1)
    ext = jnp.concatenate([state_pool, u.reshape(bs, ts, -1)], axis=1)
    d_pool = pool_diff(ext, past, bb=8).reshape(bs * ts, -1)
    x1, h2 = outproj(o_nsa, d_pool, xs, mod_s[2], mod_s[3], mod_s[4], g2, wn, wp, wpool, ps, tm, 1)
    routes = peer_route(h2, pwq, psk, tt)
    gate5 = mod_s[5].reshape(bs * ts // tt, tt, dm)
    y_sample = peer_dense(h2, pu, pv, *routes, x1, gate5, tt, et, 1).reshape(bs, ts, dm)
    shp_s = (bs, ts, 2, N_KV, HEAD_DIM)
    wbuf = cache_kv_win.shape[1]
    win_s = jnp.concatenate([cache_kv_win, kvw_s.reshape(shp_s)], axis=1)[:, -wbuf:]
    pool_s = ext[:, -POOL_BUF:]

    return (y_prompt, y_sample, rows_major(kvc_t), rows_major(kvs_t), win_p, pool_p,
            kvc_s.reshape(shp_s), kvs_s.reshape(shp_s), win_s, pool_s)
```
